```python
import jax, jax.numpy as jnp
from jax import lax
import numpy as np

D_MODEL = 1024
BATCH = 8
SEQ = 4096
DEPTH = 1
DEC_BATCH = 2
DEC_SEQ = 8192
PAST_LEN = 128

D_MIX = D_MODEL
D_ATT = D_MIX // 2
D_POOL = D_MIX - D_ATT
N_HEADS = 8
QK_NOPE = 64
QK_ROPE = 32
V_DIM = D_ATT // N_HEADS
Q_LORA = 384
KV_LORA = 256
ROPE_THETA = 10000.0
Q_BLOCK = 128
POOL_WINDOWS = (2, 4, 8, 16)
N_POOL_GROUPS = len(POOL_WINDOWS)
POOL_GROUP = D_POOL // N_POOL_GROUPS
N_EXPERTS = 16
CAPACITY_FACTOR = 2
EXPERT_FF = 1024
D_IN = Q_LORA + KV_LORA + QK_ROPE + D_POOL
N_MOD = 6
EPS = 1e-6

kernel_name = "hymba_mla_pool_ec_moe_adaln_encoder"


def rmsnorm(x, gain=None):
    x32 = x.astype(jnp.float32)
    y = (x32 * lax.rsqrt(jnp.mean(x32 * x32, axis=-1, keepdims=True) + EPS)).astype(x.dtype)
    return y if gain is None else y * gain


def rope_tables(s, dtype):
    inv = ROPE_THETA ** (-jnp.arange(0, QK_ROPE, 2, dtype=jnp.float32) / QK_ROPE)
    ang = jnp.arange(s, dtype=jnp.float32)[:, None] * inv[None, :]
    return jnp.cos(ang)[:, None, :].astype(dtype), jnp.sin(ang)[:, None, :].astype(dtype)


def apply_rope(x, cos, sin):
    x1, x2 = jnp.split(x, 2, axis=-1)
    return jnp.concatenate([x1 * cos - x2 * sin, x2 * cos + x1 * sin], axis=-1)


def latent_attention(c_q, c_kv, k_pe, q_norm_g, kv_norm_g, w_uq, w_ukv):
    b, s, _ = c_q.shape
    q = (rmsnorm(c_q, q_norm_g) @ w_uq).reshape(b, s, N_HEADS, QK_NOPE + QK_ROPE)
    kv = (rmsnorm(c_kv, kv_norm_g) @ w_ukv).reshape(b, s, N_HEADS, QK_NOPE + V_DIM)
    cos, sin = rope_tables(s, c_q.dtype)
    q = jnp.concatenate([q[..., :QK_NOPE], apply_rope(q[..., QK_NOPE:], cos, sin)], axis=-1)
    k_pe = apply_rope(k_pe[:, :, None, :], cos, sin)
    k = jnp.concatenate([kv[..., :QK_NOPE],
                         jnp.broadcast_to(k_pe, (b, s, N_HEADS, QK_ROPE))], axis=-1)
    v = kv[..., QK_NOPE:]
    scale = (QK_NOPE + QK_ROPE) ** -0.5
    n_blk = s // Q_BLOCK
    q_blocks = (q * scale).reshape(b, n_blk, Q_BLOCK, N_HEADS, QK_NOPE + QK_ROPE).transpose(1, 0, 2, 3, 4)

    def attend(qb):
        scores = jnp.einsum('bqhd,bkhd->bhqk', qb, k, preferred_element_type=jnp.float32)
        p = jax.nn.softmax(scores, axis=-1).astype(v.dtype)
        return jnp.einsum('bhqk,bkhd->bqhd', p, v)

    o = lax.map(attend, q_blocks)
    return o.transpose(1, 0, 2, 3, 4).reshape(b, s, N_HEADS * V_DIM)


def multiscale_pool(u, w_pool, pool_scale):
    b, s, _ = u.shape
    ug = u.reshape(b, s, N_POOL_GROUPS, POOL_GROUP)
    cs = jnp.pad(jnp.cumsum(ug.astype(jnp.float32), axis=1), ((0, 0), (1, 0), (0, 0), (0, 0)))
    t = jnp.arange(s)
    means = []
    for g, w in enumerate(POOL_WINDOWS):
        lo = jnp.clip(t - w // 2, 0, s)
        hi = jnp.clip(t + (w - w // 2), 0, s)
        cs_g = cs[:, :, g, :]
        count = (hi - lo).astype(jnp.float32)[None, :, None]
        means.append((cs_g[:, hi] - cs_g[:, lo]) / count)
    mean = jnp.stack(means, axis=2)
    d = (mean - ug.astype(jnp.float32)).astype(u.dtype)
    y = jnp.einsum('bsgc,gcd->bsgd', d, w_pool).reshape(b, s, D_POOL)
    return y * pool_scale


def expert_choice_ffn(h, w_router, w_gate, w_up, w_down):
    b, s, d = h.shape
    n = b * s
    cap = CAPACITY_FACTOR * n // N_EXPERTS
    xf = h.reshape(n, d)
    logits = jnp.einsum('nd,de->ne', xf, w_router, preferred_element_type=jnp.float32)
    aff = jax.nn.softmax(logits, axis=-1)
    gates, idx = lax.top_k(aff.T, cap)
    xe = xf[idx]
    a = jnp.einsum('ecd,edf->ecf', xe, w_gate)
    u = jnp.einsum('ecd,edf->ecf', xe, w_up)
    ye = jnp.einsum('ecf,efd->ecd', jax.nn.silu(a) * u, w_down) * gates[..., None].astype(h.dtype)
    out = jnp.zeros_like(xf).at[idx.reshape(-1)].add(ye.reshape(-1, d))
    return out.reshape(b, s, d)


def encoder_layer(x, c, w_ada, b_ada, w_in, q_norm_g, kv_norm_g, w_uq, w_ukv,
                  w_pool, pool_scale, w_out, w_router, w_gate, w_up, w_down):
    mod = jax.nn.silu(c) @ w_ada + b_ada
    sh1, sc1, g1, sh2, sc2, g2 = jnp.split(mod[:, None, :], N_MOD, axis=-1)
    h = rmsnorm(x) * (1 + sc1) + sh1
    z = h @ w_in
    c_q, c_kv, k_pe, u = jnp.split(z, [Q_LORA, Q_LORA + KV_LORA, Q_LORA + KV_LORA + QK_ROPE], axis=-1)
    att = latent_attention(c_q, c_kv, k_pe, q_norm_g, kv_norm_g, w_uq, w_ukv)
    pool = multiscale_pool(u, w_pool, pool_scale)
    x = x + g1 * (jnp.concatenate([att, pool], axis=-1) @ w_out)
    h = rmsnorm(x) * (1 + sc2) + sh2
    return x + g2 * expert_choice_ffn(h, w_router, w_gate, w_up, w_down)


def setup_inputs(seed: int = 0) -> dict:
    key = jax.random.key(seed)
    ks = jax.random.split(key, 20)
    f32 = jnp.float32
    L = DEPTH

    def nrm(k, shape, scale):
        return jax.random.normal(k, shape, f32) * scale

    mod_offset = jnp.array([0.0, 0.0, 1.0, 0.0, 0.0, 1.0], f32)[None, :, None]
    b_ada = (nrm(ks[5], (L, N_MOD, D_MODEL), 0.02) + mod_offset).reshape(L, N_MOD * D_MODEL)
    return {
        "x_prompt": nrm(ks[0], (BATCH, SEQ, D_MODEL), 1.0),
        "x_sample": nrm(ks[1], (DEC_BATCH, DEC_SEQ, D_MODEL), 1.0),
        "c_prompt": nrm(ks[2], (BATCH, D_MODEL), 1.0),
        "c_sample": nrm(ks[3], (DEC_BATCH, D_MODEL), 1.0),
        "w_ada": nrm(ks[4], (L, D_MODEL, N_MOD * D_MODEL), 0.1 * D_MODEL ** -0.5),
        "b_ada": b_ada,
        "w_in": nrm(ks[6], (L, D_MODEL, D_IN), D_MODEL ** -0.5),
        "q_norm_g": 1.0 + nrm(ks[7], (L, Q_LORA), 0.02),
        "kv_norm_g": 1.0 + nrm(ks[8], (L, KV_LORA), 0.02),
        "w_uq": nrm(ks[9], (L, Q_LORA, N_HEADS * (QK_NOPE + QK_ROPE)), Q_LORA ** -0.5),
        "w_ukv": nrm(ks[10], (L, KV_LORA, N_HEADS * (QK_NOPE + V_DIM)), KV_LORA ** -0.5),
        "w_pool": nrm(ks[11], (L, N_POOL_GROUPS, POOL_GROUP, POOL_GROUP), POOL_GROUP ** -0.5),
        "pool_scale": 1.0 + nrm(ks[12], (L, D_POOL), 0.1),
        "w_out": nrm(ks[13], (L, D_MIX, D_MODEL), D_MIX ** -0.5),
        "w_router": nrm(ks[14], (L, D_MODEL, N_EXPERTS), D_MODEL ** -0.5),
        "w_gate": nrm(ks[15], (L, N_EXPERTS, D_MODEL, EXPERT_FF), D_MODEL ** -0.5),
        "w_up": nrm(ks[16], (L, N_EXPERTS, D_MODEL, EXPERT_FF), D_MODEL ** -0.5),
        "w_down": nrm(ks[17], (L, N_EXPERTS, EXPERT_FF, D_MODEL), EXPERT_FF ** -0.5),
        "final_norm_g": 1.0 + nrm(ks[18], (D_MODEL,), 0.02),
    }


def reference(x_prompt, x_sample, c_prompt, c_sample, w_ada, b_ada, w_in, q_norm_g, kv_norm_g,
              w_uq, w_ukv, w_pool, pool_scale, w_out, w_router, w_gate, w_up, w_down, final_norm_g):
    def trunk(x, c):
        for l in range(DEPTH):
            x = encoder_layer(x, c, w_ada[l], b_ada[l], w_in[l], q_norm_g[l], kv_norm_g[l],
                              w_uq[l], w_ukv[l], w_pool[l], pool_scale[l], w_out[l],
                              w_router[l], w_gate[l], w_up[l], w_down[l])
        return rmsnorm(x, final_norm_g)

    y_prompt = trunk(x_prompt, c_prompt)
    y_sample = trunk(x_sample, c_sample)
    return (y_prompt, y_sample)
```

```python
import functools

import jax
import jax.numpy as jnp
import numpy as np
from jax import lax
from jax.experimental import pallas as pl
from jax.experimental.pallas import tpu as pltpu

F32 = jnp.float32
BF16 = jnp.bfloat16
I32 = jnp.int32

D_MODEL = 1024
N_HEADS = 8
QK_NOPE = 64
QK_ROPE = 32
V_DIM = 64
Q_LORA = 384
KV_LORA = 256
ROPE_THETA = 10000.0
D_ATT = N_HEADS * V_DIM
D_POOL = 512
POOL_WINDOWS = (2, 4, 8, 16)
POOL_GROUP = 128
N_EXPERTS = 16
CAPACITY_FACTOR = 2
EXPERT_FF = 1024
N_MOD = 6
EPS = 1e-6

LANES = 128
HEAD_PAD = LANES
HALF_ROPE = QK_ROPE // 2
D_IN_PAD = Q_LORA + KV_LORA + LANES + D_POOL
X_WIDTH = D_MODEL + LANES
POOL_HALO = 8
BF16_ROWS = 16

TS_PRE = 512
TQ_ATT = 256
TK_ATT = 512
TT_MOE = 256
CH_DISP = 64
CR_COMB = 128
TC_FFN = 512
PLAN_ROWS = 512
VMEM_LIMIT = 56 * 1024 * 1024


def _cparams(sem):
    return pltpu.CompilerParams(dimension_semantics=sem, vmem_limit_bytes=VMEM_LIMIT)


def _rms(x):
    return x * lax.rsqrt(jnp.mean(x * x, axis=-1, keepdims=True) + EPS)


def _ada_kernel(c_ref, w_ref, b_ref, o_ref):
    c = c_ref[...]
    s = c * jax.nn.sigmoid(c)
    o_ref[...] = jnp.dot(s, w_ref[...], preferred_element_type=F32,
                         precision=lax.Precision.HIGHEST) + b_ref[...]


def _ada(c_pad, w_ada, b_ada):
    rows = c_pad.shape[0]
    return pl.pallas_call(
        _ada_kernel,
        out_shape=jax.ShapeDtypeStruct((rows, N_MOD * D_MODEL), F32),
        grid=(N_MOD,),
        in_specs=[pl.BlockSpec((rows, D_MODEL), lambda j: (0, 0)),
                  pl.BlockSpec((D_MODEL, D_MODEL), lambda j: (0, j)),
                  pl.BlockSpec((1, D_MODEL), lambda j: (0, j))],
        out_specs=pl.BlockSpec((rows, D_MODEL), lambda j: (0, j)),
        compiler_params=_cparams(("arbitrary",)),
        name="ada",
    )(c_pad, w_ada, b_ada)


def _pre_kernel(x_ref, mod_ref, win_ref, qg_ref, kvg_ref, wq_ref, wkv_ref,
                cq_ref, sq_ref, ck_ref, sk_ref, q_ref, k_ref, v_ref, u_ref):
    x = x_ref[0]
    sh1 = mod_ref[0, 0:1, :]
    sc1 = mod_ref[0, 1:2, :]
    h = (_rms(x) * (1.0 + sc1) + sh1).astype(BF16)
    z = jnp.dot(h, win_ref[...], preferred_element_type=F32)
    c_q = z[:, :Q_LORA]
    c_kv = z[:, Q_LORA:Q_LORA + KV_LORA]
    kpe = z[:, Q_LORA + KV_LORA:Q_LORA + KV_LORA + LANES]
    u_ref[0] = z[:, Q_LORA + KV_LORA + LANES:]

    q = jnp.dot((_rms(c_q) * qg_ref[...]).astype(BF16), wq_ref[...],
                preferred_element_type=F32)
    cq_t = cq_ref[...]
    sq_t = sq_ref[...]
    for hd in range(N_HEADS):
        blk = q[:, hd * HEAD_PAD:(hd + 1) * HEAD_PAD]
        rot = pltpu.roll(blk, HEAD_PAD - HALF_ROPE, 1)
        q_ref[0, :, hd * HEAD_PAD:(hd + 1) * HEAD_PAD] = (blk * cq_t + rot * sq_t).astype(BF16)

    kv = jnp.dot((_rms(c_kv) * kvg_ref[...]).astype(BF16), wkv_ref[...],
                 preferred_element_type=F32)
    kpe_r = kpe * ck_ref[...] + pltpu.roll(kpe, HEAD_PAD - HALF_ROPE, 1) * sk_ref[...]
    for hd in range(N_HEADS):
        k_ref[0, :, hd * HEAD_PAD:(hd + 1) * HEAD_PAD] = (
            kv[:, hd * HEAD_PAD:(hd + 1) * HEAD_PAD] + kpe_r).astype(BF16)
    v_ref[0] = kv[:, N_HEADS * HEAD_PAD:].astype(BF16)


def _pre(x, mod, win, qg, kvg, wq, wkv, tabs):
    b, s, _ = x.shape
    ts = TS_PRE
    const = lambda shape: pl.BlockSpec(shape, lambda bi, i: (0,) * len(shape))
    tab = pl.BlockSpec((ts, LANES), lambda bi, i: (i, 0))
    tok = lambda w: pl.BlockSpec((1, ts, w), lambda bi, i: (bi, i, 0))
    return pl.pallas_call(
        _pre_kernel,
        out_shape=(jax.ShapeDtypeStruct((b, s, N_HEADS * HEAD_PAD), BF16),
                   jax.ShapeDtypeStruct((b, s, N_HEADS * HEAD_PAD), BF16),
                   jax.ShapeDtypeStruct((b, s, D_ATT), BF16),
                   jax.ShapeDtypeStruct((b, s, D_POOL), F32)),
        grid=(b, s // ts),
        in_specs=[tok(D_MODEL),
                  pl.BlockSpec((1, N_MOD, D_MODEL), lambda bi, i: (bi, 0, 0)),
                  const((D_MODEL, D_IN_PAD)), const((1, Q_LORA)), const((1, KV_LORA)),
                  const((Q_LORA, N_HEADS * HEAD_PAD)),
                  const((KV_LORA, N_HEADS * HEAD_PAD + D_ATT)),
                  tab, tab, tab, tab],
        out_specs=(tok(N_HEADS * HEAD_PAD), tok(N_HEADS * HEAD_PAD), tok(D_ATT), tok(D_POOL)),
        compiler_params=_cparams(("parallel", "arbitrary")),
        name="pre",
    )(x, mod, win, qg, kvg, wq, wkv, *tabs)


def _attn_kernel(q_ref, k_ref, v_ref, o_ref, *, seq, tq, tk):
    outs = []
    for hh in range(2):
        qh = q_ref[0, :, hh * HEAD_PAD:(hh + 1) * HEAD_PAD]

        def body(c, carry, hh=hh, qh=qh):
            m, l, acc = carry
            off = pl.multiple_of(c * tk, tk)
            kc = k_ref[0, pl.ds(off, tk), hh * HEAD_PAD:(hh + 1) * HEAD_PAD]
            s = lax.dot_general(qh, kc, (((1,), (1,)), ((), ())),
                                preferred_element_type=F32)
            m_new = jnp.maximum(m, jnp.max(s, axis=-1, keepdims=True))
            alpha = jnp.exp(m - m_new)
            p = jnp.exp(s - m_new)
            l = alpha * l + jnp.sum(p, axis=-1, keepdims=True)
            vc = v_ref[0, pl.ds(off, tk), :]
            acc = alpha * acc + jnp.dot(p.astype(BF16), vc, preferred_element_type=F32)
            return m_new, l, acc

        init = (jnp.full((tq, 1), -jnp.inf, F32), jnp.zeros((tq, 1), F32),
                jnp.zeros((tq, 2 * V_DIM), F32))
        _, l, acc = lax.fori_loop(0, seq // tk, body, init)
        outs.append(acc / l)
    lane = lax.broadcasted_iota(I32, (tq, 2 * V_DIM), 1)
    o_ref[0] = jnp.where(lane < V_DIM, outs[0], outs[1]).astype(BF16)


def _attn(q, k, v):
    b, s, _ = q.shape
    tq, tk = TQ_ATT, TK_ATT
    return pl.pallas_call(
        functools.partial(_attn_kernel, seq=s, tq=tq, tk=tk),
        out_shape=jax.ShapeDtypeStruct((b, s, D_ATT), BF16),
        grid=(b, N_HEADS // 2, s // tq),
        in_specs=[pl.BlockSpec((1, tq, 2 * HEAD_PAD), lambda bi, p, i: (bi, i, p)),
                  pl.BlockSpec((1, s, 2 * HEAD_PAD), lambda bi, p, i: (bi, 0, p)),
                  pl.BlockSpec((1, s, 2 * V_DIM), lambda bi, p, i: (bi, 0, p))],
        out_specs=pl.BlockSpec((1, tq, 2 * V_DIM), lambda bi, p, i: (bi, i, p)),
        compiler_params=_cparams(("parallel", "parallel", "arbitrary")),
        name="attn",
    )(q, k, v)


def _mix_kernel(x_ref, o_ref, u_ref, up_ref, un_ref, mod_ref, wp_ref, ps_ref, woa_ref, wob_ref,
                wr_ref, x1_ref, hx_ref, aff_ref, ue_ref, *, seq, ts):
    i = pl.program_id(1)
    nt = pl.num_programs(1)
    hb = POOL_HALO
    ue_ref[0:hb, :] = jnp.where(i > 0, up_ref[0], 0.0)
    ue_ref[hb:hb + ts, :] = u_ref[0]
    ue_ref[hb + ts:2 * hb + ts, :] = jnp.where(i < nt - 1, un_ref[0], 0.0)

    t = i * ts + lax.broadcasted_iota(I32, (ts, 1), 0)
    ys = []
    for g, w in enumerate(POOL_WINDOWS):
        cols = slice(g * POOL_GROUP, (g + 1) * POOL_GROUP)
        acc = None
        for off in range(-(w // 2), w - w // 2):
            sl = ue_ref[hb + off:hb + off + ts, cols]
            acc = sl if acc is None else acc + sl
        lo = jnp.maximum(t - w // 2, 0)
        hi = jnp.minimum(t + (w - w // 2), seq)
        cnt = (hi - lo).astype(F32)
        d = (acc / cnt - ue_ref[hb:hb + ts, cols]).astype(BF16)
        y = jnp.dot(d, wp_ref[g], preferred_element_type=F32) * ps_ref[:, cols]
        ys.append(y.astype(BF16))
    pool = jnp.concatenate(ys, axis=-1)
    mixed = (jnp.dot(o_ref[0], woa_ref[...], preferred_element_type=F32)
             + jnp.dot(pool, wob_ref[...], preferred_element_type=F32))
    g1 = mod_ref[0, 2:3, :]
    sh2 = mod_ref[0, 3:4, :]
    sc2 = mod_ref[0, 4:5, :]
    x1 = x_ref[0] + g1 * mixed
    x1_ref[0] = x1
    h2 = _rms(x1) * (1.0 + sc2) + sh2
    hx_ref[0, :, 0:D_MODEL] = h2.astype(BF16)

    logits = jnp.dot(h2, wr_ref[...], preferred_element_type=F32,
                     precision=lax.Precision.HIGHEST)
    lane = lax.broadcasted_iota(I32, (ts, LANES), 1)
    logits = jnp.where(lane < N_EXPERTS, logits, -jnp.inf)
    ex = jnp.exp(logits - jnp.max(logits, axis=-1, keepdims=True))
    aff = ex / jnp.sum(ex, axis=-1, keepdims=True)
    aff_ref[0] = aff
    hi_part = aff.astype(BF16).astype(F32)
    lo_part = pltpu.roll(aff - hi_part, N_EXPERTS, 1)
    hx_ref[0, :, D_MODEL:] = jnp.where(lane < N_EXPERTS, hi_part, lo_part).astype(BF16)


def _mix(x, o, u, mod, wp, ps, woa, wob, wr):
    b, s, _ = x.shape
    ts = TS_PRE
    hpt = ts // POOL_HALO
    nh = s // POOL_HALO
    const = lambda shape: pl.BlockSpec(shape, lambda bi, i: (0,) * len(shape))
    tok = lambda w: pl.BlockSpec((1, ts, w), lambda bi, i: (bi, i, 0))
    return pl.pallas_call(
        functools.partial(_mix_kernel, seq=s, ts=ts),
        out_shape=(jax.ShapeDtypeStruct((b, s, D_MODEL), F32),
                   jax.ShapeDtypeStruct((b, s, X_WIDTH), BF16),
                   jax.ShapeDtypeStruct((b, s, LANES), F32)),
        grid=(b, s // ts),
        in_specs=[tok(D_MODEL), tok(D_ATT), tok(D_POOL),
                  pl.BlockSpec((1, POOL_HALO, D_POOL),
                               lambda bi, i: (bi, jnp.maximum(i * hpt - 1, 0), 0)),
                  pl.BlockSpec((1, POOL_HALO, D_POOL),
                               lambda bi, i: (bi, jnp.minimum((i + 1) * hpt, nh - 1), 0)),
                  pl.BlockSpec((1, N_MOD, D_MODEL), lambda bi, i: (bi, 0, 0)),
                  const((len(POOL_WINDOWS), POOL_GROUP, POOL_GROUP)), const((1, D_POOL)),
                  const((D_ATT, D_MODEL)), const((D_POOL, D_MODEL)), const((D_MODEL, LANES))],
        out_specs=(tok(D_MODEL), tok(X_WIDTH), tok(LANES)),
        scratch_shapes=[pltpu.VMEM((ts + 2 * POOL_HALO, D_POOL), F32)],
        compiler_params=_cparams(("parallel", "arbitrary")),
        name="mix",
    )(x, o, u, u, u, mod, wp, ps, woa, wob, wr)


def _select(aff, thr, need, eq_before, sel_before, tri):
    v = pltpu.bitcast(aff, I32)
    gt = v > thr
    eq = v == thr
    eq_f = jnp.where(eq, 1.0, 0.0)
    rank = jnp.dot(tri, eq_f.astype(BF16), preferred_element_type=F32) + eq_before
    sel = gt | (eq & (rank < need))
    sel_f = jnp.where(sel, 1.0, 0.0)
    pos = jnp.dot(tri, sel_f.astype(BF16), preferred_element_type=F32) + sel_before
    return sel, pos, jnp.sum(sel_f, axis=0, keepdims=True), jnp.sum(eq_f, axis=0, keepdims=True)


def _plan_kernel(aff_ref, tri_ref, thr_ref, starts_ref, eqs_ref, *, n_tok, cap, tt):
    lane = lax.broadcasted_iota(I32, (1, LANES), 1)
    real = lane < N_EXPERTS
    rc = PLAN_ROWS

    def count_ge(cand):
        def body(r, acc):
            off = pl.multiple_of(r * rc, rc)
            v = pltpu.bitcast(aff_ref[pl.ds(off, rc), :], I32)
            return acc + jnp.sum(jnp.where(v >= cand, 1.0, 0.0), axis=0, keepdims=True)
        return lax.fori_loop(0, n_tok // rc, body, jnp.zeros((1, LANES), F32))

    def bit_body(kb, thr):
        cand = thr | lax.shift_left(jnp.int32(1), jnp.int32(30) - kb)
        return jnp.where(count_ge(cand) >= cap, cand, thr)

    thr = lax.fori_loop(0, 31, bit_body, jnp.zeros((1, LANES), I32))
    n_gt = count_ge(thr + 1)
    thr = jnp.where(real, thr, jnp.int32(0x7FFFFFFF))
    need = jnp.where(real, cap - n_gt, 0.0)
    thr_ref[0:1, :] = thr
    thr_ref[1:2, :] = need.astype(I32)
    thr_ref[2:8, :] = jnp.zeros((6, LANES), I32)

    def tile_body(t, carry):
        sel_before, eq_before = carry
        off = pl.multiple_of(t * tt, tt)
        _, _, n_sel, n_eq = _select(aff_ref[pl.ds(off, tt), :], thr, need,
                                    eq_before, sel_before, tri_ref[...])
        starts_ref[t] = jnp.broadcast_to(sel_before.astype(I32), (8, LANES))
        eqs_ref[t] = jnp.broadcast_to(eq_before.astype(I32), (8, LANES))
        return sel_before + n_sel, eq_before + n_eq

    nt = n_tok // tt
    zero = jnp.zeros((1, LANES), F32)
    sel_total, eq_total = lax.fori_loop(0, nt, tile_body, (zero, zero))
    starts_ref[nt] = jnp.broadcast_to(sel_total.astype(I32), (8, LANES))
    eqs_ref[nt] = jnp.broadcast_to(eq_total.astype(I32), (8, LANES))


def _plan(aff2d, tri, cap):
    n_tok = aff2d.shape[0]
    nt = n_tok // TT_MOE
    vm = pl.BlockSpec(memory_space=pltpu.VMEM)
    return pl.pallas_call(
        functools.partial(_plan_kernel, n_tok=n_tok, cap=cap, tt=TT_MOE),
        out_shape=(jax.ShapeDtypeStruct((8, LANES), I32),
                   jax.ShapeDtypeStruct((nt + 1, 8, LANES), I32),
                   jax.ShapeDtypeStruct((nt + 1, 8, LANES), I32)),
        in_specs=[vm, vm],
        out_specs=(vm, vm, vm),
        compiler_params=pltpu.CompilerParams(vmem_limit_bytes=VMEM_LIMIT),
        name="plan",
    )(aff2d, tri)


def _disp_copy(stage, xe_ref, sems, e, row):
    return pltpu.make_async_copy(stage.at[e, pl.ds(0, CH_DISP)],
                                 xe_ref.at[e, pl.ds(row, CH_DISP)], sems.at[e])


def _disp_kernel(starts_s, hx_ref, aff_ref, thr_ref, st_ref, eq_ref, tri_ref, xe_in, xe_ref,
                 stage, carry, sems, pend, *, row0, nt, tt):
    del xe_in
    t = pl.program_id(0)

    @pl.when(t == 0)
    def _():
        stage[...] = jnp.zeros(stage.shape, stage.dtype)
        carry[...] = jnp.zeros(carry.shape, carry.dtype)
        for e in range(N_EXPERTS):
            pend[e] = 0

    sel, pos, _, _ = _select(aff_ref[...], thr_ref[0:1, :], thr_ref[1:2, :].astype(F32),
                             eq_ref[0, 0:1, :].astype(F32), st_ref[0, 0:1, :].astype(F32),
                             tri_ref[...])
    pos_t = jnp.where(sel, pos, -1.0).T
    hx = hx_ref[...]
    slot_iota = lax.broadcasted_iota(I32, (CH_DISP, tt), 0).astype(F32)

    for e in range(N_EXPERTS):
        s0 = starts_s[t, e]
        s1 = starts_s[t + 1, e]
        b0 = s0 & -BF16_ROWS
        nch = jnp.where(s1 > s0, (s1 - b0 + (CH_DISP - 1)) // CH_DISP, 0)
        prow = pos_t[e:e + 1, :]

        def chunk(c, _, e=e, b0=b0, s1=s1, nch=nch, prow=prow):
            cb = pl.multiple_of(b0 + c * CH_DISP, BF16_ROWS)
            onehot = jnp.where(prow - cb.astype(F32) == slot_iota, 1.0, 0.0).astype(BF16)
            rows = jnp.dot(onehot, hx, preferred_element_type=F32)

            @pl.when(pend[e] == 1)
            def _():
                _disp_copy(stage, xe_ref, sems, e, 0).wait()

            stage[e, 0:CH_DISP, :] = rows.astype(BF16)

            @pl.when(c == 0)
            def _():
                stage[e, 0:BF16_ROWS, :] = (rows[0:BF16_ROWS, :]
                                            + carry[e].astype(F32)).astype(BF16)

            _disp_copy(stage, xe_ref, sems, e, row0 + cb).start()
            pend[e] = 1

            @pl.when(c == nch - 1)
            def _():
                off = pl.multiple_of((s1 & -BF16_ROWS) - cb, BF16_ROWS)
                carry[e] = stage[e, pl.ds(off, BF16_ROWS), :]
            return 0

        lax.fori_loop(0, nch, chunk, 0)

    @pl.when(t == nt - 1)
    def _():
        for e in range(N_EXPERTS):
            @pl.when(pend[e] == 1)
            def _(e=e):
                _disp_copy(stage, xe_ref, sems, e, 0).wait()


def _disp(starts_s, hx2d, aff2d, thr, starts_v, eqs_v, tri, xe, row0):
    n_tok = hx2d.shape[0]
    tt = TT_MOE
    nt = n_tok // tt
    grid_spec = pltpu.PrefetchScalarGridSpec(
        num_scalar_prefetch=1,
        grid=(nt,),
        in_specs=[pl.BlockSpec((tt, X_WIDTH), lambda t, s: (t, 0)),
                  pl.BlockSpec((tt, LANES), lambda t, s: (t, 0)),
                  pl.BlockSpec((8, LANES), lambda t, s: (0, 0)),
                  pl.BlockSpec((1, 8, LANES), lambda t, s: (t, 0, 0)),
                  pl.BlockSpec((1, 8, LANES), lambda t, s: (t, 0, 0)),
                  pl.BlockSpec((tt, tt), lambda t, s: (0, 0)),
                  pl.BlockSpec(memory_space=pl.ANY)],
        out_specs=pl.BlockSpec(memory_space=pl.ANY),
        scratch_shapes=[pltpu.VMEM((N_EXPERTS, CH_DISP + BF16_ROWS, X_WIDTH), BF16),
                        pltpu.VMEM((N_EXPERTS, BF16_ROWS, X_WIDTH), BF16),
                        pltpu.SemaphoreType.DMA((N_EXPERTS,)),
                        pltpu.SMEM((N_EXPERTS,), I32)])
    return pl.pallas_call(
        functools.partial(_disp_kernel, row0=row0, nt=nt, tt=tt),
        out_shape=jax.ShapeDtypeStruct(xe.shape, xe.dtype),
        grid_spec=grid_spec,
        input_output_aliases={7: 0},
        compiler_params=_cparams(("arbitrary",)),
        name="disp",
    )(starts_s, hx2d, aff2d, thr, starts_v, eqs_v, tri, xe)


def _ffn_kernel(xe_ref, wg_ref, wu_ref, wd_ref, ye_ref, *, n_live):
    e = pl.program_id(0)
    j = pl.program_id(1)

    @pl.when(j < n_live)
    def _():
        blk = xe_ref[0]
        x = blk[:, 0:D_MODEL]
        gcols = blk[:, D_MODEL:].astype(F32)
        lane = lax.broadcasted_iota(I32, gcols.shape, 1)
        gate = jnp.sum(jnp.where((lane == e) | (lane == e + N_EXPERTS), gcols, 0.0),
                       axis=-1, keepdims=True)
        a = jnp.dot(x, wg_ref[0], preferred_element_type=F32)
        u = jnp.dot(x, wu_ref[0], preferred_element_type=F32)
        hmid = (a * jax.nn.sigmoid(a) * u).astype(BF16)
        y = jnp.dot(hmid, wd_ref[0], preferred_element_type=F32) * gate
        ye_ref[0] = y.astype(BF16)

    @pl.when(j >= n_live)
    def _():
        ye_ref[...] = jnp.zeros(ye_ref.shape, ye_ref.dtype)


def _ffn(xe, wg, wu, wd, n_live):
    _, rows, _ = xe.shape
    tc = TC_FFN
    wspec = pl.BlockSpec((1, D_MODEL, EXPERT_FF), lambda e, j: (e, 0, 0))
    return pl.pallas_call(
        functools.partial(_ffn_kernel, n_live=n_live),
        out_shape=jax.ShapeDtypeStruct((N_EXPERTS, rows, D_MODEL), BF16),
        grid=(N_EXPERTS, rows // tc),
        in_specs=[pl.BlockSpec((1, tc, X_WIDTH), lambda e, j: (e, j, 0)), wspec, wspec,
                  pl.BlockSpec((1, EXPERT_FF, D_MODEL), lambda e, j: (e, 0, 0))],
        out_specs=pl.BlockSpec((1, tc, D_MODEL), lambda e, j: (e, j, 0)),
        compiler_params=_cparams(("parallel", "arbitrary")),
        name="ffn",
    )(xe, wg, wu, wd)


def _comb_copy(ye_ref, buf, sems, slot, e, row):
    return pltpu.make_async_copy(ye_ref.at[e, pl.ds(row, CR_COMB)],
                                 buf.at[slot, pl.ds(e * CR_COMB, CR_COMB)], sems.at[slot])


def _comb_kernel(starts_s, x1_ref, aff_ref, thr_ref, st_ref, eq_ref, tri_ref, mod_ref, fg_ref,
                 ye_ref, y_ref, buf, xbuf, acc_ref, sems, xsem, *, row0, nt, tt):
    t = pl.program_id(0)
    slot = lax.rem(t, 2)

    def issue(tile, sl):
        for e in range(N_EXPERTS):
            b0 = pl.multiple_of(starts_s[tile, e] & -BF16_ROWS, BF16_ROWS)
            _comb_copy(ye_ref, buf, sems, sl, e, row0 + b0).start()

    @pl.when(t == 0)
    def _():
        issue(0, 0)

    @pl.when(t + 1 < nt)
    def _():
        issue(t + 1, 1 - slot)

    sel, pos, _, _ = _select(aff_ref[...], thr_ref[0:1, :], thr_ref[1:2, :].astype(F32),
                             eq_ref[0, 0:1, :].astype(F32), st_ref[0, 0:1, :].astype(F32),
                             tri_ref[...])
    posm = jnp.where(sel, pos, -1.0)
    lane_iota = lax.broadcasted_iota(I32, (tt, CR_COMB), 1).astype(F32)
    ohs = []
    for e in range(N_EXPERTS):
        b0 = starts_s[t, e] & -BF16_ROWS
        rel = posm[:, e:e + 1] - b0.astype(F32)
        ohs.append(jnp.where(rel == lane_iota, 1.0, 0.0).astype(BF16))
    onehot = jnp.concatenate(ohs, axis=-1)

    for e in range(N_EXPERTS):
        _comb_copy(ye_ref, buf, sems, slot, e, 0).wait()
    acc_ref[...] = jnp.dot(onehot, buf[slot], preferred_element_type=F32)

    for e in range(N_EXPERTS):
        s1 = starts_s[t + 1, e]
        b0 = starts_s[t, e] & -BF16_ROWS
        nch = (s1 - b0 + (CR_COMB - 1)) // CR_COMB

        def extra(c, _, e=e, b0=b0):
            cb = pl.multiple_of(b0 + c * CR_COMB, BF16_ROWS)
            cp = pltpu.make_async_copy(ye_ref.at[e, pl.ds(row0 + cb, CR_COMB)], xbuf, xsem.at[0])
            cp.start()
            cp.wait()
            rel = posm[:, e:e + 1] - cb.astype(F32)
            oh = jnp.where(rel == lane_iota, 1.0, 0.0).astype(BF16)
            acc_ref[...] += jnp.dot(oh, xbuf[...], preferred_element_type=F32)
            return 0

        lax.fori_loop(1, nch, extra, 0)

    g2 = mod_ref[0, 5:6, :]
    y_ref[...] = _rms(x1_ref[...] + g2 * acc_ref[...]) * fg_ref[...]


def _comb(starts_s, x1_2d, aff2d, thr, starts_v, eqs_v, tri, mod, fg, ye, row0, seq):
    n_tok = x1_2d.shape[0]
    tt = TT_MOE
    nt = n_tok // tt
    tiles_per_seq = seq // tt
    grid_spec = pltpu.PrefetchScalarGridSpec(
        num_scalar_prefetch=1,
        grid=(nt,),
        in_specs=[pl.BlockSpec((tt, D_MODEL), lambda t, s: (t, 0)),
                  pl.BlockSpec((tt, LANES), lambda t, s: (t, 0)),
                  pl.BlockSpec((8, LANES), lambda t, s: (0, 0)),
                  pl.BlockSpec((1, 8, LANES), lambda t, s: (t, 0, 0)),
                  pl.BlockSpec((1, 8, LANES), lambda t, s: (t, 0, 0)),
                  pl.BlockSpec((tt, tt), lambda t, s: (0, 0)),
                  pl.BlockSpec((1, N_MOD, D_MODEL), lambda t, s: (t // tiles_per_seq, 0, 0)),
                  pl.BlockSpec((1, D_MODEL), lambda t, s: (0, 0)),
                  pl.BlockSpec(memory_space=pl.ANY)],
        out_specs=pl.BlockSpec((tt, D_MODEL), lambda t, s: (t, 0)),
        scratch_shapes=[pltpu.VMEM((2, N_EXPERTS * CR_COMB, D_MODEL), BF16),
                        pltpu.VMEM((CR_COMB, D_MODEL), BF16),
                        pltpu.VMEM((tt, D_MODEL), F32),
                        pltpu.SemaphoreType.DMA((2,)),
                        pltpu.SemaphoreType.DMA((1,))])
    return pl.pallas_call(
        functools.partial(_comb_kernel, row0=row0, nt=nt, tt=tt),
        out_shape=jax.ShapeDtypeStruct((n_tok, D_MODEL), F32),
        grid_spec=grid_spec,
        compiler_params=_cparams(("arbitrary",)),
        name="comb",
    )(starts_s, x1_2d, aff2d, thr, starts_v, eqs_v, tri, mod, fg, ye)


def _head_tile_cols(base, nope):
    r = base + nope
    return (list(range(base, base + nope)) + list(range(r, r + QK_ROPE))
            + list(range(r, r + HALF_ROPE)))


def _prep_weights(w_in, w_uq, w_ukv):
    zcol = lambda w, n: jnp.zeros((w.shape[0], n), w.dtype)
    kpe0 = Q_LORA + KV_LORA
    kpe_tile = jnp.concatenate(
        [zcol(w_in, QK_NOPE), w_in[:, kpe0:kpe0 + QK_ROPE], w_in[:, kpe0:kpe0 + HALF_ROPE],
         zcol(w_in, HEAD_PAD - QK_NOPE - QK_ROPE - HALF_ROPE)], axis=1)
    win = jnp.concatenate([w_in[:, :kpe0], kpe_tile, w_in[:, kpe0 + QK_ROPE:]], axis=1)

    q_tiles, k_tiles, v_cols = [], [], []
    for hd in range(N_HEADS):
        cols = np.asarray(_head_tile_cols(hd * (QK_NOPE + QK_ROPE), QK_NOPE))
        q_tiles += [w_uq[:, cols], zcol(w_uq, HEAD_PAD - len(cols))]
        kb = hd * (QK_NOPE + V_DIM)
        k_tiles += [w_ukv[:, kb:kb + QK_NOPE], zcol(w_ukv, HEAD_PAD - QK_NOPE)]
        v_cols.append(w_ukv[:, kb + QK_NOPE:kb + QK_NOPE + V_DIM])
    wq = jnp.concatenate(q_tiles, axis=1)
    wkv = jnp.concatenate(k_tiles + v_cols, axis=1)
    return win.astype(BF16), wq.astype(BF16), wkv.astype(BF16)


def _rope_tabs(seq):
    inv = ROPE_THETA ** (-jnp.arange(0, QK_ROPE, 2, dtype=F32) / QK_ROPE)
    ang = jnp.arange(seq, dtype=F32)[:, None] * inv[None, :]
    cos, sin = jnp.cos(ang), jnp.sin(ang)
    one = jnp.ones((seq, QK_NOPE), F32)
    z_nope = jnp.zeros((seq, QK_NOPE), F32)
    z_pad = jnp.zeros((seq, HEAD_PAD - QK_NOPE - QK_ROPE), F32)
    scale = (QK_NOPE + QK_ROPE) ** -0.5
    c_full = jnp.concatenate([one, cos, cos, z_pad], axis=1)
    s_full = jnp.concatenate([z_nope, -sin, sin, z_pad], axis=1)
    c_rope = jnp.concatenate([z_nope, cos, cos, z_pad], axis=1)
    return c_full * scale, s_full * scale, c_rope, s_full


def _tri(tt):
    r = np.arange(tt)
    return jnp.asarray((r[None, :] < r[:, None]).astype(np.float32), dtype=BF16)


def kernel(x_prompt, x_sample, c_prompt, c_sample, w_ada, b_ada, w_in, q_norm_g, kv_norm_g,
           w_uq, w_ukv, w_pool, pool_scale, w_out, w_router, w_gate, w_up, w_down, final_norm_g):
    groups = ((x_prompt, c_prompt), (x_sample, c_sample))
    n_c = sum(c.shape[0] for _, c in groups)
    c_all = jnp.concatenate([c for _, c in groups]
                            + [jnp.zeros((-n_c % 8, D_MODEL), F32)], axis=0)
    mod_all = _ada(c_all, w_ada[0], b_ada)

    win, wq, wkv = _prep_weights(w_in[0], w_uq[0], w_ukv[0])
    qg = q_norm_g[0][None, :]
    kvg = kv_norm_g[0][None, :]
    wp = w_pool[0].astype(BF16)
    ps = pool_scale[0][None, :]
    woa = w_out[0][:D_ATT].astype(BF16)
    wob = w_out[0][D_ATT:].astype(BF16)
    wr = jnp.concatenate([w_router[0], jnp.zeros((D_MODEL, LANES - N_EXPERTS), F32)], axis=1)
    wg = w_gate[0].astype(BF16)
    wu = w_up[0].astype(BF16)
    wd = w_down[0].astype(BF16)
    fg = final_norm_g[None, :]
    tri = _tri(TT_MOE)

    caps = [CAPACITY_FACTOR * x.shape[0] * x.shape[1] // N_EXPERTS for x, _ in groups]
    live_rows = sum(caps)
    assert live_rows % TC_FFN == 0 and all(c % BF16_ROWS == 0 for c in caps)
    xe = jnp.zeros((N_EXPERTS, live_rows + TC_FFN, X_WIDTH), BF16)

    staged = []
    c_off = 0
    row0 = 0
    for (x, c), cap in zip(groups, caps):
        b, s, _ = x.shape
        mod = mod_all[c_off:c_off + b].reshape(b, N_MOD, D_MODEL)
        c_off += b
        q, k, v, u = _pre(x, mod, win, qg, kvg, wq, wkv, _rope_tabs(s))
        o = _attn(q, k, v)
        x1, hx, aff = _mix(x, o, u, mod, wp, ps, woa, wob, wr)
        n_tok = b * s
        aff2d = aff.reshape(n_tok, LANES)
        thr, starts_v, eqs_v = _plan(aff2d, tri, cap)
        starts_s = starts_v[:, 0, :N_EXPERTS]
        xe = _disp(starts_s, hx.reshape(n_tok, X_WIDTH), aff2d, thr, starts_v, eqs_v, tri, xe, row0)
        staged.append((x1.reshape(n_tok, D_MODEL), aff2d, thr, starts_s, starts_v, eqs_v, mod,
                       row0, (b, s)))
        row0 += cap

    ye = _ffn(xe, wg, wu, wd, live_rows // TC_FFN)

    outs = []
    for x1, aff2d, thr, starts_s, starts_v, eqs_v, mod, r0, (b, s) in staged:
        y = _comb(starts_s, x1, aff2d, thr, starts_v, eqs_v, tri, mod, fg, ye, r0, s)
        outs.append(y.reshape(b, s, D_MODEL))
    return tuple(outs)
```

```python
import functools

import jax
import jax.numpy as jnp
import numpy as np
from jax import lax
from jax.experimental import pallas as pl
from jax.experimental.pallas import tpu as pltpu

F32 = jnp.float32
BF16 = jnp.bfloat16
I32 = jnp.int32

D_MODEL = 1024
N_HEADS = 8
QK_NOPE = 64
QK_ROPE = 32
V_DIM = 64
Q_LORA = 384
KV_LORA = 256
ROPE_THETA = 10000.0
D_ATT = N_HEADS * V_DIM
D_POOL = 512
POOL_WINDOWS = (2, 4, 8, 16)
POOL_GROUP = 128
N_EXPERTS = 16
CAPACITY_FACTOR = 2
EXPERT_FF = 1024
N_MOD = 6
EPS = 1e-6

LANES = 128
HEAD_PAD = LANES
HALF_ROPE = QK_ROPE // 2
D_IN_PAD = Q_LORA + KV_LORA + LANES + D_POOL
X_WIDTH = D_MODEL + LANES
POOL_HALO = 8
BF16_ROWS = 16

TS_PRE = 512
TQ_ATT = 512
ROWS_ATT = 512
TK_ATT = 1024
UNROLL_ATT = 4
TT_MOE = 256
CH_DISP = 64
CR_COMB = 128
TC_FFN = 512
PLAN_ROWS = 512
VMEM_LIMIT = 56 * 1024 * 1024


def _cparams(sem):
    return pltpu.CompilerParams(dimension_semantics=sem, vmem_limit_bytes=VMEM_LIMIT)


def _rms(x):
    return x * lax.rsqrt(jnp.mean(x * x, axis=-1, keepdims=True) + EPS)


def _ada_kernel(c_ref, w_ref, b_ref, o_ref):
    c = c_ref[...]
    s = c * jax.nn.sigmoid(c)
    o_ref[...] = jnp.dot(s, w_ref[...], preferred_element_type=F32,
                         precision=lax.Precision.HIGHEST) + b_ref[...]


def _ada(c_pad, w_ada, b_ada):
    rows = c_pad.shape[0]
    return pl.pallas_call(
        _ada_kernel,
        out_shape=jax.ShapeDtypeStruct((rows, N_MOD * D_MODEL), F32),
        grid=(N_MOD,),
        in_specs=[pl.BlockSpec((rows, D_MODEL), lambda j: (0, 0)),
                  pl.BlockSpec((D_MODEL, D_MODEL), lambda j: (0, j)),
                  pl.BlockSpec((1, D_MODEL), lambda j: (0, j))],
        out_specs=pl.BlockSpec((rows, D_MODEL), lambda j: (0, j)),
        compiler_params=_cparams(("arbitrary",)),
        name="ada",
    )(c_pad, w_ada, b_ada)


def _pre_kernel(x_ref, mod_ref, win_ref, qg_ref, kvg_ref, wq_ref, wkv_ref,
                cq_ref, sq_ref, ck_ref, sk_ref, vone_ref, q_ref, k_ref, v_ref, u_ref):
    x = x_ref[0]
    sh1 = mod_ref[0, 0:1, :]
    sc1 = mod_ref[0, 1:2, :]
    h = (_rms(x) * (1.0 + sc1) + sh1).astype(BF16)
    z = jnp.dot(h, win_ref[...], preferred_element_type=F32)
    c_q = z[:, :Q_LORA]
    c_kv = z[:, Q_LORA:Q_LORA + KV_LORA]
    kpe = z[:, Q_LORA + KV_LORA:Q_LORA + KV_LORA + LANES]
    u_ref[0] = z[:, Q_LORA + KV_LORA + LANES:]

    q = jnp.dot((_rms(c_q) * qg_ref[...]).astype(BF16), wq_ref[...],
                preferred_element_type=F32)
    cq_t = cq_ref[...]
    sq_t = sq_ref[...]
    for hd in range(N_HEADS):
        blk = q[:, hd * HEAD_PAD:(hd + 1) * HEAD_PAD]
        rot = pltpu.roll(blk, HEAD_PAD - HALF_ROPE, 1)
        q_ref[0, :, hd * HEAD_PAD:(hd + 1) * HEAD_PAD] = (blk * cq_t + rot * sq_t).astype(BF16)

    kv = jnp.dot((_rms(c_kv) * kvg_ref[...]).astype(BF16), wkv_ref[...],
                 preferred_element_type=F32)
    kpe_r = kpe * ck_ref[...] + pltpu.roll(kpe, HEAD_PAD - HALF_ROPE, 1) * sk_ref[...]
    for hd in range(N_HEADS):
        k_ref[0, :, hd * HEAD_PAD:(hd + 1) * HEAD_PAD] = (
            kv[:, hd * HEAD_PAD:(hd + 1) * HEAD_PAD] + kpe_r).astype(BF16)
    v_ref[0] = (kv[:, N_HEADS * HEAD_PAD:] + vone_ref[...]).astype(BF16)


def _pre(x, mod, win, qg, kvg, wq, wkv, vone, tabs):
    b, s, _ = x.shape
    ts = TS_PRE
    const = lambda shape: pl.BlockSpec(shape, lambda bi, i: (0,) * len(shape))
    tab = pl.BlockSpec((ts, LANES), lambda bi, i: (i, 0))
    tok = lambda w: pl.BlockSpec((1, ts, w), lambda bi, i: (bi, i, 0))
    return pl.pallas_call(
        _pre_kernel,
        out_shape=(jax.ShapeDtypeStruct((b, s, N_HEADS * HEAD_PAD), BF16),
                   jax.ShapeDtypeStruct((b, s, N_HEADS * HEAD_PAD), BF16),
                   jax.ShapeDtypeStruct((b, s, N_HEADS * HEAD_PAD), BF16),
                   jax.ShapeDtypeStruct((b, s, D_POOL), F32)),
        grid=(b, s // ts),
        in_specs=[tok(D_MODEL),
                  pl.BlockSpec((1, N_MOD, D_MODEL), lambda bi, i: (bi, 0, 0)),
                  const((D_MODEL, D_IN_PAD)), const((1, Q_LORA)), const((1, KV_LORA)),
                  const((Q_LORA, N_HEADS * HEAD_PAD)),
                  const((KV_LORA, 2 * N_HEADS * HEAD_PAD)),
                  tab, tab, tab, tab, const((1, N_HEADS * HEAD_PAD))],
        out_specs=(tok(N_HEADS * HEAD_PAD), tok(N_HEADS * HEAD_PAD), tok(N_HEADS * HEAD_PAD),
                   tok(D_POOL)),
        compiler_params=_cparams(("parallel", "arbitrary")),
        name="pre",
    )(x, mod, win, qg, kvg, wq, wkv, *tabs, vone)


def _attn_kernel(q_ref, k_ref, v_ref, o_ref, *, seq, tq, tk, rows, unroll):
    chains = [(r, hh) for r in range(tq // rows) for hh in range(2)]

    def body(c, carry):
        off = pl.multiple_of(c * tk, tk)
        new = []
        for (r, hh), (m, acc) in zip(chains, carry):
            cols = slice(hh * HEAD_PAD, (hh + 1) * HEAD_PAD)
            qh = q_ref[0, r * rows:(r + 1) * rows, cols]
            s = lax.dot_general(qh, k_ref[0, pl.ds(off, tk), cols], (((1,), (1,)), ((), ())),
                                preferred_element_type=F32)
            m_new = jnp.maximum(m, jnp.max(s, axis=-1, keepdims=True))
            p = jnp.exp2((s - m_new).astype(BF16))
            acc = jnp.exp2(m - m_new) * acc + jnp.dot(p, v_ref[0, pl.ds(off, tk), cols],
                                                      preferred_element_type=F32)
            new.append((m_new, acc))
        return tuple(new)

    init = tuple((jnp.full((rows, 1), -jnp.inf, F32), jnp.zeros((rows, HEAD_PAD), F32))
                 for _ in chains)
    final = lax.fori_loop(0, seq // tk, body, init, unroll=unroll)
    lane = lax.broadcasted_iota(I32, (rows, HEAD_PAD), 1)
    for r in range(tq // rows):
        a0, a1 = final[2 * r][1], final[2 * r + 1][1]
        o_ref[0, r * rows:(r + 1) * rows, :] = jnp.where(
            lane < V_DIM, a0 / a0[:, V_DIM:V_DIM + 1], a1 / a1[:, 0:1]).astype(BF16)


def _attn(q, k, v):
    b, s, _ = q.shape
    tq, tk = TQ_ATT, TK_ATT
    assert s % tq == 0 and s % tk == 0
    return pl.pallas_call(
        functools.partial(_attn_kernel, seq=s, tq=tq, tk=tk, rows=ROWS_ATT, unroll=UNROLL_ATT),
        out_shape=jax.ShapeDtypeStruct((b, s, D_ATT), BF16),
        grid=(b, N_HEADS // 2, s // tq),
        in_specs=[pl.BlockSpec((1, tq, 2 * HEAD_PAD), lambda bi, p, i: (bi, i, p)),
                  pl.BlockSpec((1, s, 2 * HEAD_PAD), lambda bi, p, i: (bi, 0, p)),
                  pl.BlockSpec((1, s, 2 * HEAD_PAD), lambda bi, p, i: (bi, 0, p))],
        out_specs=pl.BlockSpec((1, tq, 2 * V_DIM), lambda bi, p, i: (bi, i, p)),
        compiler_params=_cparams(("parallel", "parallel", "arbitrary")),
        name="attn",
    )(q, k, v)


def _mix_kernel(x_ref, o_ref, u_ref, up_ref, un_ref, mod_ref, wp_ref, ps_ref, woa_ref, wob_ref,
                wr_ref, x1_ref, hx_ref, aff_ref, ue_ref, *, seq, ts):
    i = pl.program_id(1)
    nt = pl.num_programs(1)
    hb = POOL_HALO
    ue_ref[0:hb, :] = jnp.where(i > 0, up_ref[0], 0.0)
    ue_ref[hb:hb + ts, :] = u_ref[0]
    ue_ref[hb + ts:2 * hb + ts, :] = jnp.where(i < nt - 1, un_ref[0], 0.0)

    t = i * ts + lax.broadcasted_iota(I32, (ts, 1), 0)
    ys = []
    for g, w in enumerate(POOL_WINDOWS):
        cols = slice(g * POOL_GROUP, (g + 1) * POOL_GROUP)
        acc = None
        for off in range(-(w // 2), w - w // 2):
            sl = ue_ref[hb + off:hb + off + ts, cols]
            acc = sl if acc is None else acc + sl
        lo = jnp.maximum(t - w // 2, 0)
        hi = jnp.minimum(t + (w - w // 2), seq)
        cnt = (hi - lo).astype(F32)
        d = (acc / cnt - ue_ref[hb:hb + ts, cols]).astype(BF16)
        y = jnp.dot(d, wp_ref[g], preferred_element_type=F32) * ps_ref[:, cols]
        ys.append(y.astype(BF16))
    pool = jnp.concatenate(ys, axis=-1)
    mixed = (jnp.dot(o_ref[0], woa_ref[...], preferred_element_type=F32)
             + jnp.dot(pool, wob_ref[...], preferred_element_type=F32))
    g1 = mod_ref[0, 2:3, :]
    sh2 = mod_ref[0, 3:4, :]
    sc2 = mod_ref[0, 4:5, :]
    x1 = x_ref[0] + g1 * mixed
    x1_ref[0] = x1
    h2 = _rms(x1) * (1.0 + sc2) + sh2
    hx_ref[0, :, 0:D_MODEL] = h2.astype(BF16)

    logits = jnp.dot(h2, wr_ref[...], preferred_element_type=F32,
                     precision=lax.Precision.HIGHEST)
    lane = lax.broadcasted_iota(I32, (ts, LANES), 1)
    logits = jnp.where(lane < N_EXPERTS, logits, -jnp.inf)
    ex = jnp.exp(logits - jnp.max(logits, axis=-1, keepdims=True))
    aff = ex / jnp.sum(ex, axis=-1, keepdims=True)
    aff_ref[0] = aff
    hi_part = aff.astype(BF16).astype(F32)
    lo_part = pltpu.roll(aff - hi_part, N_EXPERTS, 1)
    hx_ref[0, :, D_MODEL:] = jnp.where(lane < N_EXPERTS, hi_part, lo_part).astype(BF16)


def _mix(x, o, u, mod, wp, ps, woa, wob, wr):
    b, s, _ = x.shape
    ts = TS_PRE
    hpt = ts // POOL_HALO
    nh = s // POOL_HALO
    const = lambda shape: pl.BlockSpec(shape, lambda bi, i: (0,) * len(shape))
    tok = lambda w: pl.BlockSpec((1, ts, w), lambda bi, i: (bi, i, 0))
    return pl.pallas_call(
        functools.partial(_mix_kernel, seq=s, ts=ts),
        out_shape=(jax.ShapeDtypeStruct((b, s, D_MODEL), F32),
                   jax.ShapeDtypeStruct((b, s, X_WIDTH), BF16),
                   jax.ShapeDtypeStruct((b, s, LANES), F32)),
        grid=(b, s // ts),
        in_specs=[tok(D_MODEL), tok(D_ATT), tok(D_POOL),
                  pl.BlockSpec((1, POOL_HALO, D_POOL),
                               lambda bi, i: (bi, jnp.maximum(i * hpt - 1, 0), 0)),
                  pl.BlockSpec((1, POOL_HALO, D_POOL),
                               lambda bi, i: (bi, jnp.minimum((i + 1) * hpt, nh - 1), 0)),
                  pl.BlockSpec((1, N_MOD, D_MODEL), lambda bi, i: (bi, 0, 0)),
                  const((len(POOL_WINDOWS), POOL_GROUP, POOL_GROUP)), const((1, D_POOL)),
                  const((D_ATT, D_MODEL)), const((D_POOL, D_MODEL)), const((D_MODEL, LANES))],
        out_specs=(tok(D_MODEL), tok(X_WIDTH), tok(LANES)),
        scratch_shapes=[pltpu.VMEM((ts + 2 * POOL_HALO, D_POOL), F32)],
        compiler_params=_cparams(("parallel", "arbitrary")),
        name="mix",
    )(x, o, u, u, u, mod, wp, ps, woa, wob, wr)


def _select(aff, thr, need, eq_before, sel_before, tri):
    v = pltpu.bitcast(aff, I32)
    gt = v > thr
    eq = v == thr
    eq_f = jnp.where(eq, 1.0, 0.0)
    rank = jnp.dot(tri, eq_f.astype(BF16), preferred_element_type=F32) + eq_before
    sel = gt | (eq & (rank < need))
    sel_f = jnp.where(sel, 1.0, 0.0)
    pos = jnp.dot(tri, sel_f.astype(BF16), preferred_element_type=F32) + sel_before
    return sel, pos, jnp.sum(sel_f, axis=0, keepdims=True), jnp.sum(eq_f, axis=0, keepdims=True)


def _plan_kernel(aff_ref, tri_ref, thr_ref, starts_ref, eqs_ref, *, n_tok, cap, tt):
    lane = lax.broadcasted_iota(I32, (1, LANES), 1)
    real = lane < N_EXPERTS
    rc = PLAN_ROWS

    def count_ge(cand):
        def body(r, acc):
            off = pl.multiple_of(r * rc, rc)
            v = pltpu.bitcast(aff_ref[pl.ds(off, rc), :], I32)
            return acc + jnp.sum(jnp.where(v >= cand, 1.0, 0.0), axis=0, keepdims=True)
        return lax.fori_loop(0, n_tok // rc, body, jnp.zeros((1, LANES), F32))

    def bit_body(kb, thr):
        cand = thr | lax.shift_left(jnp.int32(1), jnp.int32(30) - kb)
        return jnp.where(count_ge(cand) >= cap, cand, thr)

    thr = lax.fori_loop(0, 31, bit_body, jnp.zeros((1, LANES), I32))
    n_gt = count_ge(thr + 1)
    thr = jnp.where(real, thr, jnp.int32(0x7FFFFFFF))
    need = jnp.where(real, cap - n_gt, 0.0)
    thr_ref[0:1, :] = thr
    thr_ref[1:2, :] = need.astype(I32)
    thr_ref[2:8, :] = jnp.zeros((6, LANES), I32)

    def tile_body(t, carry):
        sel_before, eq_before = carry
        off = pl.multiple_of(t * tt, tt)
        _, _, n_sel, n_eq = _select(aff_ref[pl.ds(off, tt), :], thr, need,
                                    eq_before, sel_before, tri_ref[...])
        starts_ref[t] = jnp.broadcast_to(sel_before.astype(I32), (8, LANES))
        eqs_ref[t] = jnp.broadcast_to(eq_before.astype(I32), (8, LANES))
        return sel_before + n_sel, eq_before + n_eq

    nt = n_tok // tt
    zero = jnp.zeros((1, LANES), F32)
    sel_total, eq_total = lax.fori_loop(0, nt, tile_body, (zero, zero))
    starts_ref[nt] = jnp.broadcast_to(sel_total.astype(I32), (8, LANES))
    eqs_ref[nt] = jnp.broadcast_to(eq_total.astype(I32), (8, LANES))


def _plan(aff2d, tri, cap):
    n_tok = aff2d.shape[0]
    nt = n_tok // TT_MOE
    vm = pl.BlockSpec(memory_space=pltpu.VMEM)
    return pl.pallas_call(
        functools.partial(_plan_kernel, n_tok=n_tok, cap=cap, tt=TT_MOE),
        out_shape=(jax.ShapeDtypeStruct((8, LANES), I32),
                   jax.ShapeDtypeStruct((nt + 1, 8, LANES), I32),
                   jax.ShapeDtypeStruct((nt + 1, 8, LANES), I32)),
        in_specs=[vm, vm],
        out_specs=(vm, vm, vm),
        compiler_params=pltpu.CompilerParams(vmem_limit_bytes=VMEM_LIMIT),
        name="plan",
    )(aff2d, tri)


def _disp_copy(stage, xe_ref, sems, e, row):
    return pltpu.make_async_copy(stage.at[e, pl.ds(0, CH_DISP)],
                                 xe_ref.at[e, pl.ds(row, CH_DISP)], sems.at[e])


def _disp_kernel(starts_s, hx_ref, aff_ref, thr_ref, st_ref, eq_ref, tri_ref, xe_in, xe_ref,
                 stage, carry, sems, pend, *, row0, nt, tt):
    del xe_in
    t = pl.program_id(0)

    @pl.when(t == 0)
    def _():
        stage[...] = jnp.zeros(stage.shape, stage.dtype)
        carry[...] = jnp.zeros(carry.shape, carry.dtype)
        for e in range(N_EXPERTS):
            pend[e] = 0

    sel, pos, _, _ = _select(aff_ref[...], thr_ref[0:1, :], thr_ref[1:2, :].astype(F32),
                             eq_ref[0, 0:1, :].astype(F32), st_ref[0, 0:1, :].astype(F32),
                             tri_ref[...])
    pos_t = jnp.where(sel, pos, -1.0).T
    hx = hx_ref[...]
    slot_iota = lax.broadcasted_iota(I32, (CH_DISP, tt), 0).astype(F32)

    for e in range(N_EXPERTS):
        s0 = starts_s[t, e]
        s1 = starts_s[t + 1, e]
        b0 = s0 & -BF16_ROWS
        nch = jnp.where(s1 > s0, (s1 - b0 + (CH_DISP - 1)) // CH_DISP, 0)
        prow = pos_t[e:e + 1, :]

        def chunk(c, _, e=e, b0=b0, s1=s1, nch=nch, prow=prow):
            cb = pl.multiple_of(b0 + c * CH_DISP, BF16_ROWS)
            onehot = jnp.where(prow - cb.astype(F32) == slot_iota, 1.0, 0.0).astype(BF16)
            rows = jnp.dot(onehot, hx, preferred_element_type=F32)

            @pl.when(pend[e] == 1)
            def _():
                _disp_copy(stage, xe_ref, sems, e, 0).wait()

            stage[e, 0:CH_DISP, :] = rows.astype(BF16)

            @pl.when(c == 0)
            def _():
                stage[e, 0:BF16_ROWS, :] = (rows[0:BF16_ROWS, :]
                                            + carry[e].astype(F32)).astype(BF16)

            _disp_copy(stage, xe_ref, sems, e, row0 + cb).start()
            pend[e] = 1

            @pl.when(c == nch - 1)
            def _():
                off = pl.multiple_of((s1 & -BF16_ROWS) - cb, BF16_ROWS)
                carry[e] = stage[e, pl.ds(off, BF16_ROWS), :]
            return 0

        lax.fori_loop(0, nch, chunk, 0)

    @pl.when(t == nt - 1)
    def _():
        for e in range(N_EXPERTS):
            @pl.when(pend[e] == 1)
            def _(e=e):
                _disp_copy(stage, xe_ref, sems, e, 0).wait()


def _disp(starts_s, hx2d, aff2d, thr, starts_v, eqs_v, tri, xe, row0):
    n_tok = hx2d.shape[0]
    tt = TT_MOE
    nt = n_tok // tt
    grid_spec = pltpu.PrefetchScalarGridSpec(
        num_scalar_prefetch=1,
        grid=(nt,),
        in_specs=[pl.BlockSpec((tt, X_WIDTH), lambda t, s: (t, 0)),
                  pl.BlockSpec((tt, LANES), lambda t, s: (t, 0)),
                  pl.BlockSpec((8, LANES), lambda t, s: (0, 0)),
                  pl.BlockSpec((1, 8, LANES), lambda t, s: (t, 0, 0)),
                  pl.BlockSpec((1, 8, LANES), lambda t, s: (t, 0, 0)),
                  pl.BlockSpec((tt, tt), lambda t, s: (0, 0)),
                  pl.BlockSpec(memory_space=pl.ANY)],
        out_specs=pl.BlockSpec(memory_space=pl.ANY),
        scratch_shapes=[pltpu.VMEM((N_EXPERTS, CH_DISP + BF16_ROWS, X_WIDTH), BF16),
                        pltpu.VMEM((N_EXPERTS, BF16_ROWS, X_WIDTH), BF16),
                        pltpu.SemaphoreType.DMA((N_EXPERTS,)),
                        pltpu.SMEM((N_EXPERTS,), I32)])
    return pl.pallas_call(
        functools.partial(_disp_kernel, row0=row0, nt=nt, tt=tt),
        out_shape=jax.ShapeDtypeStruct(xe.shape, xe.dtype),
        grid_spec=grid_spec,
        input_output_aliases={7: 0},
        compiler_params=_cparams(("arbitrary",)),
        name="disp",
    )(starts_s, hx2d, aff2d, thr, starts_v, eqs_v, tri, xe)


def _ffn_kernel(xe_ref, wg_ref, wu_ref, wd_ref, ye_ref, *, n_live):
    e = pl.program_id(0)
    j = pl.program_id(1)

    @pl.when(j < n_live)
    def _():
        blk = xe_ref[0]
        x = blk[:, 0:D_MODEL]
        gcols = blk[:, D_MODEL:].astype(F32)
        lane = lax.broadcasted_iota(I32, gcols.shape, 1)
        gate = jnp.sum(jnp.where((lane == e) | (lane == e + N_EXPERTS), gcols, 0.0),
                       axis=-1, keepdims=True)
        a = jnp.dot(x, wg_ref[0], preferred_element_type=F32)
        u = jnp.dot(x, wu_ref[0], preferred_element_type=F32)
        hmid = (a * jax.nn.sigmoid(a) * u).astype(BF16)
        y = jnp.dot(hmid, wd_ref[0], preferred_element_type=F32) * gate
        ye_ref[0] = y.astype(BF16)

    @pl.when(j >= n_live)
    def _():
        ye_ref[...] = jnp.zeros(ye_ref.shape, ye_ref.dtype)


def _ffn(xe, wg, wu, wd, n_live):
    _, rows, _ = xe.shape
    tc = TC_FFN
    wspec = pl.BlockSpec((1, D_MODEL, EXPERT_FF), lambda e, j: (e, 0, 0))
    return pl.pallas_call(
        functools.partial(_ffn_kernel, n_live=n_live),
        out_shape=jax.ShapeDtypeStruct((N_EXPERTS, rows, D_MODEL), BF16),
        grid=(N_EXPERTS, rows // tc),
        in_specs=[pl.BlockSpec((1, tc, X_WIDTH), lambda e, j: (e, j, 0)), wspec, wspec,
                  pl.BlockSpec((1, EXPERT_FF, D_MODEL), lambda e, j: (e, 0, 0))],
        out_specs=pl.BlockSpec((1, tc, D_MODEL), lambda e, j: (e, j, 0)),
        compiler_params=_cparams(("parallel", "arbitrary")),
        name="ffn",
    )(xe, wg, wu, wd)


def _comb_copy(ye_ref, buf, sems, slot, e, row):
    return pltpu.make_async_copy(ye_ref.at[e, pl.ds(row, CR_COMB)],
                                 buf.at[slot, pl.ds(e * CR_COMB, CR_COMB)], sems.at[slot])


def _comb_kernel(starts_s, x1_ref, aff_ref, thr_ref, st_ref, eq_ref, tri_ref, mod_ref, fg_ref,
                 ye_ref, y_ref, buf, xbuf, acc_ref, sems, xsem, *, row0, nt, tt):
    t = pl.program_id(0)
    slot = lax.rem(t, 2)

    def issue(tile, sl):
        for e in range(N_EXPERTS):
            b0 = pl.multiple_of(starts_s[tile, e] & -BF16_ROWS, BF16_ROWS)
            _comb_copy(ye_ref, buf, sems, sl, e, row0 + b0).start()

    @pl.when(t == 0)
    def _():
        issue(0, 0)

    @pl.when(t + 1 < nt)
    def _():
        issue(t + 1, 1 - slot)

    sel, pos, _, _ = _select(aff_ref[...], thr_ref[0:1, :], thr_ref[1:2, :].astype(F32),
                             eq_ref[0, 0:1, :].astype(F32), st_ref[0, 0:1, :].astype(F32),
                             tri_ref[...])
    posm = jnp.where(sel, pos, -1.0)
    lane_iota = lax.broadcasted_iota(I32, (tt, CR_COMB), 1).astype(F32)
    ohs = []
    for e in range(N_EXPERTS):
        b0 = starts_s[t, e] & -BF16_ROWS
        rel = posm[:, e:e + 1] - b0.astype(F32)
        ohs.append(jnp.where(rel == lane_iota, 1.0, 0.0).astype(BF16))
    onehot = jnp.concatenate(ohs, axis=-1)

    for e in range(N_EXPERTS):
        _comb_copy(ye_ref, buf, sems, slot, e, 0).wait()
    acc_ref[...] = jnp.dot(onehot, buf[slot], preferred_element_type=F32)

    for e in range(N_EXPERTS):
        s1 = starts_s[t + 1, e]
        b0 = starts_s[t, e] & -BF16_ROWS
        nch = (s1 - b0 + (CR_COMB - 1)) // CR_COMB

        def extra(c, _, e=e, b0=b0):
            cb = pl.multiple_of(b0 + c * CR_COMB, BF16_ROWS)
            cp = pltpu.make_async_copy(ye_ref.at[e, pl.ds(row0 + cb, CR_COMB)], xbuf, xsem.at[0])
            cp.start()
            cp.wait()
            rel = posm[:, e:e + 1] - cb.astype(F32)
            oh = jnp.where(rel == lane_iota, 1.0, 0.0).astype(BF16)
            acc_ref[...] += jnp.dot(oh, xbuf[...], preferred_element_type=F32)
            return 0

        lax.fori_loop(1, nch, extra, 0)

    g2 = mod_ref[0, 5:6, :]
    y_ref[...] = _rms(x1_ref[...] + g2 * acc_ref[...]) * fg_ref[...]


def _comb(starts_s, x1_2d, aff2d, thr, starts_v, eqs_v, tri, mod, fg, ye, row0, seq):
    n_tok = x1_2d.shape[0]
    tt = TT_MOE
    nt = n_tok // tt
    tiles_per_seq = seq // tt
    grid_spec = pltpu.PrefetchScalarGridSpec(
        num_scalar_prefetch=1,
        grid=(nt,),
        in_specs=[pl.BlockSpec((tt, D_MODEL), lambda t, s: (t, 0)),
                  pl.BlockSpec((tt, LANES), lambda t, s: (t, 0)),
                  pl.BlockSpec((8, LANES), lambda t, s: (0, 0)),
                  pl.BlockSpec((1, 8, LANES), lambda t, s: (t, 0, 0)),
                  pl.BlockSpec((1, 8, LANES), lambda t, s: (t, 0, 0)),
                  pl.BlockSpec((tt, tt), lambda t, s: (0, 0)),
                  pl.BlockSpec((1, N_MOD, D_MODEL), lambda t, s: (t // tiles_per_seq, 0, 0)),
                  pl.BlockSpec((1, D_MODEL), lambda t, s: (0, 0)),
                  pl.BlockSpec(memory_space=pl.ANY)],
        out_specs=pl.BlockSpec((tt, D_MODEL), lambda t, s: (t, 0)),
        scratch_shapes=[pltpu.VMEM((2, N_EXPERTS * CR_COMB, D_MODEL), BF16),
                        pltpu.VMEM((CR_COMB, D_MODEL), BF16),
                        pltpu.VMEM((tt, D_MODEL), F32),
                        pltpu.SemaphoreType.DMA((2,)),
                        pltpu.SemaphoreType.DMA((1,))])
    return pl.pallas_call(
        functools.partial(_comb_kernel, row0=row0, nt=nt, tt=tt),
        out_shape=jax.ShapeDtypeStruct((n_tok, D_MODEL), F32),
        grid_spec=grid_spec,
        compiler_params=_cparams(("arbitrary",)),
        name="comb",
    )(starts_s, x1_2d, aff2d, thr, starts_v, eqs_v, tri, mod, fg, ye)


def _head_tile_cols(base, nope):
    r = base + nope
    return (list(range(base, base + nope)) + list(range(r, r + QK_ROPE))
            + list(range(r, r + HALF_ROPE)))


def _prep_weights(w_in, w_uq, w_ukv):
    zcol = lambda w, n: jnp.zeros((w.shape[0], n), w.dtype)
    kpe0 = Q_LORA + KV_LORA
    kpe_tile = jnp.concatenate(
        [zcol(w_in, QK_NOPE), w_in[:, kpe0:kpe0 + QK_ROPE], w_in[:, kpe0:kpe0 + HALF_ROPE],
         zcol(w_in, HEAD_PAD - QK_NOPE - QK_ROPE - HALF_ROPE)], axis=1)
    win = jnp.concatenate([w_in[:, :kpe0], kpe_tile, w_in[:, kpe0 + QK_ROPE:]], axis=1)

    q_tiles, k_tiles, v_cols = [], [], []
    for hd in range(N_HEADS):
        cols = np.asarray(_head_tile_cols(hd * (QK_NOPE + QK_ROPE), QK_NOPE))
        q_tiles += [w_uq[:, cols], zcol(w_uq, HEAD_PAD - len(cols))]
        kb = hd * (QK_NOPE + V_DIM)
        k_tiles += [w_ukv[:, kb:kb + QK_NOPE], zcol(w_ukv, HEAD_PAD - QK_NOPE)]
        w_v = w_ukv[:, kb + QK_NOPE:kb + QK_NOPE + V_DIM]
        pad = zcol(w_ukv, HEAD_PAD - V_DIM)
        v_cols += [w_v, pad] if hd % 2 == 0 else [pad, w_v]
    wq = jnp.concatenate(q_tiles, axis=1)
    wkv = jnp.concatenate(k_tiles + v_cols, axis=1)
    lane = np.arange(N_HEADS * HEAD_PAD)
    ones_here = ((lane % HEAD_PAD) >= V_DIM) == ((lane // HEAD_PAD) % 2 == 0)
    vone = jnp.asarray(ones_here.astype(np.float32))[None, :]
    return win.astype(BF16), wq.astype(BF16), wkv.astype(BF16), vone


def _rope_tabs(seq):
    inv = ROPE_THETA ** (-jnp.arange(0, QK_ROPE, 2, dtype=F32) / QK_ROPE)
    ang = jnp.arange(seq, dtype=F32)[:, None] * inv[None, :]
    cos, sin = jnp.cos(ang), jnp.sin(ang)
    one = jnp.ones((seq, QK_NOPE), F32)
    z_nope = jnp.zeros((seq, QK_NOPE), F32)
    z_pad = jnp.zeros((seq, HEAD_PAD - QK_NOPE - QK_ROPE), F32)
    scale = (QK_NOPE + QK_ROPE) ** -0.5 * float(np.log2(np.e))
    c_full = jnp.concatenate([one, cos, cos, z_pad], axis=1)
    s_full = jnp.concatenate([z_nope, -sin, sin, z_pad], axis=1)
    c_rope = jnp.concatenate([z_nope, cos, cos, z_pad], axis=1)
    return c_full * scale, s_full * scale, c_rope, s_full


def _tri(tt):
    r = np.arange(tt)
    return jnp.asarray((r[None, :] < r[:, None]).astype(np.float32), dtype=BF16)


def kernel(x_prompt, x_sample, c_prompt, c_sample, w_ada, b_ada, w_in, q_norm_g, kv_norm_g,
           w_uq, w_ukv, w_pool, pool_scale, w_out, w_router, w_gate, w_up, w_down, final_norm_g):
    groups = ((x_prompt, c_prompt), (x_sample, c_sample))
    n_c = sum(c.shape[0] for _, c in groups)
    c_all = jnp.concatenate([c for _, c in groups]
                            + [jnp.zeros((-n_c % 8, D_MODEL), F32)], axis=0)
    mod_all = _ada(c_all, w_ada[0], b_ada)

    win, wq, wkv, vone = _prep_weights(w_in[0], w_uq[0], w_ukv[0])
    qg = q_norm_g[0][None, :]
    kvg = kv_norm_g[0][None, :]
    wp = w_pool[0].astype(BF16)
    ps = pool_scale[0][None, :]
    woa = w_out[0][:D_ATT].astype(BF16)
    wob = w_out[0][D_ATT:].astype(BF16)
    wr = jnp.concatenate([w_router[0], jnp.zeros((D_MODEL, LANES - N_EXPERTS), F32)], axis=1)
    wg = w_gate[0].astype(BF16)
    wu = w_up[0].astype(BF16)
    wd = w_down[0].astype(BF16)
    fg = final_norm_g[None, :]
    tri = _tri(TT_MOE)

    caps = [CAPACITY_FACTOR * x.shape[0] * x.shape[1] // N_EXPERTS for x, _ in groups]
    live_rows = sum(caps)
    assert live_rows % TC_FFN == 0 and all(c % BF16_ROWS == 0 for c in caps)
    xe = jnp.zeros((N_EXPERTS, live_rows + TC_FFN, X_WIDTH), BF16)

    staged = []
    c_off = 0
    row0 = 0
    for (x, c), cap in zip(groups, caps):
        b, s, _ = x.shape
        mod = mod_all[c_off:c_off + b].reshape(b, N_MOD, D_MODEL)
        c_off += b
        q, k, v, u = _pre(x, mod, win, qg, kvg, wq, wkv, vone, _rope_tabs(s))
        o = _attn(q, k, v)
        x1, hx, aff = _mix(x, o, u, mod, wp, ps, woa, wob, wr)
        n_tok = b * s
        aff2d = aff.reshape(n_tok, LANES)
        thr, starts_v, eqs_v = _plan(aff2d, tri, cap)
        starts_s = starts_v[:, 0, :N_EXPERTS]
        xe = _disp(starts_s, hx.reshape(n_tok, X_WIDTH), aff2d, thr, starts_v, eqs_v, tri, xe, row0)
        staged.append((x1.reshape(n_tok, D_MODEL), aff2d, thr, starts_s, starts_v, eqs_v, mod,
                       row0, (b, s)))
        row0 += cap

    ye = _ffn(xe, wg, wu, wd, live_rows // TC_FFN)

    outs = []
    for x1, aff2d, thr, starts_s, starts_v, eqs_v, mod, r0, (b, s) in staged:
        y = _comb(starts_s, x1, aff2d, thr, starts_v, eqs_v, tri, mod, fg, ye, r0, s)
        outs.append(y.reshape(b, s, D_MODEL))
    return tuple(outs)
```

```python
import functools

import jax
import jax.numpy as jnp
import numpy as np
from jax import lax
from jax.experimental import pallas as pl
from jax.experimental.pallas import tpu as pltpu

F32 = jnp.float32
BF16 = jnp.bfloat16
I32 = jnp.int32

D_MODEL = 1024
N_HEADS = 8
QK_NOPE = 64
QK_ROPE = 32
V_DIM = 64
Q_LORA = 384
KV_LORA = 256
ROPE_THETA = 10000.0
D_ATT = N_HEADS * V_DIM
D_POOL = 512
POOL_WINDOWS = (2, 4, 8, 16)
POOL_GROUP = 128
N_EXPERTS = 16
CAPACITY_FACTOR = 2
EXPERT_FF = 1024
N_MOD = 6
EPS = 1e-6

LANES = 128
HEAD_PAD = LANES
HALF_ROPE = QK_ROPE // 2
D_IN_PAD = Q_LORA + KV_LORA + LANES + D_POOL
X_WIDTH = D_MODEL + LANES
POOL_HALO = 8
BF16_ROWS = 16

TS_PRE = 512
TQ_ATT = 512
ROWS_ATT = 512
TK_ATT = 1024
UNROLL_ATT = 4
TT_MOE = 256
CH_DISP = 64
CR_COMB = 64
TC_FFN = 512
VMEM_LIMIT = 56 * 1024 * 1024


def _cparams(sem):
    return pltpu.CompilerParams(dimension_semantics=sem, vmem_limit_bytes=VMEM_LIMIT)


def _rms(x):
    return x * lax.rsqrt(jnp.mean(x * x, axis=-1, keepdims=True) + EPS)


def _ada_kernel(c_ref, w_ref, b_ref, o_ref):
    c = c_ref[...]
    s = c * jax.nn.sigmoid(c)
    o_ref[...] = jnp.dot(s, w_ref[...], preferred_element_type=F32,
                         precision=lax.Precision.HIGHEST) + b_ref[...]


def _ada(c_pad, w_ada, b_ada):
    rows = c_pad.shape[0]
    return pl.pallas_call(
        _ada_kernel,
        out_shape=jax.ShapeDtypeStruct((rows, N_MOD * D_MODEL), F32),
        grid=(N_MOD,),
        in_specs=[pl.BlockSpec((rows, D_MODEL), lambda j: (0, 0)),
                  pl.BlockSpec((D_MODEL, D_MODEL), lambda j: (0, j)),
                  pl.BlockSpec((1, D_MODEL), lambda j: (0, j))],
        out_specs=pl.BlockSpec((rows, D_MODEL), lambda j: (0, j)),
        compiler_params=_cparams(("arbitrary",)),
        name="ada",
    )(c_pad, w_ada, b_ada)


def _pre_kernel(x_ref, mod_ref, win_ref, qg_ref, kvg_ref, wq_ref, wkv_ref,
                cq_ref, sq_ref, ck_ref, sk_ref, vone_ref, q_ref, k_ref, v_ref, u_ref):
    x = x_ref[0]
    sh1 = mod_ref[0, 0:1, :]
    sc1 = mod_ref[0, 1:2, :]
    h = (_rms(x) * (1.0 + sc1) + sh1).astype(BF16)
    z = jnp.dot(h, win_ref[...], preferred_element_type=F32)
    c_q = z[:, :Q_LORA]
    c_kv = z[:, Q_LORA:Q_LORA + KV_LORA]
    kpe = z[:, Q_LORA + KV_LORA:Q_LORA + KV_LORA + LANES]
    u_ref[0] = z[:, Q_LORA + KV_LORA + LANES:]

    q = jnp.dot((_rms(c_q) * qg_ref[...]).astype(BF16), wq_ref[...],
                preferred_element_type=F32)
    cq_t = cq_ref[...]
    sq_t = sq_ref[...]
    for hd in range(N_HEADS):
        blk = q[:, hd * HEAD_PAD:(hd + 1) * HEAD_PAD]
        rot = pltpu.roll(blk, HEAD_PAD - HALF_ROPE, 1)
        q_ref[0, :, hd * HEAD_PAD:(hd + 1) * HEAD_PAD] = (blk * cq_t + rot * sq_t).astype(BF16)

    kv = jnp.dot((_rms(c_kv) * kvg_ref[...]).astype(BF16), wkv_ref[...],
                 preferred_element_type=F32)
    kpe_r = kpe * ck_ref[...] + pltpu.roll(kpe, HEAD_PAD - HALF_ROPE, 1) * sk_ref[...]
    for hd in range(N_HEADS):
        k_ref[0, :, hd * HEAD_PAD:(hd + 1) * HEAD_PAD] = (
            kv[:, hd * HEAD_PAD:(hd + 1) * HEAD_PAD] + kpe_r).astype(BF16)
    v_ref[0] = (kv[:, N_HEADS * HEAD_PAD:] + vone_ref[...]).astype(BF16)


def _pre(x, mod, win, qg, kvg, wq, wkv, vone, tabs):
    b, s, _ = x.shape
    ts = TS_PRE
    const = lambda shape: pl.BlockSpec(shape, lambda bi, i: (0,) * len(shape))
    tab = pl.BlockSpec((ts, LANES), lambda bi, i: (i, 0))
    tok = lambda w: pl.BlockSpec((1, ts, w), lambda bi, i: (bi, i, 0))
    return pl.pallas_call(
        _pre_kernel,
        out_shape=(jax.ShapeDtypeStruct((b, s, N_HEADS * HEAD_PAD), BF16),
                   jax.ShapeDtypeStruct((b, s, N_HEADS * HEAD_PAD), BF16),
                   jax.ShapeDtypeStruct((b, s, N_HEADS * HEAD_PAD), BF16),
                   jax.ShapeDtypeStruct((b, s, D_POOL), F32)),
        grid=(b, s // ts),
        in_specs=[tok(D_MODEL),
                  pl.BlockSpec((1, N_MOD, D_MODEL), lambda bi, i: (bi, 0, 0)),
                  const((D_MODEL, D_IN_PAD)), const((1, Q_LORA)), const((1, KV_LORA)),
                  const((Q_LORA, N_HEADS * HEAD_PAD)),
                  const((KV_LORA, 2 * N_HEADS * HEAD_PAD)),
                  tab, tab, tab, tab, const((1, N_HEADS * HEAD_PAD))],
        out_specs=(tok(N_HEADS * HEAD_PAD), tok(N_HEADS * HEAD_PAD), tok(N_HEADS * HEAD_PAD),
                   tok(D_POOL)),
        compiler_params=_cparams(("parallel", "arbitrary")),
        name="pre",
    )(x, mod, win, qg, kvg, wq, wkv, *tabs, vone)


def _attn_kernel(q_ref, k_ref, v_ref, o_ref, *, seq, tq, tk, rows, unroll):
    chains = [(r, hh) for r in range(tq // rows) for hh in range(2)]

    def body(c, carry):
        off = pl.multiple_of(c * tk, tk)
        new = []
        for (r, hh), (m, acc) in zip(chains, carry):
            cols = slice(hh * HEAD_PAD, (hh + 1) * HEAD_PAD)
            qh = q_ref[0, r * rows:(r + 1) * rows, cols]
            s = lax.dot_general(qh, k_ref[0, pl.ds(off, tk), cols], (((1,), (1,)), ((), ())),
                                preferred_element_type=F32)
            m_new = jnp.maximum(m, jnp.max(s, axis=-1, keepdims=True))
            p = jnp.exp2((s - m_new).astype(BF16))
            acc = jnp.exp2(m - m_new) * acc + jnp.dot(p, v_ref[0, pl.ds(off, tk), cols],
                                                      preferred_element_type=F32)
            new.append((m_new, acc))
        return tuple(new)

    init = tuple((jnp.full((rows, 1), -jnp.inf, F32), jnp.zeros((rows, HEAD_PAD), F32))
                 for _ in chains)
    final = lax.fori_loop(0, seq // tk, body, init, unroll=unroll)
    lane = lax.broadcasted_iota(I32, (rows, HEAD_PAD), 1)
    for r in range(tq // rows):
        a0, a1 = final[2 * r][1], final[2 * r + 1][1]
        o_ref[0, r * rows:(r + 1) * rows, :] = jnp.where(
            lane < V_DIM, a0 / a0[:, V_DIM:V_DIM + 1], a1 / a1[:, 0:1]).astype(BF16)


def _attn(q, k, v):
    b, s, _ = q.shape
    tq, tk = TQ_ATT, TK_ATT
    assert s % tq == 0 and s % tk == 0
    return pl.pallas_call(
        functools.partial(_attn_kernel, seq=s, tq=tq, tk=tk, rows=ROWS_ATT, unroll=UNROLL_ATT),
        out_shape=jax.ShapeDtypeStruct((b, s, D_ATT), BF16),
        grid=(b, N_HEADS // 2, s // tq),
        in_specs=[pl.BlockSpec((1, tq, 2 * HEAD_PAD), lambda bi, p, i: (bi, i, p)),
                  pl.BlockSpec((1, s, 2 * HEAD_PAD), lambda bi, p, i: (bi, 0, p)),
                  pl.BlockSpec((1, s, 2 * HEAD_PAD), lambda bi, p, i: (bi, 0, p))],
        out_specs=pl.BlockSpec((1, tq, 2 * V_DIM), lambda bi, p, i: (bi, i, p)),
        compiler_params=_cparams(("parallel", "parallel", "arbitrary")),
        name="attn",
    )(q, k, v)


def _mix_kernel(x_ref, o_ref, u_ref, up_ref, un_ref, mod_ref, wp_ref, ps_ref, woa_ref, wob_ref,
                wr_ref, x1_ref, hx_ref, aff_ref, afft_ref, ue_ref, *, seq, ts):
    i = pl.program_id(1)
    nt = pl.num_programs(1)
    hb = POOL_HALO
    ue_ref[0:hb, :] = jnp.where(i > 0, up_ref[0], 0.0)
    ue_ref[hb:hb + ts, :] = u_ref[0]
    ue_ref[hb + ts:2 * hb + ts, :] = jnp.where(i < nt - 1, un_ref[0], 0.0)

    t = i * ts + lax.broadcasted_iota(I32, (ts, 1), 0)
    ys = []
    for g, w in enumerate(POOL_WINDOWS):
        cols = slice(g * POOL_GROUP, (g + 1) * POOL_GROUP)
        acc = None
        for off in range(-(w // 2), w - w // 2):
            sl = ue_ref[hb + off:hb + off + ts, cols]
            acc = sl if acc is None else acc + sl
        lo = jnp.maximum(t - w // 2, 0)
        hi = jnp.minimum(t + (w - w // 2), seq)
        cnt = (hi - lo).astype(F32)
        d = (acc / cnt - ue_ref[hb:hb + ts, cols]).astype(BF16)
        y = jnp.dot(d, wp_ref[g], preferred_element_type=F32) * ps_ref[:, cols]
        ys.append(y.astype(BF16))
    pool = jnp.concatenate(ys, axis=-1)
    mixed = (jnp.dot(o_ref[0], woa_ref[...], preferred_element_type=F32)
             + jnp.dot(pool, wob_ref[...], preferred_element_type=F32))
    g1 = mod_ref[0, 2:3, :]
    sh2 = mod_ref[0, 3:4, :]
    sc2 = mod_ref[0, 4:5, :]
    x1 = x_ref[0] + g1 * mixed
    x1_ref[0] = x1
    h2 = _rms(x1) * (1.0 + sc2) + sh2
    hx_ref[0, :, 0:D_MODEL] = h2.astype(BF16)

    logits = jnp.dot(h2, wr_ref[...], preferred_element_type=F32,
                     precision=lax.Precision.HIGHEST)
    lane = lax.broadcasted_iota(I32, (ts, LANES), 1)
    logits = jnp.where(lane < N_EXPERTS, logits, -jnp.inf)
    ex = jnp.exp(logits - jnp.max(logits, axis=-1, keepdims=True))
    aff = ex / jnp.sum(ex, axis=-1, keepdims=True)
    aff_ref[0] = aff
    afft_ref[0] = aff.T[0:N_EXPERTS, :]
    hi_part = aff.astype(BF16).astype(F32)
    lo_part = pltpu.roll(aff - hi_part, N_EXPERTS, 1)
    hx_ref[0, :, D_MODEL:] = jnp.where(lane < N_EXPERTS, hi_part, lo_part).astype(BF16)


def _mix(x, o, u, mod, wp, ps, woa, wob, wr):
    b, s, _ = x.shape
    ts = TS_PRE
    hpt = ts // POOL_HALO
    nh = s // POOL_HALO
    const = lambda shape: pl.BlockSpec(shape, lambda bi, i: (0,) * len(shape))
    tok = lambda w: pl.BlockSpec((1, ts, w), lambda bi, i: (bi, i, 0))
    return pl.pallas_call(
        functools.partial(_mix_kernel, seq=s, ts=ts),
        out_shape=(jax.ShapeDtypeStruct((b, s, D_MODEL), F32),
                   jax.ShapeDtypeStruct((b, s, X_WIDTH), BF16),
                   jax.ShapeDtypeStruct((b, s, LANES), F32),
                   jax.ShapeDtypeStruct((b, N_EXPERTS, s), F32)),
        grid=(b, s // ts),
        in_specs=[tok(D_MODEL), tok(D_ATT), tok(D_POOL),
                  pl.BlockSpec((1, POOL_HALO, D_POOL),
                               lambda bi, i: (bi, jnp.maximum(i * hpt - 1, 0), 0)),
                  pl.BlockSpec((1, POOL_HALO, D_POOL),
                               lambda bi, i: (bi, jnp.minimum((i + 1) * hpt, nh - 1), 0)),
                  pl.BlockSpec((1, N_MOD, D_MODEL), lambda bi, i: (bi, 0, 0)),
                  const((len(POOL_WINDOWS), POOL_GROUP, POOL_GROUP)), const((1, D_POOL)),
                  const((D_ATT, D_MODEL)), const((D_POOL, D_MODEL)), const((D_MODEL, LANES))],
        out_specs=(tok(D_MODEL), tok(X_WIDTH), tok(LANES),
                   pl.BlockSpec((1, N_EXPERTS, ts), lambda bi, i: (bi, 0, i))),
        scratch_shapes=[pltpu.VMEM((ts + 2 * POOL_HALO, D_POOL), F32)],
        compiler_params=_cparams(("parallel", "arbitrary")),
        name="mix",
    )(x, o, u, u, u, mod, wp, ps, woa, wob, wr)


def _plan_kernel(afft_ref, aff_ref, tri_ref, starts_ref, pos_ref, post_ref, *, nb, cap, tt):
    def count_ge(cand):
        def body(bi, acc):
            v = pltpu.bitcast(afft_ref[bi], I32)
            return acc + jnp.sum(jnp.where(v >= cand, 1.0, 0.0), axis=1, keepdims=True)
        return lax.fori_loop(0, nb, body, jnp.zeros((N_EXPERTS, 1), F32))

    def bit_body(kb, thr):
        cand = thr | lax.shift_left(jnp.int32(1), jnp.int32(30) - kb)
        return jnp.where(count_ge(cand) >= cap, cand, thr)

    thr_col = lax.fori_loop(0, 31, bit_body, jnp.zeros((N_EXPERTS, 1), I32))
    need_col = cap - count_ge(thr_col + 1)

    sub = lax.broadcasted_iota(I32, (N_EXPERTS, LANES), 0)
    lane = lax.broadcasted_iota(I32, (N_EXPERTS, LANES), 1)
    real = lane[0:1, :] < N_EXPERTS
    thr = jnp.max(jnp.where(sub == lane, thr_col, 0), axis=0, keepdims=True)
    thr = jnp.where(real, thr, jnp.int32(0x7FFFFFFF))
    need = jnp.max(jnp.where(sub == lane, need_col, 0.0), axis=0, keepdims=True)

    def tile_body(t, carry):
        sel_before, eq_before = carry
        off = pl.multiple_of(t * tt, tt)
        v = pltpu.bitcast(aff_ref[pl.ds(off, tt), :], I32)
        eq = v == thr
        eq_f = jnp.where(eq, 1.0, 0.0)
        tri = tri_ref[...]
        rank = jnp.dot(tri, eq_f.astype(BF16), preferred_element_type=F32) + eq_before
        sel = (v > thr) | (eq & (rank < need))
        sel_f = jnp.where(sel, 1.0, 0.0)
        pos = jnp.dot(tri, sel_f.astype(BF16), preferred_element_type=F32) + sel_before
        posm = jnp.where(sel, pos, -1.0)
        starts_ref[t] = jnp.broadcast_to(sel_before.astype(I32), (8, LANES))
        pos_ref[t] = posm
        post_ref[t] = posm.T[0:N_EXPERTS, :]
        return (sel_before + jnp.sum(sel_f, axis=0, keepdims=True),
                eq_before + jnp.sum(eq_f, axis=0, keepdims=True))

    nt = pos_ref.shape[0]
    zero = jnp.zeros((1, LANES), F32)
    sel_total, _ = lax.fori_loop(0, nt, tile_body, (zero, zero))
    starts_ref[nt] = jnp.broadcast_to(sel_total.astype(I32), (8, LANES))


def _plan(afft, aff2d, tri, cap):
    nb = afft.shape[0]
    n_tok = aff2d.shape[0]
    tt = TT_MOE
    nt = n_tok // tt
    vm = pl.BlockSpec(memory_space=pltpu.VMEM)
    return pl.pallas_call(
        functools.partial(_plan_kernel, nb=nb, cap=cap, tt=tt),
        out_shape=(jax.ShapeDtypeStruct((nt + 1, 8, LANES), I32),
                   jax.ShapeDtypeStruct((nt, tt, LANES), F32),
                   jax.ShapeDtypeStruct((nt, N_EXPERTS, tt), F32)),
        in_specs=[vm, vm, vm],
        out_specs=(vm, vm, vm),
        compiler_params=pltpu.CompilerParams(vmem_limit_bytes=VMEM_LIMIT),
        name="plan",
    )(afft, aff2d, tri)


def _disp_copy(stage, xe_ref, sems, e, row):
    return pltpu.make_async_copy(stage.at[pl.ds(e * CH_DISP, CH_DISP)],
                                 xe_ref.at[e, pl.ds(row, CH_DISP)], sems.at[e])


def _disp_kernel(starts_s, hx_ref, post_ref, xe_in, xe_ref, stage, carry, sems, *, row0, nt, tt):
    del xe_in
    t = pl.program_id(0)

    @pl.when(t == 0)
    def _():
        carry[...] = jnp.zeros(carry.shape, carry.dtype)

    @pl.when(t > 0)
    def _():
        for e in range(N_EXPERTS):
            _disp_copy(stage, xe_ref, sems, e, 0).wait()

    pos_t = post_ref[0]
    slot_iota = lax.broadcasted_iota(I32, (CH_DISP, tt), 0).astype(F32)

    def window(e, base):
        onehot = jnp.where(pos_t[e:e + 1, :] - base.astype(F32) == slot_iota, 1.0, 0.0)
        return onehot.astype(BF16)

    def carry_rows(e, base):
        off = (starts_s[t + 1, e] & -BF16_ROWS) - base
        inside = jnp.minimum(off, CH_DISP - BF16_ROWS)
        blk = stage[pl.ds(pl.multiple_of(e * CH_DISP + inside, BF16_ROWS), BF16_ROWS), :]
        return jnp.where(off < CH_DISP, blk, jnp.zeros_like(blk))

    bases = [starts_s[t, e] & -BF16_ROWS for e in range(N_EXPERTS)]
    onehot = jnp.concatenate([window(e, bases[e]) for e in range(N_EXPERTS)], axis=0)
    stage[...] = jnp.dot(onehot, hx_ref[...], preferred_element_type=F32).astype(BF16)
    for e in range(N_EXPERTS):
        head = pl.ds(e * CH_DISP, BF16_ROWS)
        stage[head, :] = (stage[head, :].astype(F32) + carry[e].astype(F32)).astype(BF16)
        _disp_copy(stage, xe_ref, sems, e, row0 + pl.multiple_of(bases[e], BF16_ROWS)).start()
        carry[e] = carry_rows(e, bases[e])

    n_win = [(starts_s[t + 1, e] - bases[e] + (CH_DISP - 1)) // CH_DISP
             for e in range(N_EXPERTS)]
    most = functools.reduce(jnp.maximum, n_win)

    @pl.when(most > 1)
    def _():
        for e in range(N_EXPERTS):
            def more(c, _, e=e):
                cb = pl.multiple_of(bases[e] + c * CH_DISP, BF16_ROWS)
                extra = jnp.dot(window(e, cb), hx_ref[...], preferred_element_type=F32)
                _disp_copy(stage, xe_ref, sems, e, 0).wait()
                stage[pl.ds(e * CH_DISP, CH_DISP), :] = extra.astype(BF16)
                _disp_copy(stage, xe_ref, sems, e, row0 + cb).start()
                return 0

            lax.fori_loop(1, n_win[e], more, 0)

            @pl.when(n_win[e] > 1)
            def _(e=e):
                carry[e] = carry_rows(e, bases[e] + (n_win[e] - 1) * CH_DISP)

    @pl.when(t == nt - 1)
    def _():
        for e in range(N_EXPERTS):
            _disp_copy(stage, xe_ref, sems, e, 0).wait()


def _disp(starts_s, hx2d, post, xe, row0):
    n_tok = hx2d.shape[0]
    tt = TT_MOE
    nt = n_tok // tt
    grid_spec = pltpu.PrefetchScalarGridSpec(
        num_scalar_prefetch=1,
        grid=(nt,),
        in_specs=[pl.BlockSpec((tt, X_WIDTH), lambda t, s: (t, 0)),
                  pl.BlockSpec((1, N_EXPERTS, tt), lambda t, s: (t, 0, 0)),
                  pl.BlockSpec(memory_space=pl.ANY)],
        out_specs=pl.BlockSpec(memory_space=pl.ANY),
        scratch_shapes=[pltpu.VMEM((N_EXPERTS * CH_DISP, X_WIDTH), BF16),
                        pltpu.VMEM((N_EXPERTS, BF16_ROWS, X_WIDTH), BF16),
                        pltpu.SemaphoreType.DMA((N_EXPERTS,))])
    return pl.pallas_call(
        functools.partial(_disp_kernel, row0=row0, nt=nt, tt=tt),
        out_shape=jax.ShapeDtypeStruct(xe.shape, xe.dtype),
        grid_spec=grid_spec,
        input_output_aliases={3: 0},
        compiler_params=_cparams(("arbitrary",)),
        name="disp",
    )(starts_s, hx2d, post, xe)


def _ffn_kernel(xe_ref, wg_ref, wu_ref, wd_ref, ye_ref, *, n_live):
    e = pl.program_id(0)
    j = pl.program_id(1)

    @pl.when(j < n_live)
    def _():
        blk = xe_ref[0]
        x = blk[:, 0:D_MODEL]
        gcols = blk[:, D_MODEL:].astype(F32)
        lane = lax.broadcasted_iota(I32, gcols.shape, 1)
        gate = jnp.sum(jnp.where((lane == e) | (lane == e + N_EXPERTS), gcols, 0.0),
                       axis=-1, keepdims=True)
        a = jnp.dot(x, wg_ref[0], preferred_element_type=F32)
        u = jnp.dot(x, wu_ref[0], preferred_element_type=F32)
        hmid = (a * jax.nn.sigmoid(a) * u).astype(BF16)
        y = jnp.dot(hmid, wd_ref[0], preferred_element_type=F32) * gate
        ye_ref[0] = y.astype(BF16)

    @pl.when(j >= n_live)
    def _():
        ye_ref[...] = jnp.zeros(ye_ref.shape, ye_ref.dtype)


def _ffn(xe, wg, wu, wd, n_live):
    _, rows, _ = xe.shape
    tc = TC_FFN
    wspec = pl.BlockSpec((1, D_MODEL, EXPERT_FF), lambda e, j: (e, 0, 0))
    return pl.pallas_call(
        functools.partial(_ffn_kernel, n_live=n_live),
        out_shape=jax.ShapeDtypeStruct((N_EXPERTS, rows, D_MODEL), BF16),
        grid=(N_EXPERTS, rows // tc),
        in_specs=[pl.BlockSpec((1, tc, X_WIDTH), lambda e, j: (e, j, 0)), wspec, wspec,
                  pl.BlockSpec((1, EXPERT_FF, D_MODEL), lambda e, j: (e, 0, 0))],
        out_specs=pl.BlockSpec((1, tc, D_MODEL), lambda e, j: (e, j, 0)),
        compiler_params=_cparams(("parallel", "arbitrary")),
        name="ffn",
    )(xe, wg, wu, wd)


def _comb_copy(ye_ref, buf, sems, slot, e, row):
    return pltpu.make_async_copy(ye_ref.at[e, pl.ds(row, CR_COMB)],
                                 buf.at[slot, pl.ds(e * CR_COMB, CR_COMB)], sems.at[slot])


def _comb_kernel(starts_s, x1_ref, pos_ref, mod_ref, fg_ref, ye_ref, y_ref, buf, xbuf, acc_ref,
                 sems, xsem, *, row0, nt, tt):
    t = pl.program_id(0)
    slot = lax.rem(t, 2)

    def issue(tile, sl):
        for e in range(N_EXPERTS):
            b0 = pl.multiple_of(starts_s[tile, e] & -BF16_ROWS, BF16_ROWS)
            _comb_copy(ye_ref, buf, sems, sl, e, row0 + b0).start()

    @pl.when(t == 0)
    def _():
        issue(0, 0)

    @pl.when(t + 1 < nt)
    def _():
        issue(t + 1, 1 - slot)

    posm = pos_ref[0]
    bases = [starts_s[t, e] & -BF16_ROWS for e in range(N_EXPERTS)]
    lane = lax.broadcasted_iota(I32, (tt, LANES), 1)
    lane_f = lane.astype(F32)
    low = lane < CR_COMB
    ohs = []
    for e in range(0, N_EXPERTS, 2):
        rel0 = posm[:, e:e + 1] - bases[e].astype(F32)
        rel1 = posm[:, e + 1:e + 2] - (bases[e + 1] - CR_COMB).astype(F32)
        ohs.append(jnp.where(jnp.where(low, rel0, rel1) == lane_f, 1.0, 0.0).astype(BF16))
    onehot = jnp.concatenate(ohs, axis=-1)

    for e in range(N_EXPERTS):
        _comb_copy(ye_ref, buf, sems, slot, e, 0).wait()
    acc_ref[...] = jnp.dot(onehot, buf[slot], preferred_element_type=F32)

    n_win = [(starts_s[t + 1, e] - bases[e] + (CR_COMB - 1)) // CR_COMB
             for e in range(N_EXPERTS)]
    most = functools.reduce(jnp.maximum, n_win)

    @pl.when(most > 1)
    def _():
        win_iota = lax.broadcasted_iota(I32, (tt, CR_COMB), 1).astype(F32)
        for e in range(N_EXPERTS):
            def extra(c, _, e=e):
                cb = pl.multiple_of(bases[e] + c * CR_COMB, BF16_ROWS)
                cp = pltpu.make_async_copy(ye_ref.at[e, pl.ds(row0 + cb, CR_COMB)], xbuf,
                                           xsem.at[0])
                cp.start()
                cp.wait()
                rel = posm[:, e:e + 1] - cb.astype(F32)
                oh = jnp.where(rel == win_iota, 1.0, 0.0).astype(BF16)
                acc_ref[...] += jnp.dot(oh, xbuf[...], preferred_element_type=F32)
                return 0

            lax.fori_loop(1, n_win[e], extra, 0)

    g2 = mod_ref[0, 5:6, :]
    y_ref[...] = _rms(x1_ref[...] + g2 * acc_ref[...]) * fg_ref[...]


def _comb(starts_s, x1_2d, pos, mod, fg, ye, row0, seq):
    n_tok = x1_2d.shape[0]
    tt = TT_MOE
    nt = n_tok // tt
    tiles_per_seq = seq // tt
    assert 2 * CR_COMB == LANES
    grid_spec = pltpu.PrefetchScalarGridSpec(
        num_scalar_prefetch=1,
        grid=(nt,),
        in_specs=[pl.BlockSpec((tt, D_MODEL), lambda t, s: (t, 0)),
                  pl.BlockSpec((1, tt, LANES), lambda t, s: (t, 0, 0)),
                  pl.BlockSpec((1, N_MOD, D_MODEL), lambda t, s: (t // tiles_per_seq, 0, 0)),
                  pl.BlockSpec((1, D_MODEL), lambda t, s: (0, 0)),
                  pl.BlockSpec(memory_space=pl.ANY)],
        out_specs=pl.BlockSpec((tt, D_MODEL), lambda t, s: (t, 0)),
        scratch_shapes=[pltpu.VMEM((2, N_EXPERTS * CR_COMB, D_MODEL), BF16),
                        pltpu.VMEM((CR_COMB, D_MODEL), BF16),
                        pltpu.VMEM((tt, D_MODEL), F32),
                        pltpu.SemaphoreType.DMA((2,)),
                        pltpu.SemaphoreType.DMA((1,))])
    return pl.pallas_call(
        functools.partial(_comb_kernel, row0=row0, nt=nt, tt=tt),
        out_shape=jax.ShapeDtypeStruct((n_tok, D_MODEL), F32),
        grid_spec=grid_spec,
        compiler_params=_cparams(("arbitrary",)),
        name="comb",
    )(starts_s, x1_2d, pos, mod, fg, ye)


def _head_tile_cols(base, nope):
    r = base + nope
    return (list(range(base, base + nope)) + list(range(r, r + QK_ROPE))
            + list(range(r, r + HALF_ROPE)))


def _prep_weights(w_in, w_uq, w_ukv):
    zcol = lambda w, n: jnp.zeros((w.shape[0], n), w.dtype)
    kpe0 = Q_LORA + KV_LORA
    kpe_tile = jnp.concatenate(
        [zcol(w_in, QK_NOPE), w_in[:, kpe0:kpe0 + QK_ROPE], w_in[:, kpe0:kpe0 + HALF_ROPE],
         zcol(w_in, HEAD_PAD - QK_NOPE - QK_ROPE - HALF_ROPE)], axis=1)
    win = jnp.concatenate([w_in[:, :kpe0], kpe_tile, w_in[:, kpe0 + QK_ROPE:]], axis=1)

    q_tiles, k_tiles, v_cols = [], [], []
    for hd in range(N_HEADS):
        cols = np.asarray(_head_tile_cols(hd * (QK_NOPE + QK_ROPE), QK_NOPE))
        q_tiles += [w_uq[:, cols], zcol(w_uq, HEAD_PAD - len(cols))]
        kb = hd * (QK_NOPE + V_DIM)
        k_tiles += [w_ukv[:, kb:kb + QK_NOPE], zcol(w_ukv, HEAD_PAD - QK_NOPE)]
        w_v = w_ukv[:, kb + QK_NOPE:kb + QK_NOPE + V_DIM]
        pad = zcol(w_ukv, HEAD_PAD - V_DIM)
        v_cols += [w_v, pad] if hd % 2 == 0 else [pad, w_v]
    wq = jnp.concatenate(q_tiles, axis=1)
    wkv = jnp.concatenate(k_tiles + v_cols, axis=1)
    lane = np.arange(N_HEADS * HEAD_PAD)
    ones_here = ((lane % HEAD_PAD) >= V_DIM) == ((lane // HEAD_PAD) % 2 == 0)
    vone = jnp.asarray(ones_here.astype(np.float32))[None, :]
    return win.astype(BF16), wq.astype(BF16), wkv.astype(BF16), vone


def _rope_tabs(seq):
    inv = ROPE_THETA ** (-jnp.arange(0, QK_ROPE, 2, dtype=F32) / QK_ROPE)
    ang = jnp.arange(seq, dtype=F32)[:, None] * inv[None, :]
    cos, sin = jnp.cos(ang), jnp.sin(ang)
    one = jnp.ones((seq, QK_NOPE), F32)
    z_nope = jnp.zeros((seq, QK_NOPE), F32)
    z_pad = jnp.zeros((seq, HEAD_PAD - QK_NOPE - QK_ROPE), F32)
    scale = (QK_NOPE + QK_ROPE) ** -0.5 * float(np.log2(np.e))
    c_full = jnp.concatenate([one, cos, cos, z_pad], axis=1)
    s_full = jnp.concatenate([z_nope, -sin, sin, z_pad], axis=1)
    c_rope = jnp.concatenate([z_nope, cos, cos, z_pad], axis=1)
    return c_full * scale, s_full * scale, c_rope, s_full


def _tri(tt):
    r = np.arange(tt)
    return jnp.asarray((r[None, :] < r[:, None]).astype(np.float32), dtype=BF16)


def kernel(x_prompt, x_sample, c_prompt, c_sample, w_ada, b_ada, w_in, q_norm_g, kv_norm_g,
           w_uq, w_ukv, w_pool, pool_scale, w_out, w_router, w_gate, w_up, w_down, final_norm_g):
    groups = ((x_prompt, c_prompt), (x_sample, c_sample))
    n_c = sum(c.shape[0] for _, c in groups)
    c_all = jnp.concatenate([c for _, c in groups]
                            + [jnp.zeros((-n_c % 8, D_MODEL), F32)], axis=0)
    mod_all = _ada(c_all, w_ada[0], b_ada)

    win, wq, wkv, vone = _prep_weights(w_in[0], w_uq[0], w_ukv[0])
    qg = q_norm_g[0][None, :]
    kvg = kv_norm_g[0][None, :]
    wp = w_pool[0].astype(BF16)
    ps = pool_scale[0][None, :]
    woa = w_out[0][:D_ATT].astype(BF16)
    wob = w_out[0][D_ATT:].astype(BF16)
    wr = jnp.concatenate([w_router[0], jnp.zeros((D_MODEL, LANES - N_EXPERTS), F32)], axis=1)
    wg = w_gate[0].astype(BF16)
    wu = w_up[0].astype(BF16)
    wd = w_down[0].astype(BF16)
    fg = final_norm_g[None, :]
    tri = _tri(TT_MOE)

    caps = [CAPACITY_FACTOR * x.shape[0] * x.shape[1] // N_EXPERTS for x, _ in groups]
    live_rows = sum(caps)
    assert live_rows % TC_FFN == 0 and all(c % BF16_ROWS == 0 for c in caps)
    xe = jnp.zeros((N_EXPERTS, live_rows + TC_FFN, X_WIDTH), BF16)

    staged = []
    c_off = 0
    row0 = 0
    for (x, c), cap in zip(groups, caps):
        b, s, _ = x.shape
        mod = mod_all[c_off:c_off + b].reshape(b, N_MOD, D_MODEL)
        c_off += b
        q, k, v, u = _pre(x, mod, win, qg, kvg, wq, wkv, vone, _rope_tabs(s))
        o = _attn(q, k, v)
        x1, hx, aff, afft = _mix(x, o, u, mod, wp, ps, woa, wob, wr)
        n_tok = b * s
        starts_v, pos, post = _plan(afft, aff.reshape(n_tok, LANES), tri, cap)
        starts_s = starts_v[:, 0, :N_EXPERTS]
        xe = _disp(starts_s, hx.reshape(n_tok, X_WIDTH), post, xe, row0)
        staged.append((x1.reshape(n_tok, D_MODEL), pos, starts_s, mod, row0, (b, s)))
        row0 += cap

    ye = _ffn(xe, wg, wu, wd, live_rows // TC_FFN)

    outs = []
    for x1, pos, starts_s, mod, r0, (b, s) in staged:
        y = _comb(starts_s, x1, pos, mod, fg, ye, r0, s)
        outs.append(y.reshape(b, s, D_MODEL))
    return tuple(outs)
```

```python
import functools

import jax
import jax.numpy as jnp
import numpy as np
from jax import lax
from jax.experimental import pallas as pl
from jax.experimental.pallas import tpu as pltpu

F32 = jnp.float32
BF16 = jnp.bfloat16
I32 = jnp.int32

D_MODEL = 1024
N_HEADS = 8
QK_NOPE = 64
QK_ROPE = 32
V_DIM = 64
Q_LORA = 384
KV_LORA = 256
ROPE_THETA = 10000.0
D_ATT = N_HEADS * V_DIM
D_POOL = 512
POOL_WINDOWS = (2, 4, 8, 16)
POOL_GROUP = 128
N_EXPERTS = 16
CAPACITY_FACTOR = 2
EXPERT_FF = 1024
N_MOD = 6
EPS = 1e-6

LANES = 128
HEAD_PAD = LANES
HALF_ROPE = QK_ROPE // 2
D_IN_PAD = Q_LORA + KV_LORA + LANES + D_POOL
X_WIDTH = D_MODEL + LANES
POOL_HALO = 8
BF16_ROWS = 16

TS_PRE = 512
TQ_ATT = 512
ROWS_ATT = 512
TK_ATT = 1024
UNROLL_ATT = 4
TT_MOE = 256
CH_DISP = 64
CR_COMB = 64
TC_FFN = 512
PLAN_UNROLL = 4
MIN_NORMAL = float(np.finfo(np.float32).tiny)
GEO_STEPS = 32
BISECT_STEPS = GEO_STEPS + 6
VMEM_LIMIT = 56 * 1024 * 1024


def _cparams(sem):
    return pltpu.CompilerParams(dimension_semantics=sem, vmem_limit_bytes=VMEM_LIMIT)


def _rms(x):
    return x * lax.rsqrt(jnp.mean(x * x, axis=-1, keepdims=True) + EPS)


def _ada_kernel(c_ref, w_ref, b_ref, o_ref):
    c = c_ref[...]
    s = c * jax.nn.sigmoid(c)
    o_ref[...] = jnp.dot(s, w_ref[...], preferred_element_type=F32,
                         precision=lax.Precision.HIGHEST) + b_ref[...]


def _ada(c_pad, w_ada, b_ada):
    rows = c_pad.shape[0]
    return pl.pallas_call(
        _ada_kernel,
        out_shape=jax.ShapeDtypeStruct((rows, N_MOD * D_MODEL), F32),
        grid=(N_MOD,),
        in_specs=[pl.BlockSpec((rows, D_MODEL), lambda j: (0, 0)),
                  pl.BlockSpec((D_MODEL, D_MODEL), lambda j: (0, j)),
                  pl.BlockSpec((1, D_MODEL), lambda j: (0, j))],
        out_specs=pl.BlockSpec((rows, D_MODEL), lambda j: (0, j)),
        compiler_params=_cparams(("arbitrary",)),
        name="ada",
    )(c_pad, w_ada, b_ada)


def _pre_kernel(x_ref, mod_ref, win_ref, qg_ref, kvg_ref, wq_ref, wkv_ref,
                cq_ref, sq_ref, ck_ref, sk_ref, vone_ref, q_ref, k_ref, v_ref, u_ref):
    x = x_ref[0]
    sh1 = mod_ref[0, 0:1, :]
    sc1 = mod_ref[0, 1:2, :]
    h = (_rms(x) * (1.0 + sc1) + sh1).astype(BF16)
    z = jnp.dot(h, win_ref[...], preferred_element_type=F32)
    c_q = z[:, :Q_LORA]
    c_kv = z[:, Q_LORA:Q_LORA + KV_LORA]
    kpe = z[:, Q_LORA + KV_LORA:Q_LORA + KV_LORA + LANES]
    u_ref[0] = z[:, Q_LORA + KV_LORA + LANES:]

    q = jnp.dot((_rms(c_q) * qg_ref[...]).astype(BF16), wq_ref[...],
                preferred_element_type=F32)
    cq_t = cq_ref[...]
    sq_t = sq_ref[...]
    for hd in range(N_HEADS):
        blk = q[:, hd * HEAD_PAD:(hd + 1) * HEAD_PAD]
        rot = pltpu.roll(blk, HEAD_PAD - HALF_ROPE, 1)
        q_ref[0, :, hd * HEAD_PAD:(hd + 1) * HEAD_PAD] = (blk * cq_t + rot * sq_t).astype(BF16)

    kv = jnp.dot((_rms(c_kv) * kvg_ref[...]).astype(BF16), wkv_ref[...],
                 preferred_element_type=F32)
    kpe_r = kpe * ck_ref[...] + pltpu.roll(kpe, HEAD_PAD - HALF_ROPE, 1) * sk_ref[...]
    for hd in range(N_HEADS):
        k_ref[0, :, hd * HEAD_PAD:(hd + 1) * HEAD_PAD] = (
            kv[:, hd * HEAD_PAD:(hd + 1) * HEAD_PAD] + kpe_r).astype(BF16)
    v_ref[0] = (kv[:, N_HEADS * HEAD_PAD:] + vone_ref[...]).astype(BF16)


def _pre(x, mod, win, qg, kvg, wq, wkv, vone, tabs):
    b, s, _ = x.shape
    ts = TS_PRE
    const = lambda shape: pl.BlockSpec(shape, lambda bi, i: (0,) * len(shape))
    tab = pl.BlockSpec((ts, LANES), lambda bi, i: (i, 0))
    tok = lambda w: pl.BlockSpec((1, ts, w), lambda bi, i: (bi, i, 0))
    return pl.pallas_call(
        _pre_kernel,
        out_shape=(jax.ShapeDtypeStruct((b, s, N_HEADS * HEAD_PAD), BF16),
                   jax.ShapeDtypeStruct((b, s, N_HEADS * HEAD_PAD), BF16),
                   jax.ShapeDtypeStruct((b, s, N_HEADS * HEAD_PAD), BF16),
                   jax.ShapeDtypeStruct((b, s, D_POOL), F32)),
        grid=(b, s // ts),
        in_specs=[tok(D_MODEL),
                  pl.BlockSpec((1, N_MOD, D_MODEL), lambda bi, i: (bi, 0, 0)),
                  const((D_MODEL, D_IN_PAD)), const((1, Q_LORA)), const((1, KV_LORA)),
                  const((Q_LORA, N_HEADS * HEAD_PAD)),
                  const((KV_LORA, 2 * N_HEADS * HEAD_PAD)),
                  tab, tab, tab, tab, const((1, N_HEADS * HEAD_PAD))],
        out_specs=(tok(N_HEADS * HEAD_PAD), tok(N_HEADS * HEAD_PAD), tok(N_HEADS * HEAD_PAD),
                   tok(D_POOL)),
        compiler_params=_cparams(("parallel", "arbitrary")),
        name="pre",
    )(x, mod, win, qg, kvg, wq, wkv, *tabs, vone)


def _attn_kernel(q_ref, k_ref, v_ref, o_ref, *, seq, tq, tk, rows, unroll):
    chains = [(r, hh) for r in range(tq // rows) for hh in range(2)]

    def body(c, carry):
        off = pl.multiple_of(c * tk, tk)
        new = []
        for (r, hh), (m, acc) in zip(chains, carry):
            cols = slice(hh * HEAD_PAD, (hh + 1) * HEAD_PAD)
            qh = q_ref[0, r * rows:(r + 1) * rows, cols]
            s = lax.dot_general(qh, k_ref[0, pl.ds(off, tk), cols], (((1,), (1,)), ((), ())),
                                preferred_element_type=F32)
            m_new = jnp.maximum(m, jnp.max(s, axis=-1, keepdims=True))
            p = jnp.exp2((s - m_new).astype(BF16))
            acc = jnp.exp2(m - m_new) * acc + jnp.dot(p, v_ref[0, pl.ds(off, tk), cols],
                                                      preferred_element_type=F32)
            new.append((m_new, acc))
        return tuple(new)

    init = tuple((jnp.full((rows, 1), -jnp.inf, F32), jnp.zeros((rows, HEAD_PAD), F32))
                 for _ in chains)
    final = lax.fori_loop(0, seq // tk, body, init, unroll=unroll)
    lane = lax.broadcasted_iota(I32, (rows, HEAD_PAD), 1)
    for r in range(tq // rows):
        a0, a1 = final[2 * r][1], final[2 * r + 1][1]
        o_ref[0, r * rows:(r + 1) * rows, :] = jnp.where(
            lane < V_DIM, a0 / a0[:, V_DIM:V_DIM + 1], a1 / a1[:, 0:1]).astype(BF16)


def _attn(q, k, v):
    b, s, _ = q.shape
    tq, tk = TQ_ATT, TK_ATT
    assert s % tq == 0 and s % tk == 0
    return pl.pallas_call(
        functools.partial(_attn_kernel, seq=s, tq=tq, tk=tk, rows=ROWS_ATT, unroll=UNROLL_ATT),
        out_shape=jax.ShapeDtypeStruct((b, s, D_ATT), BF16),
        grid=(b, N_HEADS // 2, s // tq),
        in_specs=[pl.BlockSpec((1, tq, 2 * HEAD_PAD), lambda bi, p, i: (bi, i, p)),
                  pl.BlockSpec((1, s, 2 * HEAD_PAD), lambda bi, p, i: (bi, 0, p)),
                  pl.BlockSpec((1, s, 2 * HEAD_PAD), lambda bi, p, i: (bi, 0, p))],
        out_specs=pl.BlockSpec((1, tq, 2 * V_DIM), lambda bi, p, i: (bi, i, p)),
        compiler_params=_cparams(("parallel", "parallel", "arbitrary")),
        name="attn",
    )(q, k, v)


def _mix_kernel(x_ref, o_ref, u_ref, up_ref, un_ref, mod_ref, wp_ref, ps_ref, woa_ref, wob_ref,
                wr_ref, icnt_ref, x1_ref, hx_ref, aff_ref, afft_ref, ue_ref, *, ts):
    i = pl.program_id(1)
    nt = pl.num_programs(1)
    hb = POOL_HALO
    ue_ref[0:hb, :] = jnp.where(i > 0, up_ref[0], 0.0)
    ue_ref[hb:hb + ts, :] = u_ref[0]
    ue_ref[hb + ts:2 * hb + ts, :] = jnp.where(i < nt - 1, un_ref[0], 0.0)

    ys = []
    for g, w in enumerate(POOL_WINDOWS):
        cols = slice(g * POOL_GROUP, (g + 1) * POOL_GROUP)
        acc = None
        for off in range(-(w // 2), w - w // 2):
            sl = ue_ref[hb + off:hb + off + ts, cols]
            acc = sl if acc is None else acc + sl
        d = (acc * icnt_ref[:, g:g + 1] - ue_ref[hb:hb + ts, cols]).astype(BF16)
        y = jnp.dot(d, wp_ref[g], preferred_element_type=F32) * ps_ref[:, cols]
        ys.append(y.astype(BF16))
    pool = jnp.concatenate(ys, axis=-1)
    mixed = (jnp.dot(o_ref[0], woa_ref[...], preferred_element_type=F32)
             + jnp.dot(pool, wob_ref[...], preferred_element_type=F32))
    g1 = mod_ref[0, 2:3, :]
    sh2 = mod_ref[0, 3:4, :]
    sc2 = mod_ref[0, 4:5, :]
    x1 = x_ref[0] + g1 * mixed
    x1_ref[0] = x1
    h2 = _rms(x1) * (1.0 + sc2) + sh2
    hx_ref[0, :, 0:D_MODEL] = h2.astype(BF16)

    h_hi = h2.astype(BF16)
    h_lo = (h2 - h_hi.astype(F32)).astype(BF16)
    both = jnp.dot(h_hi, wr_ref[...], preferred_element_type=F32)
    logits = (both[:, :LANES] + both[:, LANES:]
              + jnp.dot(h_lo, wr_ref[:, :LANES], preferred_element_type=F32))
    lane = lax.broadcasted_iota(I32, (ts, LANES), 1)
    logits = jnp.where(lane < N_EXPERTS, logits, -jnp.inf)
    ex = jnp.exp(logits - jnp.max(logits, axis=-1, keepdims=True))
    aff = ex / jnp.sum(ex, axis=-1, keepdims=True)
    aff_ref[0] = aff
    afft_ref[0] = aff.T[0:N_EXPERTS, :]
    hi_part = aff.astype(BF16).astype(F32)
    lo_part = pltpu.roll(aff - hi_part, N_EXPERTS, 1)
    hx_ref[0, :, D_MODEL:] = jnp.where(lane < N_EXPERTS, hi_part, lo_part).astype(BF16)


def _pool_inv_counts(seq):
    t = np.arange(seq)[:, None]
    w = np.asarray(POOL_WINDOWS)[None, :]
    cnt = np.minimum(t + (w - w // 2), seq) - np.maximum(t - w // 2, 0)
    tab = np.zeros((seq, LANES), np.float64)
    tab[:, :len(POOL_WINDOWS)] = 1.0 / cnt
    return jnp.asarray(tab, dtype=F32)


def _mix(x, o, u, mod, wp, ps, woa, wob, wr):
    b, s, _ = x.shape
    ts = TS_PRE
    hpt = ts // POOL_HALO
    nh = s // POOL_HALO
    const = lambda shape: pl.BlockSpec(shape, lambda bi, i: (0,) * len(shape))
    tok = lambda w: pl.BlockSpec((1, ts, w), lambda bi, i: (bi, i, 0))
    return pl.pallas_call(
        functools.partial(_mix_kernel, ts=ts),
        out_shape=(jax.ShapeDtypeStruct((b, s, D_MODEL), F32),
                   jax.ShapeDtypeStruct((b, s, X_WIDTH), BF16),
                   jax.ShapeDtypeStruct((b, s, LANES), F32),
                   jax.ShapeDtypeStruct((b, N_EXPERTS, s), F32)),
        grid=(b, s // ts),
        in_specs=[tok(D_MODEL), tok(D_ATT), tok(D_POOL),
                  pl.BlockSpec((1, POOL_HALO, D_POOL),
                               lambda bi, i: (bi, jnp.maximum(i * hpt - 1, 0), 0)),
                  pl.BlockSpec((1, POOL_HALO, D_POOL),
                               lambda bi, i: (bi, jnp.minimum((i + 1) * hpt, nh - 1), 0)),
                  pl.BlockSpec((1, N_MOD, D_MODEL), lambda bi, i: (bi, 0, 0)),
                  const((len(POOL_WINDOWS), POOL_GROUP, POOL_GROUP)), const((1, D_POOL)),
                  const((D_ATT, D_MODEL)), const((D_POOL, D_MODEL)), const((D_MODEL, 2 * LANES)),
                  pl.BlockSpec((ts, LANES), lambda bi, i: (i, 0))],
        out_specs=(tok(D_MODEL), tok(X_WIDTH), tok(LANES),
                   pl.BlockSpec((1, N_EXPERTS, ts), lambda bi, i: (bi, 0, i))),
        scratch_shapes=[pltpu.VMEM((ts + 2 * POOL_HALO, D_POOL), F32)],
        compiler_params=_cparams(("parallel", "arbitrary")),
        name="mix",
    )(x, o, u, u, u, mod, wp, ps, woa, wob, wr, _pool_inv_counts(s))


def _plan_kernel(afft_ref, aff_ref, tri_ref, starts_ref, pos_ref, post_ref, *, nb, cap, tt):
    def count_ge(cand):
        def body(bi, acc):
            return acc + jnp.sum(jnp.where(afft_ref[bi] >= cand, 1.0, 0.0), axis=1, keepdims=True)
        return lax.fori_loop(0, nb, body, jnp.zeros((N_EXPERTS, 1), F32))

    def shrink(step, bounds):
        lo, hi = bounds
        mid = jnp.where(step < GEO_STEPS, jnp.sqrt(lo * hi), lo + 0.5 * (hi - lo))
        mid = jnp.clip(mid, lo, hi)
        enough = count_ge(mid) >= cap
        return jnp.where(enough, mid, lo), jnp.where(enough, hi, mid)

    lo_col, hi_col = lax.fori_loop(0, BISECT_STEPS, shrink,
                                   (jnp.full((N_EXPERTS, 1), MIN_NORMAL, F32),
                                    jnp.full((N_EXPERTS, 1), 2.0, F32)))
    tiny = count_ge(jnp.full((N_EXPERTS, 1), MIN_NORMAL, F32)) < cap
    lo_col = jnp.where(tiny, -1.0, lo_col)
    hi_col = jnp.where(tiny, MIN_NORMAL, hi_col)
    need_col = cap - count_ge(hi_col)

    sub = lax.broadcasted_iota(I32, (N_EXPERTS, LANES), 0)
    lane = lax.broadcasted_iota(I32, (N_EXPERTS, LANES), 1)
    real = lane[0:1, :] < N_EXPERTS

    def to_lanes(col, pad):
        row = jnp.sum(jnp.where(sub == lane, col, 0.0), axis=0, keepdims=True)
        return jnp.where(real, row, pad)

    lo = to_lanes(lo_col, 3.0)
    hi = to_lanes(hi_col, 3.0)
    need = to_lanes(need_col, 0.0)

    def tile_body(t, carry):
        sel_before, eq_before = carry
        off = pl.multiple_of(t * tt, tt)
        v = aff_ref[pl.ds(off, tt), :]
        eq = (v >= lo) & (v < hi)
        eq_f = jnp.where(eq, 1.0, 0.0)
        tri = tri_ref[...]
        rank = jnp.dot(tri, eq_f.astype(BF16), preferred_element_type=F32) + eq_before
        sel = (v >= hi) | (eq & (rank < need))
        sel_f = jnp.where(sel, 1.0, 0.0)
        pos = jnp.dot(tri, sel_f.astype(BF16), preferred_element_type=F32) + sel_before
        posm = jnp.where(sel, pos, -1.0)
        starts_ref[t] = jnp.broadcast_to(sel_before.astype(I32), (8, LANES))
        pos_ref[t] = posm
        post_ref[t] = posm.T[0:N_EXPERTS, :]
        return (sel_before + jnp.sum(sel_f, axis=0, keepdims=True),
                eq_before + jnp.sum(eq_f, axis=0, keepdims=True))

    nt = pos_ref.shape[0]
    zero = jnp.zeros((1, LANES), F32)
    sel_total, _ = lax.fori_loop(0, nt, tile_body, (zero, zero), unroll=PLAN_UNROLL)
    starts_ref[nt] = jnp.broadcast_to(sel_total.astype(I32), (8, LANES))


def _plan(afft, aff2d, tri, cap):
    nb = afft.shape[0]
    n_tok = aff2d.shape[0]
    tt = TT_MOE
    nt = n_tok // tt
    vm = pl.BlockSpec(memory_space=pltpu.VMEM)
    return pl.pallas_call(
        functools.partial(_plan_kernel, nb=nb, cap=cap, tt=tt),
        out_shape=(jax.ShapeDtypeStruct((nt + 1, 8, LANES), I32),
                   jax.ShapeDtypeStruct((nt, tt, LANES), F32),
                   jax.ShapeDtypeStruct((nt, N_EXPERTS, tt), F32)),
        in_specs=[vm, vm, vm],
        out_specs=(vm, vm, vm),
        compiler_params=pltpu.CompilerParams(vmem_limit_bytes=VMEM_LIMIT),
        name="plan",
    )(afft, aff2d, tri)


def _disp_copy(stage, xe_ref, sems, e, row, dst_e=None):
    dst_e = e if dst_e is None else dst_e
    return pltpu.make_async_copy(stage.at[pl.ds(e * CH_DISP, CH_DISP)],
                                 xe_ref.at[dst_e, pl.ds(row, CH_DISP)], sems.at[e])


def _disp_kernel(starts_s, hx_ref, post_ref, xe_ref, stage, carry, sems, *, cap, nt, tt):
    t = pl.program_id(0)

    @pl.when(t == 0)
    def _():
        carry[...] = jnp.zeros(carry.shape, carry.dtype)
        stage[0:CH_DISP, :] = jnp.zeros((CH_DISP, X_WIDTH), BF16)
        for e in range(N_EXPERTS):
            _disp_copy(stage, xe_ref, sems, 0, cap, dst_e=e).start()
        for e in range(N_EXPERTS):
            _disp_copy(stage, xe_ref, sems, 0, cap, dst_e=e).wait()

    @pl.when(t > 0)
    def _():
        for e in range(N_EXPERTS):
            _disp_copy(stage, xe_ref, sems, e, 0).wait()

    pos_t = post_ref[0]
    slot_iota = lax.broadcasted_iota(I32, (CH_DISP, tt), 0).astype(F32)

    def window(e, base):
        onehot = jnp.where(pos_t[e:e + 1, :] - base.astype(F32) == slot_iota, 1.0, 0.0)
        return onehot.astype(BF16)

    def carry_rows(e, base):
        off = (starts_s[t + 1, e] & -BF16_ROWS) - base
        inside = jnp.minimum(off, CH_DISP - BF16_ROWS)
        blk = stage[pl.ds(pl.multiple_of(e * CH_DISP + inside, BF16_ROWS), BF16_ROWS), :]
        return jnp.where(off < CH_DISP, blk, jnp.zeros_like(blk))

    bases = [starts_s[t, e] & -BF16_ROWS for e in range(N_EXPERTS)]
    onehot = jnp.concatenate([window(e, bases[e]) for e in range(N_EXPERTS)], axis=0)
    stage[...] = jnp.dot(onehot, hx_ref[...], preferred_element_type=F32).astype(BF16)
    for e in range(N_EXPERTS):
        head = pl.ds(e * CH_DISP, BF16_ROWS)
        stage[head, :] = (stage[head, :].astype(F32) + carry[e].astype(F32)).astype(BF16)
        _disp_copy(stage, xe_ref, sems, e, pl.multiple_of(bases[e], BF16_ROWS)).start()
        carry[e] = carry_rows(e, bases[e])

    n_win = [(starts_s[t + 1, e] - bases[e] + (CH_DISP - 1)) // CH_DISP
             for e in range(N_EXPERTS)]
    most = functools.reduce(jnp.maximum, n_win)

    @pl.when(most > 1)
    def _():
        for e in range(N_EXPERTS):
            def more(c, _, e=e):
                cb = pl.multiple_of(bases[e] + c * CH_DISP, BF16_ROWS)
                extra = jnp.dot(window(e, cb), hx_ref[...], preferred_element_type=F32)
                _disp_copy(stage, xe_ref, sems, e, 0).wait()
                stage[pl.ds(e * CH_DISP, CH_DISP), :] = extra.astype(BF16)
                _disp_copy(stage, xe_ref, sems, e, cb).start()
                return 0

            lax.fori_loop(1, n_win[e], more, 0)

            @pl.when(n_win[e] > 1)
            def _(e=e):
                carry[e] = carry_rows(e, bases[e] + (n_win[e] - 1) * CH_DISP)

    @pl.when(t == nt - 1)
    def _():
        for e in range(N_EXPERTS):
            _disp_copy(stage, xe_ref, sems, e, 0).wait()


def _disp(starts_s, hx2d, post, cap):
    n_tok = hx2d.shape[0]
    tt = TT_MOE
    nt = n_tok // tt
    grid_spec = pltpu.PrefetchScalarGridSpec(
        num_scalar_prefetch=1,
        grid=(nt,),
        in_specs=[pl.BlockSpec((tt, X_WIDTH), lambda t, s: (t, 0)),
                  pl.BlockSpec((1, N_EXPERTS, tt), lambda t, s: (t, 0, 0))],
        out_specs=pl.BlockSpec(memory_space=pl.ANY),
        scratch_shapes=[pltpu.VMEM((N_EXPERTS * CH_DISP, X_WIDTH), BF16),
                        pltpu.VMEM((N_EXPERTS, BF16_ROWS, X_WIDTH), BF16),
                        pltpu.SemaphoreType.DMA((N_EXPERTS,))])
    return pl.pallas_call(
        functools.partial(_disp_kernel, cap=cap, nt=nt, tt=tt),
        out_shape=jax.ShapeDtypeStruct((N_EXPERTS, cap + CH_DISP, X_WIDTH), BF16),
        grid_spec=grid_spec,
        compiler_params=_cparams(("arbitrary",)),
        name="disp",
    )(starts_s, hx2d, post)


def _ffn_kernel(*refs, blocks):
    n_grp = len(blocks)
    xe_refs = refs[:n_grp]
    wg_ref, wu_ref, wd_ref, ye_ref, wg_bf, wu_bf, wd_bf = refs[n_grp:]
    e = pl.program_id(0)
    j = pl.program_id(1)

    @pl.when(j == 0)
    def _():
        wg_bf[...] = wg_ref[0].astype(BF16)
        wu_bf[...] = wu_ref[0].astype(BF16)
        wd_bf[...] = wd_ref[0].astype(BF16)

    def experts(xe_ref):
        blk = xe_ref[0]
        x = blk[:, 0:D_MODEL]
        gcols = blk[:, D_MODEL:].astype(F32)
        lane = lax.broadcasted_iota(I32, gcols.shape, 1)
        gate = jnp.sum(jnp.where((lane == e) | (lane == e + N_EXPERTS), gcols, 0.0),
                       axis=-1, keepdims=True)
        a = jnp.dot(x, wg_bf[...], preferred_element_type=F32)
        u = jnp.dot(x, wu_bf[...], preferred_element_type=F32)
        hmid = (a * jax.nn.sigmoid(a) * u).astype(BF16)
        y = jnp.dot(hmid, wd_bf[...], preferred_element_type=F32) * gate
        ye_ref[0] = y.astype(BF16)

    for g, (first, count) in enumerate(blocks):
        pl.when((j >= first) & (j < first + count))(functools.partial(experts, xe_refs[g]))

    @pl.when(j >= blocks[-1][0] + blocks[-1][1])
    def _():
        ye_ref[...] = jnp.zeros(ye_ref.shape, ye_ref.dtype)


def _ffn(xes, caps, wg, wu, wd):
    tc = TC_FFN
    blocks, first = [], 0
    for cap in caps:
        blocks.append((first, cap // tc))
        first += cap // tc
    wspec = pl.BlockSpec((1, D_MODEL, EXPERT_FF), lambda e, j: (e, 0, 0))

    def slot_spec(first, count):
        return pl.BlockSpec((1, tc, X_WIDTH),
                            lambda e, j: (e, jnp.clip(j - first, 0, count - 1), 0))

    return pl.pallas_call(
        functools.partial(_ffn_kernel, blocks=tuple(blocks)),
        out_shape=jax.ShapeDtypeStruct((N_EXPERTS, (first + 1) * tc, D_MODEL), BF16),
        grid=(N_EXPERTS, first + 1),
        in_specs=[slot_spec(f, c) for f, c in blocks] + [wspec, wspec,
                  pl.BlockSpec((1, EXPERT_FF, D_MODEL), lambda e, j: (e, 0, 0))],
        out_specs=pl.BlockSpec((1, tc, D_MODEL), lambda e, j: (e, j, 0)),
        scratch_shapes=[pltpu.VMEM((D_MODEL, EXPERT_FF), BF16), pltpu.VMEM((D_MODEL, EXPERT_FF), BF16),
                        pltpu.VMEM((EXPERT_FF, D_MODEL), BF16)],
        compiler_params=_cparams(("parallel", "arbitrary")),
        name="ffn",
    )(*xes, wg, wu, wd)


def _comb_copy(ye_ref, buf, sems, slot, e, row):
    return pltpu.make_async_copy(ye_ref.at[e, pl.ds(row, CR_COMB)],
                                 buf.at[slot, pl.ds(e * CR_COMB, CR_COMB)], sems.at[slot])


def _comb_kernel(starts_s, x1_ref, pos_ref, mod_ref, fg_ref, ye_ref, y_ref, buf, xbuf, acc_ref,
                 sems, xsem, *, row0, nt, tt):
    t = pl.program_id(0)
    slot = lax.rem(t, 2)

    def issue(tile, sl):
        for e in range(N_EXPERTS):
            b0 = pl.multiple_of(starts_s[tile, e] & -BF16_ROWS, BF16_ROWS)
            _comb_copy(ye_ref, buf, sems, sl, e, row0 + b0).start()

    @pl.when(t == 0)
    def _():
        issue(0, 0)

    @pl.when(t + 1 < nt)
    def _():
        issue(t + 1, 1 - slot)

    posm = pos_ref[0]
    bases = [starts_s[t, e] & -BF16_ROWS for e in range(N_EXPERTS)]
    lane = lax.broadcasted_iota(I32, (tt, LANES), 1)
    lane_f = lane.astype(F32)
    low = lane < CR_COMB
    ohs = []
    for e in range(0, N_EXPERTS, 2):
        rel0 = posm[:, e:e + 1] - bases[e].astype(F32)
        rel1 = posm[:, e + 1:e + 2] - (bases[e + 1] - CR_COMB).astype(F32)
        ohs.append(jnp.where(jnp.where(low, rel0, rel1) == lane_f, 1.0, 0.0).astype(BF16))
    onehot = jnp.concatenate(ohs, axis=-1)

    for e in range(N_EXPERTS):
        _comb_copy(ye_ref, buf, sems, slot, e, 0).wait()
    acc_ref[...] = jnp.dot(onehot, buf[slot], preferred_element_type=F32)

    n_win = [(starts_s[t + 1, e] - bases[e] + (CR_COMB - 1)) // CR_COMB
             for e in range(N_EXPERTS)]
    most = functools.reduce(jnp.maximum, n_win)

    @pl.when(most > 1)
    def _():
        win_iota = lax.broadcasted_iota(I32, (tt, CR_COMB), 1).astype(F32)
        for e in range(N_EXPERTS):
            def extra(c, _, e=e):
                cb = pl.multiple_of(bases[e] + c * CR_COMB, BF16_ROWS)
                cp = pltpu.make_async_copy(ye_ref.at[e, pl.ds(row0 + cb, CR_COMB)], xbuf,
                                           xsem.at[0])
                cp.start()
                cp.wait()
                rel = posm[:, e:e + 1] - cb.astype(F32)
                oh = jnp.where(rel == win_iota, 1.0, 0.0).astype(BF16)
                acc_ref[...] += jnp.dot(oh, xbuf[...], preferred_element_type=F32)
                return 0

            lax.fori_loop(1, n_win[e], extra, 0)

    g2 = mod_ref[0, 5:6, :]
    y_ref[...] = _rms(x1_ref[...] + g2 * acc_ref[...]) * fg_ref[...]


def _comb(starts_s, x1_2d, pos, mod, fg, ye, row0, seq):
    n_tok = x1_2d.shape[0]
    tt = TT_MOE
    nt = n_tok // tt
    tiles_per_seq = seq // tt
    assert 2 * CR_COMB == LANES
    grid_spec = pltpu.PrefetchScalarGridSpec(
        num_scalar_prefetch=1,
        grid=(nt,),
        in_specs=[pl.BlockSpec((tt, D_MODEL), lambda t, s: (t, 0)),
                  pl.BlockSpec((1, tt, LANES), lambda t, s: (t, 0, 0)),
                  pl.BlockSpec((1, N_MOD, D_MODEL), lambda t, s: (t // tiles_per_seq, 0, 0)),
                  pl.BlockSpec((1, D_MODEL), lambda t, s: (0, 0)),
                  pl.BlockSpec(memory_space=pl.ANY)],
        out_specs=pl.BlockSpec((tt, D_MODEL), lambda t, s: (t, 0)),
        scratch_shapes=[pltpu.VMEM((2, N_EXPERTS * CR_COMB, D_MODEL), BF16),
                        pltpu.VMEM((CR_COMB, D_MODEL), BF16),
                        pltpu.VMEM((tt, D_MODEL), F32),
                        pltpu.SemaphoreType.DMA((2,)),
                        pltpu.SemaphoreType.DMA((1,))])
    return pl.pallas_call(
        functools.partial(_comb_kernel, row0=row0, nt=nt, tt=tt),
        out_shape=jax.ShapeDtypeStruct((n_tok, D_MODEL), F32),
        grid_spec=grid_spec,
        compiler_params=_cparams(("arbitrary",)),
        name="comb",
    )(starts_s, x1_2d, pos, mod, fg, ye)


def _head_tile_cols(base, nope):
    r = base + nope
    return (list(range(base, base + nope)) + list(range(r, r + QK_ROPE))
            + list(range(r, r + HALF_ROPE)))


def _prep_weights(w_in, w_uq, w_ukv):
    zcol = lambda w, n: jnp.zeros((w.shape[0], n), w.dtype)
    kpe0 = Q_LORA + KV_LORA
    kpe_tile = jnp.concatenate(
        [zcol(w_in, QK_NOPE), w_in[:, kpe0:kpe0 + QK_ROPE], w_in[:, kpe0:kpe0 + HALF_ROPE],
         zcol(w_in, HEAD_PAD - QK_NOPE - QK_ROPE - HALF_ROPE)], axis=1)
    win = jnp.concatenate([w_in[:, :kpe0], kpe_tile, w_in[:, kpe0 + QK_ROPE:]], axis=1)

    q_tiles, k_tiles, v_cols = [], [], []
    for hd in range(N_HEADS):
        cols = np.asarray(_head_tile_cols(hd * (QK_NOPE + QK_ROPE), QK_NOPE))
        q_tiles += [w_uq[:, cols], zcol(w_uq, HEAD_PAD - len(cols))]
        kb = hd * (QK_NOPE + V_DIM)
        k_tiles += [w_ukv[:, kb:kb + QK_NOPE], zcol(w_ukv, HEAD_PAD - QK_NOPE)]
        w_v = w_ukv[:, kb + QK_NOPE:kb + QK_NOPE + V_DIM]
        pad = zcol(w_ukv, HEAD_PAD - V_DIM)
        v_cols += [w_v, pad] if hd % 2 == 0 else [pad, w_v]
    wq = jnp.concatenate(q_tiles, axis=1)
    wkv = jnp.concatenate(k_tiles + v_cols, axis=1)
    lane = np.arange(N_HEADS * HEAD_PAD)
    ones_here = ((lane % HEAD_PAD) >= V_DIM) == ((lane // HEAD_PAD) % 2 == 0)
    vone = jnp.asarray(ones_here.astype(np.float32))[None, :]
    return win.astype(BF16), wq.astype(BF16), wkv.astype(BF16), vone


def _rope_tabs(seq):
    inv = ROPE_THETA ** (-jnp.arange(0, QK_ROPE, 2, dtype=F32) / QK_ROPE)
    ang = jnp.arange(seq, dtype=F32)[:, None] * inv[None, :]
    cos, sin = jnp.cos(ang), jnp.sin(ang)
    one = jnp.ones((seq, QK_NOPE), F32)
    z_nope = jnp.zeros((seq, QK_NOPE), F32)
    z_pad = jnp.zeros((seq, HEAD_PAD - QK_NOPE - QK_ROPE), F32)
    scale = (QK_NOPE + QK_ROPE) ** -0.5 * float(np.log2(np.e))
    c_full = jnp.concatenate([one, cos, cos, z_pad], axis=1)
    s_full = jnp.concatenate([z_nope, -sin, sin, z_pad], axis=1)
    c_rope = jnp.concatenate([z_nope, cos, cos, z_pad], axis=1)
    return c_full * scale, s_full * scale, c_rope, s_full


def _tri(tt):
    r = np.arange(tt)
    return jnp.asarray((r[None, :] < r[:, None]).astype(np.float32), dtype=BF16)


def kernel(x_prompt, x_sample, c_prompt, c_sample, w_ada, b_ada, w_in, q_norm_g, kv_norm_g,
           w_uq, w_ukv, w_pool, pool_scale, w_out, w_router, w_gate, w_up, w_down, final_norm_g):
    groups = ((x_prompt, c_prompt), (x_sample, c_sample))
    n_c = sum(c.shape[0] for _, c in groups)
    c_all = jnp.concatenate([c for _, c in groups]
                            + [jnp.zeros((-n_c % 8, D_MODEL), F32)], axis=0)
    mod_all = _ada(c_all, w_ada[0], b_ada)

    win, wq, wkv, vone = _prep_weights(w_in[0], w_uq[0], w_ukv[0])
    qg = q_norm_g[0][None, :]
    kvg = kv_norm_g[0][None, :]
    wp = w_pool[0].astype(BF16)
    ps = pool_scale[0][None, :]
    woa = w_out[0][:D_ATT].astype(BF16)
    wob = w_out[0][D_ATT:].astype(BF16)
    wr_pad = jnp.concatenate([w_router[0], jnp.zeros((D_MODEL, LANES - N_EXPERTS), F32)], axis=1)
    wr_hi = wr_pad.astype(BF16)
    wr = jnp.concatenate([wr_hi, (wr_pad - wr_hi.astype(F32)).astype(BF16)], axis=1)
    wg, wu, wd = w_gate[0], w_up[0], w_down[0]
    fg = final_norm_g[None, :]
    tri = _tri(TT_MOE)

    caps = [CAPACITY_FACTOR * x.shape[0] * x.shape[1] // N_EXPERTS for x, _ in groups]
    assert all(c % TC_FFN == 0 for c in caps)
    xes = []

    staged = []
    c_off = 0
    row0 = 0
    for (x, c), cap in zip(groups, caps):
        b, s, _ = x.shape
        mod = mod_all[c_off:c_off + b].reshape(b, N_MOD, D_MODEL)
        c_off += b
        q, k, v, u = _pre(x, mod, win, qg, kvg, wq, wkv, vone, _rope_tabs(s))
        o = _attn(q, k, v)
        x1, hx, aff, afft = _mix(x, o, u, mod, wp, ps, woa, wob, wr)
        n_tok = b * s
        starts_v, pos, post = _plan(afft, aff.reshape(n_tok, LANES), tri, cap)
        starts_s = starts_v[:, 0, :N_EXPERTS]
        xes.append(_disp(starts_s, hx.reshape(n_tok, X_WIDTH), post, cap))
        staged.append((x1.reshape(n_tok, D_MODEL), pos, starts_s, mod, row0, (b, s)))
        row0 += cap

    ye = _ffn(xes, caps, wg, wu, wd)

    outs = []
    for x1, pos, starts_s, mod, r0, (b, s) in staged:
        y = _comb(starts_s, x1, pos, mod, fg, ye, r0, s)
        outs.append(y.reshape(b, s, D_MODEL))
    return tuple(outs)
```

```python
import functools

import jax
import jax.numpy as jnp
import numpy as np
from jax import lax
from jax.experimental import pallas as pl
from jax.experimental.pallas import tpu as pltpu

F32 = jnp.float32
BF16 = jnp.bfloat16
I32 = jnp.int32

D_MODEL = 1024
N_HEADS = 8
QK_NOPE = 64
QK_ROPE = 32
V_DIM = 64
Q_LORA = 384
KV_LORA = 256
ROPE_THETA = 10000.0
D_ATT = N_HEADS * V_DIM
D_POOL = 512
POOL_WINDOWS = (2, 4, 8, 16)
POOL_GROUP = 128
N_EXPERTS = 16
CAPACITY_FACTOR = 2
EXPERT_FF = 1024
N_MOD = 6
EPS = 1e-6

LANES = 128
HEAD_PAD = LANES
HALF_ROPE = QK_ROPE // 2
D_IN_PAD = Q_LORA + KV_LORA + LANES + D_POOL
X_WIDTH = D_MODEL + LANES
POOL_HALO = 8
BF16_ROWS = 16

TS_PRE = 512
TQ_ATT = 512
ROWS_ATT = 512
TK_ATT = 4096
UNROLL_ATT = 2
TT_MOE = 256
CH_DISP = 64
CR_COMB = 64
TC_FFN = 512
PLAN_UNROLL = 4
MIN_NORMAL = float(np.finfo(np.float32).tiny)
GEO_STEPS = 32
BISECT_STEPS = GEO_STEPS + 6
VMEM_LIMIT = 56 * 1024 * 1024


def _cparams(sem):
    return pltpu.CompilerParams(dimension_semantics=sem, vmem_limit_bytes=VMEM_LIMIT)


def _rms(x):
    return x * lax.rsqrt(jnp.mean(x * x, axis=-1, keepdims=True) + EPS)


def _ada_kernel(c_ref, w_ref, b_ref, o_ref):
    c = c_ref[...]
    s = c * jax.nn.sigmoid(c)
    o_ref[...] = jnp.dot(s, w_ref[...], preferred_element_type=F32,
                         precision=lax.Precision.HIGHEST) + b_ref[...]


def _ada(c_pad, w_ada, b_ada):
    rows = c_pad.shape[0]
    return pl.pallas_call(
        _ada_kernel,
        out_shape=jax.ShapeDtypeStruct((rows, N_MOD * D_MODEL), F32),
        grid=(N_MOD,),
        in_specs=[pl.BlockSpec((rows, D_MODEL), lambda j: (0, 0)),
                  pl.BlockSpec((D_MODEL, D_MODEL), lambda j: (0, j)),
                  pl.BlockSpec((1, D_MODEL), lambda j: (0, j))],
        out_specs=pl.BlockSpec((rows, D_MODEL), lambda j: (0, j)),
        compiler_params=_cparams(("arbitrary",)),
        name="ada",
    )(c_pad, w_ada, b_ada)


def _pre_kernel(x_ref, mod_ref, win_ref, qg_ref, kvg_ref, wq_ref, wkv_ref,
                cq_ref, sq_ref, ck_ref, sk_ref, vone_ref, q_ref, k_ref, v_ref, u_ref):
    x = x_ref[0]
    sh1 = mod_ref[0, 0:1, :]
    sc1 = mod_ref[0, 1:2, :]
    h = (_rms(x) * (1.0 + sc1) + sh1).astype(BF16)
    z = jnp.dot(h, win_ref[...], preferred_element_type=F32)
    c_q = z[:, :Q_LORA]
    c_kv = z[:, Q_LORA:Q_LORA + KV_LORA]
    kpe = z[:, Q_LORA + KV_LORA:Q_LORA + KV_LORA + LANES]
    u_ref[0] = z[:, Q_LORA + KV_LORA + LANES:]

    q = jnp.dot((_rms(c_q) * qg_ref[...]).astype(BF16), wq_ref[...],
                preferred_element_type=F32)
    cq_t = cq_ref[...]
    sq_t = sq_ref[...]
    for hd in range(N_HEADS):
        blk = q[:, hd * HEAD_PAD:(hd + 1) * HEAD_PAD]
        rot = pltpu.roll(blk, HEAD_PAD - HALF_ROPE, 1)
        q_ref[0, :, hd * HEAD_PAD:(hd + 1) * HEAD_PAD] = (blk * cq_t + rot * sq_t).astype(BF16)

    kv = jnp.dot((_rms(c_kv) * kvg_ref[...]).astype(BF16), wkv_ref[...],
                 preferred_element_type=F32)
    kpe_r = kpe * ck_ref[...] + pltpu.roll(kpe, HEAD_PAD - HALF_ROPE, 1) * sk_ref[...]
    k_full = jnp.concatenate([kv[:, hd * HEAD_PAD:(hd + 1) * HEAD_PAD] + kpe_r
                              for hd in range(N_HEADS)], axis=-1)
    k_ref[0, 0] = k_full.T.astype(BF16)
    v_ref[0] = (kv[:, N_HEADS * HEAD_PAD:] + vone_ref[...]).astype(BF16)


def _pre(x, mod, win, qg, kvg, wq, wkv, vone, tabs):
    b, s, _ = x.shape
    ts = TS_PRE
    tk = min(TK_ATT, s)
    per_chunk = tk // ts
    assert tk % ts == 0 and s % tk == 0
    const = lambda shape: pl.BlockSpec(shape, lambda bi, i: (0,) * len(shape))
    tab = pl.BlockSpec((ts, LANES), lambda bi, i: (i, 0))
    tok = lambda w: pl.BlockSpec((1, ts, w), lambda bi, i: (bi, i, 0))
    return pl.pallas_call(
        _pre_kernel,
        out_shape=(jax.ShapeDtypeStruct((b, s, N_HEADS * HEAD_PAD), BF16),
                   jax.ShapeDtypeStruct((b, s // tk, N_HEADS * HEAD_PAD, tk), BF16),
                   jax.ShapeDtypeStruct((b, s, N_HEADS * HEAD_PAD), BF16),
                   jax.ShapeDtypeStruct((b, s, D_POOL), F32)),
        grid=(b, s // ts),
        in_specs=[tok(D_MODEL),
                  pl.BlockSpec((1, N_MOD, D_MODEL), lambda bi, i: (bi, 0, 0)),
                  const((D_MODEL, D_IN_PAD)), const((1, Q_LORA)), const((1, KV_LORA)),
                  const((Q_LORA, N_HEADS * HEAD_PAD)),
                  const((KV_LORA, 2 * N_HEADS * HEAD_PAD)),
                  tab, tab, tab, tab, const((1, N_HEADS * HEAD_PAD))],
        out_specs=(tok(N_HEADS * HEAD_PAD),
                   pl.BlockSpec((1, 1, N_HEADS * HEAD_PAD, ts),
                                lambda bi, i: (bi, i // per_chunk, 0, i % per_chunk)),
                   tok(N_HEADS * HEAD_PAD), tok(D_POOL)),
        compiler_params=_cparams(("parallel", "arbitrary")),
        name="pre",
    )(x, mod, win, qg, kvg, wq, wkv, *tabs, vone)


def _attn_kernel(q_ref, k_ref, v_ref, o_ref, *, seq, tq, tk, rows, unroll):
    chains = [(r, hh) for r in range(tq // rows) for hh in range(2)]

    def body(c, carry):
        off = pl.multiple_of(c * tk, tk)
        new = []
        for (r, hh), (m, acc) in zip(chains, carry):
            cols = slice(hh * HEAD_PAD, (hh + 1) * HEAD_PAD)
            qh = q_ref[0, r * rows:(r + 1) * rows, cols]
            s = jnp.dot(qh, k_ref[0, c, cols, :], preferred_element_type=F32)
            m_new = jnp.maximum(m, jnp.max(s, axis=-1, keepdims=True))
            p = jnp.exp2((s - m_new).astype(BF16))
            acc = jnp.exp2(m - m_new) * acc + jnp.dot(p, v_ref[0, pl.ds(off, tk), cols],
                                                      preferred_element_type=F32)
            new.append((m_new, acc))
        return tuple(new)

    init = tuple((jnp.full((rows, 1), -jnp.inf, F32), jnp.zeros((rows, HEAD_PAD), F32))
                 for _ in chains)
    final = lax.fori_loop(0, seq // tk, body, init, unroll=unroll)
    lane = lax.broadcasted_iota(I32, (rows, HEAD_PAD), 1)
    for r in range(tq // rows):
        a0, a1 = final[2 * r][1], final[2 * r + 1][1]
        o_ref[0, r * rows:(r + 1) * rows, :] = jnp.where(
            lane < V_DIM, a0 / a0[:, V_DIM:V_DIM + 1], a1 / a1[:, 0:1]).astype(BF16)


def _attn(q, k, v):
    b, s, _ = q.shape
    tq, tk = TQ_ATT, min(TK_ATT, s)
    assert s % tq == 0 and s % tk == 0
    return pl.pallas_call(
        functools.partial(_attn_kernel, seq=s, tq=tq, tk=tk, rows=ROWS_ATT, unroll=UNROLL_ATT),
        out_shape=jax.ShapeDtypeStruct((b, s, D_ATT), BF16),
        grid=(b, N_HEADS // 2, s // tq),
        in_specs=[pl.BlockSpec((1, tq, 2 * HEAD_PAD), lambda bi, p, i: (bi, i, p)),
                  pl.BlockSpec((1, s // tk, 2 * HEAD_PAD, tk), lambda bi, p, i: (bi, 0, p, 0)),
                  pl.BlockSpec((1, s, 2 * HEAD_PAD), lambda bi, p, i: (bi, 0, p))],
        out_specs=pl.BlockSpec((1, tq, 2 * V_DIM), lambda bi, p, i: (bi, i, p)),
        compiler_params=_cparams(("parallel", "parallel", "arbitrary")),
        name="attn",
    )(q, k, v)


def _mix_kernel(x_ref, o_ref, u_ref, up_ref, un_ref, mod_ref, wp_ref, ps_ref, woa_ref, wob_ref,
                wr_ref, icnt_ref, x1_ref, hx_ref, aff_ref, afft_ref, ue_ref, *, ts):
    i = pl.program_id(1)
    nt = pl.num_programs(1)
    hb = POOL_HALO
    ue_ref[0:hb, :] = jnp.where(i > 0, up_ref[0], 0.0)
    ue_ref[hb:hb + ts, :] = u_ref[0]
    ue_ref[hb + ts:2 * hb + ts, :] = jnp.where(i < nt - 1, un_ref[0], 0.0)

    ys = []
    for g, w in enumerate(POOL_WINDOWS):
        cols = slice(g * POOL_GROUP, (g + 1) * POOL_GROUP)
        acc = None
        for off in range(-(w // 2), w - w // 2):
            sl = ue_ref[hb + off:hb + off + ts, cols]
            acc = sl if acc is None else acc + sl
        d = (acc * icnt_ref[:, g:g + 1] - ue_ref[hb:hb + ts, cols]).astype(BF16)
        y = jnp.dot(d, wp_ref[g], preferred_element_type=F32) * ps_ref[:, cols]
        ys.append(y.astype(BF16))
    pool = jnp.concatenate(ys, axis=-1)
    mixed = (jnp.dot(o_ref[0], woa_ref[...], preferred_element_type=F32)
             + jnp.dot(pool, wob_ref[...], preferred_element_type=F32))
    g1 = mod_ref[0, 2:3, :]
    sh2 = mod_ref[0, 3:4, :]
    sc2 = mod_ref[0, 4:5, :]
    x1 = x_ref[0] + g1 * mixed
    x1_ref[0] = x1
    h2 = _rms(x1) * (1.0 + sc2) + sh2
    hx_ref[0, :, 0:D_MODEL] = h2.astype(BF16)

    h_hi = h2.astype(BF16)
    h_lo = (h2 - h_hi.astype(F32)).astype(BF16)
    both = jnp.dot(h_hi, wr_ref[...], preferred_element_type=F32)
    logits = (both[:, :LANES] + both[:, LANES:]
              + jnp.dot(h_lo, wr_ref[:, :LANES], preferred_element_type=F32))
    lane = lax.broadcasted_iota(I32, (ts, LANES), 1)
    logits = jnp.where(lane < N_EXPERTS, logits, -jnp.inf)
    ex = jnp.exp(logits - jnp.max(logits, axis=-1, keepdims=True))
    aff = ex / jnp.sum(ex, axis=-1, keepdims=True)
    aff_ref[0] = aff
    afft_ref[0] = aff.T[0:N_EXPERTS, :]
    hi_part = aff.astype(BF16).astype(F32)
    lo_part = pltpu.roll(aff - hi_part, N_EXPERTS, 1)
    hx_ref[0, :, D_MODEL:] = jnp.where(lane < N_EXPERTS, hi_part, lo_part).astype(BF16)


def _pool_inv_counts(seq):
    t = np.arange(seq)[:, None]
    w = np.asarray(POOL_WINDOWS)[None, :]
    cnt = np.minimum(t + (w - w // 2), seq) - np.maximum(t - w // 2, 0)
    tab = np.zeros((seq, LANES), np.float64)
    tab[:, :len(POOL_WINDOWS)] = 1.0 / cnt
    return jnp.asarray(tab, dtype=F32)


def _mix(x, o, u, mod, wp, ps, woa, wob, wr):
    b, s, _ = x.shape
    ts = TS_PRE
    hpt = ts // POOL_HALO
    nh = s // POOL_HALO
    const = lambda shape: pl.BlockSpec(shape, lambda bi, i: (0,) * len(shape))
    tok = lambda w: pl.BlockSpec((1, ts, w), lambda bi, i: (bi, i, 0))
    return pl.pallas_call(
        functools.partial(_mix_kernel, ts=ts),
        out_shape=(jax.ShapeDtypeStruct((b, s, D_MODEL), F32),
                   jax.ShapeDtypeStruct((b, s, X_WIDTH), BF16),
                   jax.ShapeDtypeStruct((b, s, LANES), F32),
                   jax.ShapeDtypeStruct((b, N_EXPERTS, s), F32)),
        grid=(b, s // ts),
        in_specs=[tok(D_MODEL), tok(D_ATT), tok(D_POOL),
                  pl.BlockSpec((1, POOL_HALO, D_POOL),
                               lambda bi, i: (bi, jnp.maximum(i * hpt - 1, 0), 0)),
                  pl.BlockSpec((1, POOL_HALO, D_POOL),
                               lambda bi, i: (bi, jnp.minimum((i + 1) * hpt, nh - 1), 0)),
                  pl.BlockSpec((1, N_MOD, D_MODEL), lambda bi, i: (bi, 0, 0)),
                  const((len(POOL_WINDOWS), POOL_GROUP, POOL_GROUP)), const((1, D_POOL)),
                  const((D_ATT, D_MODEL)), const((D_POOL, D_MODEL)), const((D_MODEL, 2 * LANES)),
                  pl.BlockSpec((ts, LANES), lambda bi, i: (i, 0))],
        out_specs=(tok(D_MODEL), tok(X_WIDTH), tok(LANES),
                   pl.BlockSpec((1, N_EXPERTS, ts), lambda bi, i: (bi, 0, i))),
        scratch_shapes=[pltpu.VMEM((ts + 2 * POOL_HALO, D_POOL), F32)],
        compiler_params=_cparams(("parallel", "arbitrary")),
        name="mix",
    )(x, o, u, u, u, mod, wp, ps, woa, wob, wr, _pool_inv_counts(s))


def _plan_kernel(afft_ref, aff_ref, tri_ref, starts_ref, pos_ref, post_ref, *, nb, cap, tt):
    def count_ge(cand):
        def body(bi, acc):
            return acc + jnp.sum(jnp.where(afft_ref[bi] >= cand, 1.0, 0.0), axis=1, keepdims=True)
        return lax.fori_loop(0, nb, body, jnp.zeros((N_EXPERTS, 1), F32))

    def shrink(step, bounds):
        lo, hi = bounds
        mid = jnp.where(step < GEO_STEPS, jnp.sqrt(lo * hi), lo + 0.5 * (hi - lo))
        mid = jnp.clip(mid, lo, hi)
        enough = count_ge(mid) >= cap
        return jnp.where(enough, mid, lo), jnp.where(enough, hi, mid)

    lo_col, hi_col = lax.fori_loop(0, BISECT_STEPS, shrink,
                                   (jnp.full((N_EXPERTS, 1), MIN_NORMAL, F32),
                                    jnp.full((N_EXPERTS, 1), 2.0, F32)))
    tiny = count_ge(jnp.full((N_EXPERTS, 1), MIN_NORMAL, F32)) < cap
    lo_col = jnp.where(tiny, -1.0, lo_col)
    hi_col = jnp.where(tiny, MIN_NORMAL, hi_col)
    need_col = cap - count_ge(hi_col)

    sub = lax.broadcasted_iota(I32, (N_EXPERTS, LANES), 0)
    lane = lax.broadcasted_iota(I32, (N_EXPERTS, LANES), 1)
    real = lane[0:1, :] < N_EXPERTS

    def to_lanes(col, pad):
        row = jnp.sum(jnp.where(sub == lane, col, 0.0), axis=0, keepdims=True)
        return jnp.where(real, row, pad)

    lo = to_lanes(lo_col, 3.0)
    hi = to_lanes(hi_col, 3.0)
    need = to_lanes(need_col, 0.0)

    def tile_body(t, carry):
        sel_before, eq_before = carry
        off = pl.multiple_of(t * tt, tt)
        v = aff_ref[pl.ds(off, tt), :]
        eq = (v >= lo) & (v < hi)
        eq_f = jnp.where(eq, 1.0, 0.0)
        tri = tri_ref[...]
        rank = jnp.dot(tri, eq_f.astype(BF16), preferred_element_type=F32) + eq_before
        sel = (v >= hi) | (eq & (rank < need))
        sel_f = jnp.where(sel, 1.0, 0.0)
        pos = jnp.dot(tri, sel_f.astype(BF16), preferred_element_type=F32) + sel_before
        posm = jnp.where(sel, pos, -1.0)
        starts_ref[t] = jnp.broadcast_to(sel_before.astype(I32), (8, LANES))
        pos_ref[t] = posm
        post_ref[t] = posm.T[0:N_EXPERTS, :]
        return (sel_before + jnp.sum(sel_f, axis=0, keepdims=True),
                eq_before + jnp.sum(eq_f, axis=0, keepdims=True))

    nt = pos_ref.shape[0]
    zero = jnp.zeros((1, LANES), F32)
    sel_total, _ = lax.fori_loop(0, nt, tile_body, (zero, zero), unroll=PLAN_UNROLL)
    starts_ref[nt] = jnp.broadcast_to(sel_total.astype(I32), (8, LANES))


def _plan(afft, aff2d, tri, cap):
    nb = afft.shape[0]
    n_tok = aff2d.shape[0]
    tt = TT_MOE
    nt = n_tok // tt
    vm = pl.BlockSpec(memory_space=pltpu.VMEM)
    return pl.pallas_call(
        functools.partial(_plan_kernel, nb=nb, cap=cap, tt=tt),
        out_shape=(jax.ShapeDtypeStruct((nt + 1, 8, LANES), I32),
                   jax.ShapeDtypeStruct((nt, tt, LANES), F32),
                   jax.ShapeDtypeStruct((nt, N_EXPERTS, tt), F32)),
        in_specs=[vm, vm, vm],
        out_specs=(vm, vm, vm),
        compiler_params=pltpu.CompilerParams(vmem_limit_bytes=VMEM_LIMIT),
        name="plan",
    )(afft, aff2d, tri)


def _disp_copy(stage, xe_ref, sems, e, row, dst_e=None):
    dst_e = e if dst_e is None else dst_e
    return pltpu.make_async_copy(stage.at[pl.ds(e * CH_DISP, CH_DISP)],
                                 xe_ref.at[dst_e, pl.ds(row, CH_DISP)], sems.at[e])


def _disp_kernel(starts_s, hx_ref, post_ref, xe_ref, stage, carry, sems, *, cap, nt, tt):
    t = pl.program_id(0)

    @pl.when(t == 0)
    def _():
        carry[...] = jnp.zeros(carry.shape, carry.dtype)
        stage[0:CH_DISP, :] = jnp.zeros((CH_DISP, X_WIDTH), BF16)
        for e in range(N_EXPERTS):
            _disp_copy(stage, xe_ref, sems, 0, cap, dst_e=e).start()
        for e in range(N_EXPERTS):
            _disp_copy(stage, xe_ref, sems, 0, cap, dst_e=e).wait()

    @pl.when(t > 0)
    def _():
        for e in range(N_EXPERTS):
            _disp_copy(stage, xe_ref, sems, e, 0).wait()

    pos_t = post_ref[0]
    slot_iota = lax.broadcasted_iota(I32, (CH_DISP, tt), 0).astype(F32)

    def window(e, base):
        onehot = jnp.where(pos_t[e:e + 1, :] - base.astype(F32) == slot_iota, 1.0, 0.0)
        return onehot.astype(BF16)

    def carry_rows(e, base):
        off = (starts_s[t + 1, e] & -BF16_ROWS) - base
        inside = jnp.minimum(off, CH_DISP - BF16_ROWS)
        blk = stage[pl.ds(pl.multiple_of(e * CH_DISP + inside, BF16_ROWS), BF16_ROWS), :]
        return jnp.where(off < CH_DISP, blk, jnp.zeros_like(blk))

    bases = [starts_s[t, e] & -BF16_ROWS for e in range(N_EXPERTS)]
    onehot = jnp.concatenate([window(e, bases[e]) for e in range(N_EXPERTS)], axis=0)
    stage[...] = jnp.dot(onehot, hx_ref[...], preferred_element_type=F32).astype(BF16)
    for e in range(N_EXPERTS):
        head = pl.ds(e * CH_DISP, BF16_ROWS)
        stage[head, :] = (stage[head, :].astype(F32) + carry[e].astype(F32)).astype(BF16)
        _disp_copy(stage, xe_ref, sems, e, pl.multiple_of(bases[e], BF16_ROWS)).start()
        carry[e] = carry_rows(e, bases[e])

    n_win = [(starts_s[t + 1, e] - bases[e] + (CH_DISP - 1)) // CH_DISP
             for e in range(N_EXPERTS)]
    most = functools.reduce(jnp.maximum, n_win)

    @pl.when(most > 1)
    def _():
        for e in range(N_EXPERTS):
            def more(c, _, e=e):
                cb = pl.multiple_of(bases[e] + c * CH_DISP, BF16_ROWS)
                extra = jnp.dot(window(e, cb), hx_ref[...], preferred_element_type=F32)
                _disp_copy(stage, xe_ref, sems, e, 0).wait()
                stage[pl.ds(e * CH_DISP, CH_DISP), :] = extra.astype(BF16)
                _disp_copy(stage, xe_ref, sems, e, cb).start()
                return 0

            lax.fori_loop(1, n_win[e], more, 0)

            @pl.when(n_win[e] > 1)
            def _(e=e):
                carry[e] = carry_rows(e, bases[e] + (n_win[e] - 1) * CH_DISP)

    @pl.when(t == nt - 1)
    def _():
        for e in range(N_EXPERTS):
            _disp_copy(stage, xe_ref, sems, e, 0).wait()


def _disp(starts_s, hx2d, post, cap):
    n_tok = hx2d.shape[0]
    tt = TT_MOE
    nt = n_tok // tt
    grid_spec = pltpu.PrefetchScalarGridSpec(
        num_scalar_prefetch=1,
        grid=(nt,),
        in_specs=[pl.BlockSpec((tt, X_WIDTH), lambda t, s: (t, 0)),
                  pl.BlockSpec((1, N_EXPERTS, tt), lambda t, s: (t, 0, 0))],
        out_specs=pl.BlockSpec(memory_space=pl.ANY),
        scratch_shapes=[pltpu.VMEM((N_EXPERTS * CH_DISP, X_WIDTH), BF16),
                        pltpu.VMEM((N_EXPERTS, BF16_ROWS, X_WIDTH), BF16),
                        pltpu.SemaphoreType.DMA((N_EXPERTS,))])
    return pl.pallas_call(
        functools.partial(_disp_kernel, cap=cap, nt=nt, tt=tt),
        out_shape=jax.ShapeDtypeStruct((N_EXPERTS, cap + CH_DISP, X_WIDTH), BF16),
        grid_spec=grid_spec,
        compiler_params=_cparams(("arbitrary",)),
        name="disp",
    )(starts_s, hx2d, post)


def _ffn_kernel(*refs, blocks):
    n_grp = len(blocks)
    xe_refs = refs[:n_grp]
    wg_ref, wu_ref, wd_ref, ye_ref, wg_bf, wu_bf, wd_bf = refs[n_grp:]
    e = pl.program_id(0)
    j = pl.program_id(1)

    @pl.when(j == 0)
    def _():
        wg_bf[...] = wg_ref[0].astype(BF16)
        wu_bf[...] = wu_ref[0].astype(BF16)
        wd_bf[...] = wd_ref[0].astype(BF16)
        ye_ref[...] = jnp.zeros(ye_ref.shape, ye_ref.dtype)

    def experts(xe_ref):
        blk = xe_ref[0]
        x = blk[:, 0:D_MODEL]
        gcols = blk[:, D_MODEL:].astype(F32)
        lane = lax.broadcasted_iota(I32, gcols.shape, 1)
        gate = jnp.sum(jnp.where((lane == e) | (lane == e + N_EXPERTS), gcols, 0.0),
                       axis=-1, keepdims=True)
        a = jnp.dot(x, wg_bf[...], preferred_element_type=F32)
        u = jnp.dot(x, wu_bf[...], preferred_element_type=F32)
        hmid = (a * jax.nn.sigmoid(a) * u).astype(BF16)
        y = jnp.dot(hmid, wd_bf[...], preferred_element_type=F32) * gate
        ye_ref[0] = y.astype(BF16)

    for g, (first, count) in enumerate(blocks):
        pl.when((j > first) & (j <= first + count))(functools.partial(experts, xe_refs[g]))


def _ffn(xes, caps, wg, wu, wd):
    tc = TC_FFN
    blocks, first = [], 0
    for cap in caps:
        blocks.append((first, cap // tc))
        first += cap // tc
    wspec = pl.BlockSpec((1, D_MODEL, EXPERT_FF), lambda e, j: (e, 0, 0))

    def slot_spec(first, count):
        return pl.BlockSpec((1, tc, X_WIDTH),
                            lambda e, j: (e, jnp.clip(j - 1 - first, 0, count - 1), 0))

    return pl.pallas_call(
        functools.partial(_ffn_kernel, blocks=tuple(blocks)),
        out_shape=jax.ShapeDtypeStruct((N_EXPERTS, (first + 1) * tc, D_MODEL), BF16),
        grid=(N_EXPERTS, first + 1),
        in_specs=[slot_spec(f, c) for f, c in blocks] + [wspec, wspec,
                  pl.BlockSpec((1, EXPERT_FF, D_MODEL), lambda e, j: (e, 0, 0))],
        out_specs=pl.BlockSpec((1, tc, D_MODEL),
                               lambda e, j: (e, jnp.where(j == 0, first, j - 1), 0)),
        scratch_shapes=[pltpu.VMEM((D_MODEL, EXPERT_FF), BF16), pltpu.VMEM((D_MODEL, EXPERT_FF), BF16),
                        pltpu.VMEM((EXPERT_FF, D_MODEL), BF16)],
        compiler_params=_cparams(("parallel", "arbitrary")),
        name="ffn",
    )(*xes, wg, wu, wd)


def _comb_copy(ye_ref, buf, sems, slot, e, row):
    return pltpu.make_async_copy(ye_ref.at[e, pl.ds(row, CR_COMB)],
                                 buf.at[slot, pl.ds(e * CR_COMB, CR_COMB)], sems.at[slot])


def _comb_kernel(starts_s, x1_ref, pos_ref, mod_ref, fg_ref, ye_ref, y_ref, buf, xbuf, acc_ref,
                 sems, xsem, *, row0, nt, tt):
    t = pl.program_id(0)
    slot = lax.rem(t, 2)

    def issue(tile, sl):
        for e in range(N_EXPERTS):
            b0 = pl.multiple_of(starts_s[tile, e] & -BF16_ROWS, BF16_ROWS)
            _comb_copy(ye_ref, buf, sems, sl, e, row0 + b0).start()

    @pl.when(t == 0)
    def _():
        issue(0, 0)

    @pl.when(t + 1 < nt)
    def _():
        issue(t + 1, 1 - slot)

    posm = pos_ref[0]
    bases = [starts_s[t, e] & -BF16_ROWS for e in range(N_EXPERTS)]
    lane = lax.broadcasted_iota(I32, (tt, LANES), 1)
    lane_f = lane.astype(F32)
    low = lane < CR_COMB
    ohs = []
    for e in range(0, N_EXPERTS, 2):
        rel0 = posm[:, e:e + 1] - bases[e].astype(F32)
        rel1 = posm[:, e + 1:e + 2] - (bases[e + 1] - CR_COMB).astype(F32)
        ohs.append(jnp.where(jnp.where(low, rel0, rel1) == lane_f, 1.0, 0.0).astype(BF16))
    onehot = jnp.concatenate(ohs, axis=-1)

    for e in range(N_EXPERTS):
        _comb_copy(ye_ref, buf, sems, slot, e, 0).wait()
    acc_ref[...] = jnp.dot(onehot, buf[slot], preferred_element_type=F32)

    n_win = [(starts_s[t + 1, e] - bases[e] + (CR_COMB - 1)) // CR_COMB
             for e in range(N_EXPERTS)]
    most = functools.reduce(jnp.maximum, n_win)

    @pl.when(most > 1)
    def _():
        win_iota = lax.broadcasted_iota(I32, (tt, CR_COMB), 1).astype(F32)
        for e in range(N_EXPERTS):
            def extra(c, _, e=e):
                cb = pl.multiple_of(bases[e] + c * CR_COMB, BF16_ROWS)
                cp = pltpu.make_async_copy(ye_ref.at[e, pl.ds(row0 + cb, CR_COMB)], xbuf,
                                           xsem.at[0])
                cp.start()
                cp.wait()
                rel = posm[:, e:e + 1] - cb.astype(F32)
                oh = jnp.where(rel == win_iota, 1.0, 0.0).astype(BF16)
                acc_ref[...] += jnp.dot(oh, xbuf[...], preferred_element_type=F32)
                return 0

            lax.fori_loop(1, n_win[e], extra, 0)

    g2 = mod_ref[0, 5:6, :]
    y_ref[...] = _rms(x1_ref[...] + g2 * acc_ref[...]) * fg_ref[...]


def _comb(starts_s, x1_2d, pos, mod, fg, ye, row0, seq):
    n_tok = x1_2d.shape[0]
    tt = TT_MOE
    nt = n_tok // tt
    tiles_per_seq = seq // tt
    assert 2 * CR_COMB == LANES
    grid_spec = pltpu.PrefetchScalarGridSpec(
        num_scalar_prefetch=1,
        grid=(nt,),
        in_specs=[pl.BlockSpec((tt, D_MODEL), lambda t, s: (t, 0)),
                  pl.BlockSpec((1, tt, LANES), lambda t, s: (t, 0, 0)),
                  pl.BlockSpec((1, N_MOD, D_MODEL), lambda t, s: (t // tiles_per_seq, 0, 0)),
                  pl.BlockSpec((1, D_MODEL), lambda t, s: (0, 0)),
                  pl.BlockSpec(memory_space=pl.ANY)],
        out_specs=pl.BlockSpec((tt, D_MODEL), lambda t, s: (t, 0)),
        scratch_shapes=[pltpu.VMEM((2, N_EXPERTS * CR_COMB, D_MODEL), BF16),
                        pltpu.VMEM((CR_COMB, D_MODEL), BF16),
                        pltpu.VMEM((tt, D_MODEL), F32),
                        pltpu.SemaphoreType.DMA((2,)),
                        pltpu.SemaphoreType.DMA((1,))])
    return pl.pallas_call(
        functools.partial(_comb_kernel, row0=row0, nt=nt, tt=tt),
        out_shape=jax.ShapeDtypeStruct((n_tok, D_MODEL), F32),
        grid_spec=grid_spec,
        compiler_params=_cparams(("arbitrary",)),
        name="comb",
    )(starts_s, x1_2d, pos, mod, fg, ye)


def _head_tile_cols(base, nope):
    r = base + nope
    return (list(range(base, base + nope)) + list(range(r, r + QK_ROPE))
            + list(range(r, r + HALF_ROPE)))


def _prep_weights(w_in, w_uq, w_ukv):
    zcol = lambda w, n: jnp.zeros((w.shape[0], n), w.dtype)
    kpe0 = Q_LORA + KV_LORA
    kpe_tile = jnp.concatenate(
        [zcol(w_in, QK_NOPE), w_in[:, kpe0:kpe0 + QK_ROPE], w_in[:, kpe0:kpe0 + HALF_ROPE],
         zcol(w_in, HEAD_PAD - QK_NOPE - QK_ROPE - HALF_ROPE)], axis=1)
    win = jnp.concatenate([w_in[:, :kpe0], kpe_tile, w_in[:, kpe0 + QK_ROPE:]], axis=1)

    q_tiles, k_tiles, v_cols = [], [], []
    for hd in range(N_HEADS):
        cols = np.asarray(_head_tile_cols(hd * (QK_NOPE + QK_ROPE), QK_NOPE))
        q_tiles += [w_uq[:, cols], zcol(w_uq, HEAD_PAD - len(cols))]
        kb = hd * (QK_NOPE + V_DIM)
        k_tiles += [w_ukv[:, kb:kb + QK_NOPE], zcol(w_ukv, HEAD_PAD - QK_NOPE)]
        w_v = w_ukv[:, kb + QK_NOPE:kb + QK_NOPE + V_DIM]
        pad = zcol(w_ukv, HEAD_PAD - V_DIM)
        v_cols += [w_v, pad] if hd % 2 == 0 else [pad, w_v]
    wq = jnp.concatenate(q_tiles, axis=1)
    wkv = jnp.concatenate(k_tiles + v_cols, axis=1)
    lane = np.arange(N_HEADS * HEAD_PAD)
    ones_here = ((lane % HEAD_PAD) >= V_DIM) == ((lane // HEAD_PAD) % 2 == 0)
    vone = jnp.asarray(ones_here.astype(np.float32))[None, :]
    return win.astype(BF16), wq.astype(BF16), wkv.astype(BF16), vone


def _rope_tabs(seq):
    inv = ROPE_THETA ** (-jnp.arange(0, QK_ROPE, 2, dtype=F32) / QK_ROPE)
    ang = jnp.arange(seq, dtype=F32)[:, None] * inv[None, :]
    cos, sin = jnp.cos(ang), jnp.sin(ang)
    one = jnp.ones((seq, QK_NOPE), F32)
    z_nope = jnp.zeros((seq, QK_NOPE), F32)
    z_pad = jnp.zeros((seq, HEAD_PAD - QK_NOPE - QK_ROPE), F32)
    scale = (QK_NOPE + QK_ROPE) ** -0.5 * float(np.log2(np.e))
    c_full = jnp.concatenate([one, cos, cos, z_pad], axis=1)
    s_full = jnp.concatenate([z_nope, -sin, sin, z_pad], axis=1)
    c_rope = jnp.concatenate([z_nope, cos, cos, z_pad], axis=1)
    return c_full * scale, s_full * scale, c_rope, s_full


def _tri(tt):
    r = np.arange(tt)
    return jnp.asarray((r[None, :] < r[:, None]).astype(np.float32), dtype=BF16)


def kernel(x_prompt, x_sample, c_prompt, c_sample, w_ada, b_ada, w_in, q_norm_g, kv_norm_g,
           w_uq, w_ukv, w_pool, pool_scale, w_out, w_router, w_gate, w_up, w_down, final_norm_g):
    groups = ((x_prompt, c_prompt), (x_sample, c_sample))
    n_c = sum(c.shape[0] for _, c in groups)
    c_all = jnp.concatenate([c for _, c in groups]
                            + [jnp.zeros((-n_c % 8, D_MODEL), F32)], axis=0)
    mod_all = _ada(c_all, w_ada[0], b_ada)

    win, wq, wkv, vone = _prep_weights(w_in[0], w_uq[0], w_ukv[0])
    qg = q_norm_g[0][None, :]
    kvg = kv_norm_g[0][None, :]
    wp = w_pool[0].astype(BF16)
    ps = pool_scale[0][None, :]
    woa = w_out[0][:D_ATT].astype(BF16)
    wob = w_out[0][D_ATT:].astype(BF16)
    wr_pad = jnp.concatenate([w_router[0], jnp.zeros((D_MODEL, LANES - N_EXPERTS), F32)], axis=1)
    wr_hi = wr_pad.astype(BF16)
    wr = jnp.concatenate([wr_hi, (wr_pad - wr_hi.astype(F32)).astype(BF16)], axis=1)
    wg, wu, wd = w_gate[0], w_up[0], w_down[0]
    fg = final_norm_g[None, :]
    tri = _tri(TT_MOE)

    caps = [CAPACITY_FACTOR * x.shape[0] * x.shape[1] // N_EXPERTS for x, _ in groups]
    assert all(c % TC_FFN == 0 for c in caps)
    xes = []

    staged = []
    c_off = 0
    row0 = 0
    for (x, c), cap in zip(groups, caps):
        b, s, _ = x.shape
        mod = mod_all[c_off:c_off + b].reshape(b, N_MOD, D_MODEL)
        c_off += b
        q, k, v, u = _pre(x, mod, win, qg, kvg, wq, wkv, vone, _rope_tabs(s))
        o = _attn(q, k, v)
        x1, hx, aff, afft = _mix(x, o, u, mod, wp, ps, woa, wob, wr)
        n_tok = b * s
        starts_v, pos, post = _plan(afft, aff.reshape(n_tok, LANES), tri, cap)
        starts_s = starts_v[:, 0, :N_EXPERTS]
        xes.append(_disp(starts_s, hx.reshape(n_tok, X_WIDTH), post, cap))
        staged.append((x1.reshape(n_tok, D_MODEL), pos, starts_s, mod, row0, (b, s)))
        row0 += cap

    ye = _ffn(xes, caps, wg, wu, wd)

    outs = []
    for x1, pos, starts_s, mod, r0, (b, s) in staged:
        y = _comb(starts_s, x1, pos, mod, fg, ye, r0, s)
        outs.append(y.reshape(b, s, D_MODEL))
    return tuple(outs)
```

```python
import functools

import jax
import jax.numpy as jnp
import numpy as np
from jax import lax
from jax.experimental import pallas as pl
from jax.experimental.pallas import tpu as pltpu

F32 = jnp.float32
BF16 = jnp.bfloat16
I32 = jnp.int32

D_MODEL = 1024
N_HEADS = 8
QK_NOPE = 64
QK_ROPE = 32
V_DIM = 64
Q_LORA = 384
KV_LORA = 256
ROPE_THETA = 10000.0
D_ATT = N_HEADS * V_DIM
D_POOL = 512
POOL_WINDOWS = (2, 4, 8, 16)
POOL_GROUP = 128
N_EXPERTS = 16
CAPACITY_FACTOR = 2
EXPERT_FF = 1024
N_MOD = 6
EPS = 1e-6

LANES = 128
HEAD_PAD = LANES
HALF_ROPE = QK_ROPE // 2
D_IN_PAD = Q_LORA + KV_LORA + LANES + D_POOL
X_WIDTH = D_MODEL + LANES
POOL_HALO = 8
BF16_ROWS = 16

TS_PRE = 512
TQ_ATT = 512
ROWS_ATT = 512
TK_ATT = 4096
UNROLL_ATT = 2
TT_MOE = 256
CH_DISP = 64
CR_COMB = 64
SLOT_RADIX = 64.0
TC_FFN = 512
PLAN_UNROLL = 4
MIN_NORMAL = float(np.finfo(np.float32).tiny)
GEO_STEPS = 32
BISECT_STEPS = GEO_STEPS + 6
VMEM_LIMIT = 56 * 1024 * 1024


def _cparams(sem):
    return pltpu.CompilerParams(dimension_semantics=sem, vmem_limit_bytes=VMEM_LIMIT)


def _rms(x):
    return x * lax.rsqrt(jnp.mean(x * x, axis=-1, keepdims=True) + EPS)


def _ada_kernel(c_ref, w_ref, b_ref, o_ref):
    c = c_ref[...]
    s = c * jax.nn.sigmoid(c)
    o_ref[...] = jnp.dot(s, w_ref[...], preferred_element_type=F32,
                         precision=lax.Precision.HIGHEST) + b_ref[...]


def _ada(c_pad, w_ada, b_ada):
    rows = c_pad.shape[0]
    return pl.pallas_call(
        _ada_kernel,
        out_shape=jax.ShapeDtypeStruct((rows, N_MOD * D_MODEL), F32),
        grid=(N_MOD,),
        in_specs=[pl.BlockSpec((rows, D_MODEL), lambda j: (0, 0)),
                  pl.BlockSpec((D_MODEL, D_MODEL), lambda j: (0, j)),
                  pl.BlockSpec((1, D_MODEL), lambda j: (0, j))],
        out_specs=pl.BlockSpec((rows, D_MODEL), lambda j: (0, j)),
        compiler_params=_cparams(("arbitrary",)),
        name="ada",
    )(c_pad, w_ada, b_ada)


def _pre_kernel(x_ref, mod_ref, win_ref, qg_ref, kvg_ref, wq_ref, wkv_ref,
                cq_ref, sq_ref, ck_ref, sk_ref, vone_ref, q_ref, k_ref, v_ref, u_ref):
    x = x_ref[0]
    sh1 = mod_ref[0, 0:1, :]
    sc1 = mod_ref[0, 1:2, :]
    h = (_rms(x) * (1.0 + sc1) + sh1).astype(BF16)
    z = jnp.dot(h, win_ref[...], preferred_element_type=F32)
    c_q = z[:, :Q_LORA]
    c_kv = z[:, Q_LORA:Q_LORA + KV_LORA]
    kpe = z[:, Q_LORA + KV_LORA:Q_LORA + KV_LORA + LANES]
    u_ref[0] = z[:, Q_LORA + KV_LORA + LANES:]

    q = jnp.dot((_rms(c_q) * qg_ref[...]).astype(BF16), wq_ref[...],
                preferred_element_type=F32)
    cq_t = cq_ref[...]
    sq_t = sq_ref[...]
    for hd in range(N_HEADS):
        blk = q[:, hd * HEAD_PAD:(hd + 1) * HEAD_PAD]
        rot = pltpu.roll(blk, HEAD_PAD - HALF_ROPE, 1)
        q_ref[0, :, hd * HEAD_PAD:(hd + 1) * HEAD_PAD] = (blk * cq_t + rot * sq_t).astype(BF16)

    kv = jnp.dot((_rms(c_kv) * kvg_ref[...]).astype(BF16), wkv_ref[...],
                 preferred_element_type=F32)
    kpe_r = kpe * ck_ref[...] + pltpu.roll(kpe, HEAD_PAD - HALF_ROPE, 1) * sk_ref[...]
    k_full = jnp.concatenate([kv[:, hd * HEAD_PAD:(hd + 1) * HEAD_PAD] + kpe_r
                              for hd in range(N_HEADS)], axis=-1)
    k_ref[0, 0] = k_full.T.astype(BF16)
    v_ref[0] = (kv[:, N_HEADS * HEAD_PAD:] + vone_ref[...]).astype(BF16)


def _pre(x, mod, win, qg, kvg, wq, wkv, vone, tabs):
    b, s, _ = x.shape
    ts = TS_PRE
    tk = min(TK_ATT, s)
    per_chunk = tk // ts
    assert tk % ts == 0 and s % tk == 0
    const = lambda shape: pl.BlockSpec(shape, lambda bi, i: (0,) * len(shape))
    tab = pl.BlockSpec((ts, LANES), lambda bi, i: (i, 0))
    tok = lambda w: pl.BlockSpec((1, ts, w), lambda bi, i: (bi, i, 0))
    return pl.pallas_call(
        _pre_kernel,
        out_shape=(jax.ShapeDtypeStruct((b, s, N_HEADS * HEAD_PAD), BF16),
                   jax.ShapeDtypeStruct((b, s // tk, N_HEADS * HEAD_PAD, tk), BF16),
                   jax.ShapeDtypeStruct((b, s, N_HEADS * HEAD_PAD), BF16),
                   jax.ShapeDtypeStruct((b, s, D_POOL), F32)),
        grid=(b, s // ts),
        in_specs=[tok(D_MODEL),
                  pl.BlockSpec((1, N_MOD, D_MODEL), lambda bi, i: (bi, 0, 0)),
                  const((D_MODEL, D_IN_PAD)), const((1, Q_LORA)), const((1, KV_LORA)),
                  const((Q_LORA, N_HEADS * HEAD_PAD)),
                  const((KV_LORA, 2 * N_HEADS * HEAD_PAD)),
                  tab, tab, tab, tab, const((1, N_HEADS * HEAD_PAD))],
        out_specs=(tok(N_HEADS * HEAD_PAD),
                   pl.BlockSpec((1, 1, N_HEADS * HEAD_PAD, ts),
                                lambda bi, i: (bi, i // per_chunk, 0, i % per_chunk)),
                   tok(N_HEADS * HEAD_PAD), tok(D_POOL)),
        compiler_params=_cparams(("parallel", "arbitrary")),
        name="pre",
    )(x, mod, win, qg, kvg, wq, wkv, *tabs, vone)


def _attn_kernel(q_ref, k_ref, v_ref, o_ref, *, seq, tq, tk, rows, unroll):
    chains = [(r, hh) for r in range(tq // rows) for hh in range(2)]

    def body(c, carry):
        off = pl.multiple_of(c * tk, tk)
        new = []
        for (r, hh), (m, acc) in zip(chains, carry):
            cols = slice(hh * HEAD_PAD, (hh + 1) * HEAD_PAD)
            qh = q_ref[0, r * rows:(r + 1) * rows, cols]
            s = jnp.dot(qh, k_ref[0, c, cols, :], preferred_element_type=F32)
            m_new = jnp.maximum(m, jnp.max(s, axis=-1, keepdims=True))
            p = jnp.exp2((s - m_new).astype(BF16))
            acc = jnp.exp2(m - m_new) * acc + jnp.dot(p, v_ref[0, pl.ds(off, tk), cols],
                                                      preferred_element_type=F32)
            new.append((m_new, acc))
        return tuple(new)

    init = tuple((jnp.full((rows, 1), -jnp.inf, F32), jnp.zeros((rows, HEAD_PAD), F32))
                 for _ in chains)
    final = lax.fori_loop(0, seq // tk, body, init, unroll=unroll)
    lane = lax.broadcasted_iota(I32, (rows, HEAD_PAD), 1)
    for r in range(tq // rows):
        a0, a1 = final[2 * r][1], final[2 * r + 1][1]
        o_ref[0, r * rows:(r + 1) * rows, :] = jnp.where(
            lane < V_DIM, a0 / a0[:, V_DIM:V_DIM + 1], a1 / a1[:, 0:1]).astype(BF16)


def _attn(q, k, v):
    b, s, _ = q.shape
    tq, tk = TQ_ATT, min(TK_ATT, s)
    assert s % tq == 0 and s % tk == 0
    return pl.pallas_call(
        functools.partial(_attn_kernel, seq=s, tq=tq, tk=tk, rows=ROWS_ATT, unroll=UNROLL_ATT),
        out_shape=jax.ShapeDtypeStruct((b, s, D_ATT), BF16),
        grid=(b, N_HEADS // 2, s // tq),
        in_specs=[pl.BlockSpec((1, tq, 2 * HEAD_PAD), lambda bi, p, i: (bi, i, p)),
                  pl.BlockSpec((1, s // tk, 2 * HEAD_PAD, tk), lambda bi, p, i: (bi, 0, p, 0)),
                  pl.BlockSpec((1, s, 2 * HEAD_PAD), lambda bi, p, i: (bi, 0, p))],
        out_specs=pl.BlockSpec((1, tq, 2 * V_DIM), lambda bi, p, i: (bi, i, p)),
        compiler_params=_cparams(("parallel", "parallel", "arbitrary")),
        name="attn",
    )(q, k, v)


def _mix_kernel(x_ref, o_ref, u_ref, up_ref, un_ref, mod_ref, wp_ref, ps_ref, woa_ref, wob_ref,
                wr_ref, icnt_ref, x1_ref, hx_ref, aff_ref, afft_ref, ue_ref, *, ts):
    i = pl.program_id(1)
    nt = pl.num_programs(1)
    hb = POOL_HALO
    ue_ref[0:hb, :] = jnp.where(i > 0, up_ref[0], 0.0)
    ue_ref[hb:hb + ts, :] = u_ref[0]
    ue_ref[hb + ts:2 * hb + ts, :] = jnp.where(i < nt - 1, un_ref[0], 0.0)

    ys = []
    for g, w in enumerate(POOL_WINDOWS):
        cols = slice(g * POOL_GROUP, (g + 1) * POOL_GROUP)
        acc = None
        for off in range(-(w // 2), w - w // 2):
            sl = ue_ref[hb + off:hb + off + ts, cols]
            acc = sl if acc is None else acc + sl
        d = (acc * icnt_ref[:, g:g + 1] - ue_ref[hb:hb + ts, cols]).astype(BF16)
        y = jnp.dot(d, wp_ref[g], preferred_element_type=F32) * ps_ref[:, cols]
        ys.append(y.astype(BF16))
    pool = jnp.concatenate(ys, axis=-1)
    mixed = (jnp.dot(o_ref[0], woa_ref[...], preferred_element_type=F32)
             + jnp.dot(pool, wob_ref[...], preferred_element_type=F32))
    g1 = mod_ref[0, 2:3, :]
    sh2 = mod_ref[0, 3:4, :]
    sc2 = mod_ref[0, 4:5, :]
    x1 = x_ref[0] + g1 * mixed
    x1_ref[0] = x1
    h2 = _rms(x1) * (1.0 + sc2) + sh2
    hx_ref[0, :, 0:D_MODEL] = h2.astype(BF16)

    h_hi = h2.astype(BF16)
    h_lo = (h2 - h_hi.astype(F32)).astype(BF16)
    both = jnp.dot(h_hi, wr_ref[...], preferred_element_type=F32)
    logits = (both[:, :LANES] + both[:, LANES:]
              + jnp.dot(h_lo, wr_ref[:, :LANES], preferred_element_type=F32))
    lane = lax.broadcasted_iota(I32, (ts, LANES), 1)
    logits = jnp.where(lane < N_EXPERTS, logits, -jnp.inf)
    ex = jnp.exp(logits - jnp.max(logits, axis=-1, keepdims=True))
    aff = ex / jnp.sum(ex, axis=-1, keepdims=True)
    aff_ref[0] = aff
    afft_ref[0] = aff.T[0:N_EXPERTS, :]
    hi_part = aff.astype(BF16).astype(F32)
    lo_part = pltpu.roll(aff - hi_part, N_EXPERTS, 1)
    hx_ref[0, :, D_MODEL:] = jnp.where(lane < N_EXPERTS, hi_part, lo_part).astype(BF16)


def _pool_inv_counts(seq):
    t = np.arange(seq)[:, None]
    w = np.asarray(POOL_WINDOWS)[None, :]
    cnt = np.minimum(t + (w - w // 2), seq) - np.maximum(t - w // 2, 0)
    tab = np.zeros((seq, LANES), np.float64)
    tab[:, :len(POOL_WINDOWS)] = 1.0 / cnt
    return jnp.asarray(tab, dtype=F32)


def _mix(x, o, u, mod, wp, ps, woa, wob, wr):
    b, s, _ = x.shape
    ts = TS_PRE
    hpt = ts // POOL_HALO
    nh = s // POOL_HALO
    const = lambda shape: pl.BlockSpec(shape, lambda bi, i: (0,) * len(shape))
    tok = lambda w: pl.BlockSpec((1, ts, w), lambda bi, i: (bi, i, 0))
    return pl.pallas_call(
        functools.partial(_mix_kernel, ts=ts),
        out_shape=(jax.ShapeDtypeStruct((b, s, D_MODEL), F32),
                   jax.ShapeDtypeStruct((b, s, X_WIDTH), BF16),
                   jax.ShapeDtypeStruct((b, s, LANES), F32),
                   jax.ShapeDtypeStruct((b, N_EXPERTS, s), F32)),
        grid=(b, s // ts),
        in_specs=[tok(D_MODEL), tok(D_ATT), tok(D_POOL),
                  pl.BlockSpec((1, POOL_HALO, D_POOL),
                               lambda bi, i: (bi, jnp.maximum(i * hpt - 1, 0), 0)),
                  pl.BlockSpec((1, POOL_HALO, D_POOL),
                               lambda bi, i: (bi, jnp.minimum((i + 1) * hpt, nh - 1), 0)),
                  pl.BlockSpec((1, N_MOD, D_MODEL), lambda bi, i: (bi, 0, 0)),
                  const((len(POOL_WINDOWS), POOL_GROUP, POOL_GROUP)), const((1, D_POOL)),
                  const((D_ATT, D_MODEL)), const((D_POOL, D_MODEL)), const((D_MODEL, 2 * LANES)),
                  pl.BlockSpec((ts, LANES), lambda bi, i: (i, 0))],
        out_specs=(tok(D_MODEL), tok(X_WIDTH), tok(LANES),
                   pl.BlockSpec((1, N_EXPERTS, ts), lambda bi, i: (bi, 0, i))),
        scratch_shapes=[pltpu.VMEM((ts + 2 * POOL_HALO, D_POOL), F32)],
        compiler_params=_cparams(("parallel", "arbitrary")),
        name="mix",
    )(x, o, u, u, u, mod, wp, ps, woa, wob, wr, _pool_inv_counts(s))


def _plan_kernel(afft_ref, aff_ref, tri_ref, starts_ref, pos_ref, post_ref, split_ref, *, nb, cap,
                 tt):
    def count_ge(cand):
        def body(bi, acc):
            return acc + jnp.sum(jnp.where(afft_ref[bi] >= cand, 1.0, 0.0), axis=1, keepdims=True)
        return lax.fori_loop(0, nb, body, jnp.zeros((N_EXPERTS, 1), F32))

    def shrink(step, bounds):
        lo, hi = bounds
        mid = jnp.where(step < GEO_STEPS, jnp.sqrt(lo * hi), lo + 0.5 * (hi - lo))
        mid = jnp.clip(mid, lo, hi)
        enough = count_ge(mid) >= cap
        return jnp.where(enough, mid, lo), jnp.where(enough, hi, mid)

    lo_col, hi_col = lax.fori_loop(0, BISECT_STEPS, shrink,
                                   (jnp.full((N_EXPERTS, 1), MIN_NORMAL, F32),
                                    jnp.full((N_EXPERTS, 1), 2.0, F32)))
    tiny = count_ge(jnp.full((N_EXPERTS, 1), MIN_NORMAL, F32)) < cap
    lo_col = jnp.where(tiny, -1.0, lo_col)
    hi_col = jnp.where(tiny, MIN_NORMAL, hi_col)
    need_col = cap - count_ge(hi_col)

    sub = lax.broadcasted_iota(I32, (N_EXPERTS, LANES), 0)
    lane = lax.broadcasted_iota(I32, (N_EXPERTS, LANES), 1)
    real = lane[0:1, :] < N_EXPERTS

    def to_lanes(col, pad):
        row = jnp.sum(jnp.where(sub == lane, col, 0.0), axis=0, keepdims=True)
        return jnp.where(real, row, pad)

    lo = to_lanes(lo_col, 3.0)
    hi = to_lanes(hi_col, 3.0)
    need = to_lanes(need_col, 0.0)

    def tile_body(t, carry):
        sel_before, eq_before = carry
        off = pl.multiple_of(t * tt, tt)
        v = aff_ref[pl.ds(off, tt), :]
        eq = (v >= lo) & (v < hi)
        eq_f = jnp.where(eq, 1.0, 0.0)
        tri = tri_ref[...]
        rank = jnp.dot(tri, eq_f.astype(BF16), preferred_element_type=F32) + eq_before
        sel = (v >= hi) | (eq & (rank < need))
        sel_f = jnp.where(sel, 1.0, 0.0)
        pos = jnp.dot(tri, sel_f.astype(BF16), preferred_element_type=F32) + sel_before
        posm = jnp.where(sel, pos, -1.0)
        starts_ref[t] = jnp.broadcast_to(sel_before.astype(I32), (8, LANES))
        pos_ref[t] = posm
        post_ref[t] = posm.T[0:N_EXPERTS, :]
        slot1 = posm + 1.0
        high = jnp.floor(slot1 * (1.0 / SLOT_RADIX))
        split_ref[t] = jnp.concatenate([high, slot1 - SLOT_RADIX * high], axis=-1).astype(BF16)
        return (sel_before + jnp.sum(sel_f, axis=0, keepdims=True),
                eq_before + jnp.sum(eq_f, axis=0, keepdims=True))

    nt = pos_ref.shape[0]
    zero = jnp.zeros((1, LANES), F32)
    sel_total, _ = lax.fori_loop(0, nt, tile_body, (zero, zero), unroll=PLAN_UNROLL)
    starts_ref[nt] = jnp.broadcast_to(sel_total.astype(I32), (8, LANES))


def _plan(afft, aff2d, tri, cap):
    nb = afft.shape[0]
    n_tok = aff2d.shape[0]
    tt = TT_MOE
    nt = n_tok // tt
    vm = pl.BlockSpec(memory_space=pltpu.VMEM)
    return pl.pallas_call(
        functools.partial(_plan_kernel, nb=nb, cap=cap, tt=tt),
        out_shape=(jax.ShapeDtypeStruct((nt + 1, 8, LANES), I32),
                   jax.ShapeDtypeStruct((nt, tt, LANES), F32),
                   jax.ShapeDtypeStruct((nt, N_EXPERTS, tt), F32),
                   jax.ShapeDtypeStruct((nt, tt, 2 * LANES), BF16)),
        in_specs=[vm, vm, vm],
        out_specs=(vm, vm, vm, vm),
        compiler_params=pltpu.CompilerParams(vmem_limit_bytes=VMEM_LIMIT),
        name="plan",
    )(afft, aff2d, tri)


def _disp_copy(stage, xe_ref, sems, e, row, src_e=None):
    src_e = e if src_e is None else src_e
    return pltpu.make_async_copy(stage.at[pl.ds(src_e * CH_DISP, CH_DISP)],
                                 xe_ref.at[e, pl.ds(row, CH_DISP)], sems.at[e])


def _disp_kernel(starts_s, hx_ref, post_ref, xe_ref, stage, carry, sems, *, cap, nt, tt):
    t = pl.program_id(0)

    @pl.when(t == 0)
    def _():
        carry[...] = jnp.zeros(carry.shape, carry.dtype)
        stage[0:CH_DISP, :] = jnp.zeros((CH_DISP, X_WIDTH), BF16)
        for e in range(N_EXPERTS):
            _disp_copy(stage, xe_ref, sems, e, cap, src_e=0).start()
        for e in range(N_EXPERTS):
            _disp_copy(stage, xe_ref, sems, e, 0).wait()

    @pl.when(t > 0)
    def _():
        for e in range(N_EXPERTS):
            _disp_copy(stage, xe_ref, sems, e, 0).wait()

    pos_t = post_ref[0]
    slot_iota = lax.broadcasted_iota(I32, (CH_DISP, tt), 0).astype(F32)

    def window(e, base):
        onehot = jnp.where(pos_t[e:e + 1, :] - base.astype(F32) == slot_iota, 1.0, 0.0)
        return onehot.astype(BF16)

    def carry_rows(e, base):
        off = (starts_s[t + 1, e] & -BF16_ROWS) - base
        inside = jnp.minimum(off, CH_DISP - BF16_ROWS)
        blk = stage[pl.ds(pl.multiple_of(e * CH_DISP + inside, BF16_ROWS), BF16_ROWS), :]
        return jnp.where(off < CH_DISP, blk, jnp.zeros_like(blk))

    bases = [starts_s[t, e] & -BF16_ROWS for e in range(N_EXPERTS)]
    onehot = jnp.concatenate([window(e, bases[e]) for e in range(N_EXPERTS)], axis=0)
    stage[...] = jnp.dot(onehot, hx_ref[...], preferred_element_type=F32).astype(BF16)
    for e in range(N_EXPERTS):
        head = pl.ds(e * CH_DISP, BF16_ROWS)
        stage[head, :] = (stage[head, :].astype(F32) + carry[e].astype(F32)).astype(BF16)
        _disp_copy(stage, xe_ref, sems, e, pl.multiple_of(bases[e], BF16_ROWS)).start()
        carry[e] = carry_rows(e, bases[e])

    n_win = [(starts_s[t + 1, e] - bases[e] + (CH_DISP - 1)) // CH_DISP
             for e in range(N_EXPERTS)]
    most = functools.reduce(jnp.maximum, n_win)

    @pl.when(most > 1)
    def _():
        for e in range(N_EXPERTS):
            def more(c, _, e=e):
                cb = pl.multiple_of(bases[e] + c * CH_DISP, BF16_ROWS)
                extra = jnp.dot(window(e, cb), hx_ref[...], preferred_element_type=F32)
                _disp_copy(stage, xe_ref, sems, e, 0).wait()
                stage[pl.ds(e * CH_DISP, CH_DISP), :] = extra.astype(BF16)
                _disp_copy(stage, xe_ref, sems, e, cb).start()
                return 0

            lax.fori_loop(1, n_win[e], more, 0)

            @pl.when(n_win[e] > 1)
            def _(e=e):
                carry[e] = carry_rows(e, bases[e] + (n_win[e] - 1) * CH_DISP)

    @pl.when(t == nt - 1)
    def _():
        for e in range(N_EXPERTS):
            _disp_copy(stage, xe_ref, sems, e, 0).wait()


def _disp(starts_s, hx2d, post, cap):
    n_tok = hx2d.shape[0]
    tt = TT_MOE
    nt = n_tok // tt
    grid_spec = pltpu.PrefetchScalarGridSpec(
        num_scalar_prefetch=1,
        grid=(nt,),
        in_specs=[pl.BlockSpec((tt, X_WIDTH), lambda t, s: (t, 0)),
                  pl.BlockSpec((1, N_EXPERTS, tt), lambda t, s: (t, 0, 0))],
        out_specs=pl.BlockSpec(memory_space=pl.ANY),
        scratch_shapes=[pltpu.VMEM((N_EXPERTS * CH_DISP, X_WIDTH), BF16),
                        pltpu.VMEM((N_EXPERTS, BF16_ROWS, X_WIDTH), BF16),
                        pltpu.SemaphoreType.DMA((N_EXPERTS,))])
    return pl.pallas_call(
        functools.partial(_disp_kernel, cap=cap, nt=nt, tt=tt),
        out_shape=jax.ShapeDtypeStruct((N_EXPERTS, cap + CH_DISP, X_WIDTH), BF16),
        grid_spec=grid_spec,
        compiler_params=_cparams(("arbitrary",)),
        name="disp",
    )(starts_s, hx2d, post)


def _ffn_kernel(*refs, blocks):
    n_grp = len(blocks)
    xe_refs = refs[:n_grp]
    wg_ref, wu_ref, wd_ref, ye_ref, wg_bf, wu_bf, wd_bf = refs[n_grp:]
    e = pl.program_id(0)
    j = pl.program_id(1)

    @pl.when(j == 0)
    def _():
        wg_bf[...] = wg_ref[0].astype(BF16)
        wu_bf[...] = wu_ref[0].astype(BF16)
        wd_bf[...] = wd_ref[0].astype(BF16)
        ye_ref[...] = jnp.zeros(ye_ref.shape, ye_ref.dtype)

    def experts(xe_ref):
        blk = xe_ref[0]
        x = blk[:, 0:D_MODEL]
        gcols = blk[:, D_MODEL:].astype(F32)
        lane = lax.broadcasted_iota(I32, gcols.shape, 1)
        gate = jnp.sum(jnp.where((lane == e) | (lane == e + N_EXPERTS), gcols, 0.0),
                       axis=-1, keepdims=True)
        a = jnp.dot(x, wg_bf[...], preferred_element_type=F32)
        u = jnp.dot(x, wu_bf[...], preferred_element_type=F32)
        hmid = (a * jax.nn.sigmoid(a) * u).astype(BF16)
        y = jnp.dot(hmid, wd_bf[...], preferred_element_type=F32) * gate
        ye_ref[0] = y.astype(BF16)

    for g, (first, count) in enumerate(blocks):
        pl.when((j > first) & (j <= first + count))(functools.partial(experts, xe_refs[g]))


def _ffn(xes, caps, wg, wu, wd):
    tc = TC_FFN
    blocks, first = [], 0
    for cap in caps:
        blocks.append((first, cap // tc))
        first += cap // tc
    wspec = pl.BlockSpec((1, D_MODEL, EXPERT_FF), lambda e, j: (e, 0, 0))

    def slot_spec(first, count):
        return pl.BlockSpec((1, tc, X_WIDTH),
                            lambda e, j: (e, jnp.clip(j - 1 - first, 0, count - 1), 0))

    return pl.pallas_call(
        functools.partial(_ffn_kernel, blocks=tuple(blocks)),
        out_shape=jax.ShapeDtypeStruct((N_EXPERTS, (first + 1) * tc, D_MODEL), BF16),
        grid=(N_EXPERTS, first + 1),
        in_specs=[slot_spec(f, c) for f, c in blocks] + [wspec, wspec,
                  pl.BlockSpec((1, EXPERT_FF, D_MODEL), lambda e, j: (e, 0, 0))],
        out_specs=pl.BlockSpec((1, tc, D_MODEL),
                               lambda e, j: (e, jnp.where(j == 0, first, j - 1), 0)),
        scratch_shapes=[pltpu.VMEM((D_MODEL, EXPERT_FF), BF16), pltpu.VMEM((D_MODEL, EXPERT_FF), BF16),
                        pltpu.VMEM((EXPERT_FF, D_MODEL), BF16)],
        compiler_params=_cparams(("parallel", "arbitrary")),
        name="ffn",
    )(*xes, wg, wu, wd)


def _comb_copy(ye_ref, buf, sems, slot, e, row):
    return pltpu.make_async_copy(ye_ref.at[e, pl.ds(row, CR_COMB)],
                                 buf.at[slot, pl.ds(e * CR_COMB, CR_COMB)], sems.at[slot])


def _comb_kernel(starts_s, x1_ref, pos_ref, split_ref, spread_ref, mod_ref, fg_ref, ye_ref, y_ref,
                 buf, xbuf, acc_ref, sems, xsem, *, row0, nt, tt):
    t = pl.program_id(0)
    slot = lax.rem(t, 2)

    def issue(tile, sl):
        for e in range(N_EXPERTS):
            b0 = pl.multiple_of(starts_s[tile, e] & -BF16_ROWS, BF16_ROWS)
            _comb_copy(ye_ref, buf, sems, sl, e, row0 + b0).start()

    def wait_all(sl):
        pltpu.make_async_copy(ye_ref.at[0, pl.ds(0, N_EXPERTS * CR_COMB)], buf.at[sl],
                              sems.at[sl]).wait()

    @pl.when(t == 0)
    def _():
        issue(0, 0)

    issue(jnp.minimum(t + 1, nt - 1), 1 - slot)

    posm = pos_ref[0]
    bases = [starts_s[t, e] & -BF16_ROWS for e in range(N_EXPERTS)]
    slot1 = jnp.dot(split_ref[0], spread_ref[...], preferred_element_type=F32)
    col = lax.broadcasted_iota(I32, (1, N_EXPERTS * CR_COMB), 1)
    want = (col & (CR_COMB - 1)) + 1
    for e in range(N_EXPERTS):
        want = want + jnp.where(col // CR_COMB == e, bases[e], 0)
    onehot = jnp.where(slot1 == want.astype(F32), 1.0, 0.0).astype(BF16)

    wait_all(slot)
    acc_ref[...] = jnp.dot(onehot, buf[slot], preferred_element_type=F32)

    n_win = [(starts_s[t + 1, e] - bases[e] + (CR_COMB - 1)) // CR_COMB
             for e in range(N_EXPERTS)]
    most = functools.reduce(jnp.maximum, n_win)

    @pl.when(most > 1)
    def _():
        win_iota = lax.broadcasted_iota(I32, (tt, CR_COMB), 1).astype(F32)
        for e in range(N_EXPERTS):
            def extra(c, _, e=e):
                cb = pl.multiple_of(bases[e] + c * CR_COMB, BF16_ROWS)
                cp = pltpu.make_async_copy(ye_ref.at[e, pl.ds(row0 + cb, CR_COMB)], xbuf,
                                           xsem.at[0])
                cp.start()
                cp.wait()
                rel = posm[:, e:e + 1] - cb.astype(F32)
                oh = jnp.where(rel == win_iota, 1.0, 0.0).astype(BF16)
                acc_ref[...] += jnp.dot(oh, xbuf[...], preferred_element_type=F32)
                return 0

            lax.fori_loop(1, n_win[e], extra, 0)

    g2 = mod_ref[0, 5:6, :]
    y_ref[...] = _rms(x1_ref[...] + g2 * acc_ref[...]) * fg_ref[...]

    @pl.when(t == nt - 1)
    def _():
        wait_all(1 - slot)


def _slot_spread():
    w = np.zeros((2 * LANES, N_EXPERTS * CR_COMB), np.float32)
    for e in range(N_EXPERTS):
        w[e, e * CR_COMB:(e + 1) * CR_COMB] = SLOT_RADIX
        w[LANES + e, e * CR_COMB:(e + 1) * CR_COMB] = 1.0
    return jnp.asarray(w, dtype=BF16)


def _comb(starts_s, x1_2d, pos, split, mod, fg, ye, row0, seq):
    n_tok = x1_2d.shape[0]
    tt = TT_MOE
    nt = n_tok // tt
    tiles_per_seq = seq // tt
    grid_spec = pltpu.PrefetchScalarGridSpec(
        num_scalar_prefetch=1,
        grid=(nt,),
        in_specs=[pl.BlockSpec((tt, D_MODEL), lambda t, s: (t, 0)),
                  pl.BlockSpec((1, tt, LANES), lambda t, s: (t, 0, 0)),
                  pl.BlockSpec((1, tt, 2 * LANES), lambda t, s: (t, 0, 0)),
                  pl.BlockSpec((2 * LANES, N_EXPERTS * CR_COMB), lambda t, s: (0, 0)),
                  pl.BlockSpec((1, N_MOD, D_MODEL), lambda t, s: (t // tiles_per_seq, 0, 0)),
                  pl.BlockSpec((1, D_MODEL), lambda t, s: (0, 0)),
                  pl.BlockSpec(memory_space=pl.ANY)],
        out_specs=pl.BlockSpec((tt, D_MODEL), lambda t, s: (t, 0)),
        scratch_shapes=[pltpu.VMEM((2, N_EXPERTS * CR_COMB, D_MODEL), BF16),
                        pltpu.VMEM((CR_COMB, D_MODEL), BF16),
                        pltpu.VMEM((tt, D_MODEL), F32),
                        pltpu.SemaphoreType.DMA((2,)),
                        pltpu.SemaphoreType.DMA((1,))])
    return pl.pallas_call(
        functools.partial(_comb_kernel, row0=row0, nt=nt, tt=tt),
        out_shape=jax.ShapeDtypeStruct((n_tok, D_MODEL), F32),
        grid_spec=grid_spec,
        compiler_params=_cparams(("arbitrary",)),
        name="comb",
    )(starts_s, x1_2d, pos, split, _slot_spread(), mod, fg, ye)


def _head_tile_cols(base, nope):
    r = base + nope
    return (list(range(base, base + nope)) + list(range(r, r + QK_ROPE))
            + list(range(r, r + HALF_ROPE)))


def _prep_weights(w_in, w_uq, w_ukv):
    zcol = lambda w, n: jnp.zeros((w.shape[0], n), w.dtype)
    kpe0 = Q_LORA + KV_LORA
    kpe_tile = jnp.concatenate(
        [zcol(w_in, QK_NOPE), w_in[:, kpe0:kpe0 + QK_ROPE], w_in[:, kpe0:kpe0 + HALF_ROPE],
         zcol(w_in, HEAD_PAD - QK_NOPE - QK_ROPE - HALF_ROPE)], axis=1)
    win = jnp.concatenate([w_in[:, :kpe0], kpe_tile, w_in[:, kpe0 + QK_ROPE:]], axis=1)

    q_tiles, k_tiles, v_cols = [], [], []
    for hd in range(N_HEADS):
        cols = np.asarray(_head_tile_cols(hd * (QK_NOPE + QK_ROPE), QK_NOPE))
        q_tiles += [w_uq[:, cols], zcol(w_uq, HEAD_PAD - len(cols))]
        kb = hd * (QK_NOPE + V_DIM)
        k_tiles += [w_ukv[:, kb:kb + QK_NOPE], zcol(w_ukv, HEAD_PAD - QK_NOPE)]
        w_v = w_ukv[:, kb + QK_NOPE:kb + QK_NOPE + V_DIM]
        pad = zcol(w_ukv, HEAD_PAD - V_DIM)
        v_cols += [w_v, pad] if hd % 2 == 0 else [pad, w_v]
    wq = jnp.concatenate(q_tiles, axis=1)
    wkv = jnp.concatenate(k_tiles + v_cols, axis=1)
    lane = np.arange(N_HEADS * HEAD_PAD)
    ones_here = ((lane % HEAD_PAD) >= V_DIM) == ((lane // HEAD_PAD) % 2 == 0)
    vone = jnp.asarray(ones_here.astype(np.float32))[None, :]
    return win.astype(BF16), wq.astype(BF16), wkv.astype(BF16), vone


def _rope_tabs(seq):
    inv = ROPE_THETA ** (-jnp.arange(0, QK_ROPE, 2, dtype=F32) / QK_ROPE)
    ang = jnp.arange(seq, dtype=F32)[:, None] * inv[None, :]
    cos, sin = jnp.cos(ang), jnp.sin(ang)
    one = jnp.ones((seq, QK_NOPE), F32)
    z_nope = jnp.zeros((seq, QK_NOPE), F32)
    z_pad = jnp.zeros((seq, HEAD_PAD - QK_NOPE - QK_ROPE), F32)
    scale = (QK_NOPE + QK_ROPE) ** -0.5 * float(np.log2(np.e))
    c_full = jnp.concatenate([one, cos, cos, z_pad], axis=1)
    s_full = jnp.concatenate([z_nope, -sin, sin, z_pad], axis=1)
    c_rope = jnp.concatenate([z_nope, cos, cos, z_pad], axis=1)
    return c_full * scale, s_full * scale, c_rope, s_full


def _tri(tt):
    r = np.arange(tt)
    return jnp.asarray((r[None, :] < r[:, None]).astype(np.float32), dtype=BF16)


def kernel(x_prompt, x_sample, c_prompt, c_sample, w_ada, b_ada, w_in, q_norm_g, kv_norm_g,
           w_uq, w_ukv, w_pool, pool_scale, w_out, w_router, w_gate, w_up, w_down, final_norm_g):
    groups = ((x_prompt, c_prompt), (x_sample, c_sample))
    n_c = sum(c.shape[0] for _, c in groups)
    c_all = jnp.concatenate([c for _, c in groups]
                            + [jnp.zeros((-n_c % 8, D_MODEL), F32)], axis=0)
    mod_all = _ada(c_all, w_ada[0], b_ada)

    win, wq, wkv, vone = _prep_weights(w_in[0], w_uq[0], w_ukv[0])
    qg = q_norm_g[0][None, :]
    kvg = kv_norm_g[0][None, :]
    wp = w_pool[0].astype(BF16)
    ps = pool_scale[0][None, :]
    woa = w_out[0][:D_ATT].astype(BF16)
    wob = w_out[0][D_ATT:].astype(BF16)
    wr_pad = jnp.concatenate([w_router[0], jnp.zeros((D_MODEL, LANES - N_EXPERTS), F32)], axis=1)
    wr_hi = wr_pad.astype(BF16)
    wr = jnp.concatenate([wr_hi, (wr_pad - wr_hi.astype(F32)).astype(BF16)], axis=1)
    wg, wu, wd = w_gate[0], w_up[0], w_down[0]
    fg = final_norm_g[None, :]
    tri = _tri(TT_MOE)

    caps = [CAPACITY_FACTOR * x.shape[0] * x.shape[1] // N_EXPERTS for x, _ in groups]
    assert all(c % TC_FFN == 0 for c in caps)
    xes = []

    staged = []
    c_off = 0
    row0 = 0
    for (x, c), cap in zip(groups, caps):
        b, s, _ = x.shape
        mod = mod_all[c_off:c_off + b].reshape(b, N_MOD, D_MODEL)
        c_off += b
        q, k, v, u = _pre(x, mod, win, qg, kvg, wq, wkv, vone, _rope_tabs(s))
        o = _attn(q, k, v)
        x1, hx, aff, afft = _mix(x, o, u, mod, wp, ps, woa, wob, wr)
        n_tok = b * s
        starts_v, pos, post, split = _plan(afft, aff.reshape(n_tok, LANES), tri, cap)
        starts_s = starts_v[:, 0, :N_EXPERTS]
        xes.append(_disp(starts_s, hx.reshape(n_tok, X_WIDTH), post, cap))
        staged.append((x1.reshape(n_tok, D_MODEL), pos, split, starts_s, mod, row0, (b, s)))
        row0 += cap

    ye = _ffn(xes, caps, wg, wu, wd)

    outs = []
    for x1, pos, split, starts_s, mod, r0, (b, s) in staged:
        y = _comb(starts_s, x1, pos, split, mod, fg, ye, r0, s)
        outs.append(y.reshape(b, s, D_MODEL))
    return tuple(outs)
```

```python
import functools

import jax
import jax.numpy as jnp
import numpy as np
from jax import lax
from jax.experimental import pallas as pl
from jax.experimental.pallas import tpu as pltpu

F32 = jnp.float32
BF16 = jnp.bfloat16
I32 = jnp.int32

D_MODEL = 1024
N_HEADS = 8
QK_NOPE = 64
QK_ROPE = 32
V_DIM = 64
Q_LORA = 384
KV_LORA = 256
ROPE_THETA = 10000.0
D_ATT = N_HEADS * V_DIM
D_POOL = 512
POOL_WINDOWS = (2, 4, 8, 16)
POOL_GROUP = 128
N_EXPERTS = 16
CAPACITY_FACTOR = 2
EXPERT_FF = 1024
N_MOD = 6
EPS = 1e-6

LANES = 128
HEAD_PAD = LANES
HALF_ROPE = QK_ROPE // 2
D_IN_PAD = Q_LORA + KV_LORA + LANES + D_POOL
X_WIDTH = D_MODEL + LANES
POOL_HALO = 8
BF16_ROWS = 16

TS_PRE = 512
TQ_ATT = 512
ROWS_ATT = 512
TK_ATT = 4096
UNROLL_ATT = 2
TT_MOE = 256
CH_DISP = 64
CR_COMB = 64
SLOT_RADIX = 64.0
TC_FFN = 1024
PLAN_UNROLL = 4
MIN_NORMAL = float(np.finfo(np.float32).tiny)
GEO_STEPS = 32
BISECT_STEPS = GEO_STEPS + 6
VMEM_LIMIT = 56 * 1024 * 1024


def _cparams(sem):
    return pltpu.CompilerParams(dimension_semantics=sem, vmem_limit_bytes=VMEM_LIMIT)


def _rms(x):
    return x * lax.rsqrt(jnp.mean(x * x, axis=-1, keepdims=True) + EPS)


def _ada_kernel(c_ref, w_ref, b_ref, o_ref):
    c = c_ref[...]
    s = c * jax.nn.sigmoid(c)
    o_ref[...] = jnp.dot(s, w_ref[...], preferred_element_type=F32,
                         precision=lax.Precision.HIGHEST) + b_ref[...]


def _ada(c_pad, w_ada, b_ada):
    rows = c_pad.shape[0]
    return pl.pallas_call(
        _ada_kernel,
        out_shape=jax.ShapeDtypeStruct((rows, N_MOD * D_MODEL), F32),
        grid=(N_MOD,),
        in_specs=[pl.BlockSpec((rows, D_MODEL), lambda j: (0, 0)),
                  pl.BlockSpec((D_MODEL, D_MODEL), lambda j: (0, j)),
                  pl.BlockSpec((1, D_MODEL), lambda j: (0, j))],
        out_specs=pl.BlockSpec((rows, D_MODEL), lambda j: (0, j)),
        compiler_params=_cparams(("arbitrary",)),
        name="ada",
    )(c_pad, w_ada, b_ada)


def _pre_kernel(x_ref, mod_ref, win_ref, qg_ref, kvg_ref, wq_ref, wkv_ref,
                cq_ref, sq_ref, ck_ref, sk_ref, vone_ref, q_ref, k_ref, v_ref, u_ref):
    x = x_ref[0]
    sh1 = mod_ref[0, 0:1, :]
    sc1 = mod_ref[0, 1:2, :]
    h = (_rms(x) * (1.0 + sc1) + sh1).astype(BF16)
    z = jnp.dot(h, win_ref[...], preferred_element_type=F32)
    c_q = z[:, :Q_LORA]
    c_kv = z[:, Q_LORA:Q_LORA + KV_LORA]
    kpe = z[:, Q_LORA + KV_LORA:Q_LORA + KV_LORA + LANES]
    u_ref[0] = z[:, Q_LORA + KV_LORA + LANES:]

    q = jnp.dot((_rms(c_q) * qg_ref[...]).astype(BF16), wq_ref[...],
                preferred_element_type=F32)
    cq_t = cq_ref[...]
    sq_t = sq_ref[...]
    for hd in range(N_HEADS):
        blk = q[:, hd * HEAD_PAD:(hd + 1) * HEAD_PAD]
        rot = pltpu.roll(blk, HEAD_PAD - HALF_ROPE, 1)
        q_ref[0, :, hd * HEAD_PAD:(hd + 1) * HEAD_PAD] = (blk * cq_t + rot * sq_t).astype(BF16)

    kv = jnp.dot((_rms(c_kv) * kvg_ref[...]).astype(BF16), wkv_ref[...],
                 preferred_element_type=F32)
    kpe_r = kpe * ck_ref[...] + pltpu.roll(kpe, HEAD_PAD - HALF_ROPE, 1) * sk_ref[...]
    k_full = jnp.concatenate([kv[:, hd * HEAD_PAD:(hd + 1) * HEAD_PAD] + kpe_r
                              for hd in range(N_HEADS)], axis=-1)
    k_ref[0, 0] = k_full.T.astype(BF16)
    v_ref[0] = (kv[:, N_HEADS * HEAD_PAD:] + vone_ref[...]).astype(BF16)


def _pre(x, mod, win, qg, kvg, wq, wkv, vone, tabs):
    b, s, _ = x.shape
    ts = TS_PRE
    tk = min(TK_ATT, s)
    per_chunk = tk // ts
    assert tk % ts == 0 and s % tk == 0
    const = lambda shape: pl.BlockSpec(shape, lambda bi, i: (0,) * len(shape))
    tab = pl.BlockSpec((ts, LANES), lambda bi, i: (i, 0))
    tok = lambda w: pl.BlockSpec((1, ts, w), lambda bi, i: (bi, i, 0))
    return pl.pallas_call(
        _pre_kernel,
        out_shape=(jax.ShapeDtypeStruct((b, s, N_HEADS * HEAD_PAD), BF16),
                   jax.ShapeDtypeStruct((b, s // tk, N_HEADS * HEAD_PAD, tk), BF16),
                   jax.ShapeDtypeStruct((b, s, N_HEADS * HEAD_PAD), BF16),
                   jax.ShapeDtypeStruct((b, s, D_POOL), F32)),
        grid=(b, s // ts),
        in_specs=[tok(D_MODEL),
                  pl.BlockSpec((1, N_MOD, D_MODEL), lambda bi, i: (bi, 0, 0)),
                  const((D_MODEL, D_IN_PAD)), const((1, Q_LORA)), const((1, KV_LORA)),
                  const((Q_LORA, N_HEADS * HEAD_PAD)),
                  const((KV_LORA, 2 * N_HEADS * HEAD_PAD)),
                  tab, tab, tab, tab, const((1, N_HEADS * HEAD_PAD))],
        out_specs=(tok(N_HEADS * HEAD_PAD),
                   pl.BlockSpec((1, 1, N_HEADS * HEAD_PAD, ts),
                                lambda bi, i: (bi, i // per_chunk, 0, i % per_chunk)),
                   tok(N_HEADS * HEAD_PAD), tok(D_POOL)),
        compiler_params=_cparams(("parallel", "arbitrary")),
        name="pre",
    )(x, mod, win, qg, kvg, wq, wkv, *tabs, vone)


def _attn_kernel(q_ref, k_ref, v_ref, o_ref, *, seq, tq, tk, rows, unroll):
    chains = [(r, hh) for r in range(tq // rows) for hh in range(2)]

    def body(c, carry):
        off = pl.multiple_of(c * tk, tk)
        new = []
        for (r, hh), (m, acc) in zip(chains, carry):
            cols = slice(hh * HEAD_PAD, (hh + 1) * HEAD_PAD)
            qh = q_ref[0, r * rows:(r + 1) * rows, cols]
            s = jnp.dot(qh, k_ref[0, c, cols, :], preferred_element_type=F32)
            m_new = jnp.maximum(m, jnp.max(s, axis=-1, keepdims=True))
            p = jnp.exp2((s - m_new).astype(BF16))
            acc = jnp.exp2(m - m_new) * acc + jnp.dot(p, v_ref[0, pl.ds(off, tk), cols],
                                                      preferred_element_type=F32)
            new.append((m_new, acc))
        return tuple(new)

    init = tuple((jnp.full((rows, 1), -jnp.inf, F32), jnp.zeros((rows, HEAD_PAD), F32))
                 for _ in chains)
    final = lax.fori_loop(0, seq // tk, body, init, unroll=unroll)
    lane = lax.broadcasted_iota(I32, (rows, HEAD_PAD), 1)
    for r in range(tq // rows):
        a0, a1 = final[2 * r][1], final[2 * r + 1][1]
        o_ref[0, r * rows:(r + 1) * rows, :] = jnp.where(
            lane < V_DIM, a0 / a0[:, V_DIM:V_DIM + 1], a1 / a1[:, 0:1]).astype(BF16)


def _attn(q, k, v):
    b, s, _ = q.shape
    tq, tk = TQ_ATT, min(TK_ATT, s)
    assert s % tq == 0 and s % tk == 0
    return pl.pallas_call(
        functools.partial(_attn_kernel, seq=s, tq=tq, tk=tk, rows=ROWS_ATT, unroll=UNROLL_ATT),
        out_shape=jax.ShapeDtypeStruct((b, s, D_ATT), BF16),
        grid=(b, N_HEADS // 2, s // tq),
        in_specs=[pl.BlockSpec((1, tq, 2 * HEAD_PAD), lambda bi, p, i: (bi, i, p)),
                  pl.BlockSpec((1, s // tk, 2 * HEAD_PAD, tk), lambda bi, p, i: (bi, 0, p, 0)),
                  pl.BlockSpec((1, s, 2 * HEAD_PAD), lambda bi, p, i: (bi, 0, p))],
        out_specs=pl.BlockSpec((1, tq, 2 * V_DIM), lambda bi, p, i: (bi, i, p)),
        compiler_params=_cparams(("parallel", "parallel", "arbitrary")),
        name="attn",
    )(q, k, v)


def _mix_kernel(x_ref, o_ref, u_ref, up_ref, un_ref, mod_ref, wp_ref, ps_ref, woa_ref, wob_ref,
                wr_ref, icnt_ref, x1_ref, hx_ref, aff_ref, afft_ref, ue_ref, *, ts):
    i = pl.program_id(1)
    nt = pl.num_programs(1)
    hb = POOL_HALO
    ue_ref[0:hb, :] = jnp.where(i > 0, up_ref[0], 0.0)
    ue_ref[hb:hb + ts, :] = u_ref[0]
    ue_ref[hb + ts:2 * hb + ts, :] = jnp.where(i < nt - 1, un_ref[0], 0.0)

    ys = []
    for g, w in enumerate(POOL_WINDOWS):
        cols = slice(g * POOL_GROUP, (g + 1) * POOL_GROUP)
        acc = None
        for off in range(-(w // 2), w - w // 2):
            sl = ue_ref[hb + off:hb + off + ts, cols]
            acc = sl if acc is None else acc + sl
        d = (acc * icnt_ref[:, g:g + 1] - ue_ref[hb:hb + ts, cols]).astype(BF16)
        y = jnp.dot(d, wp_ref[g], preferred_element_type=F32) * ps_ref[:, cols]
        ys.append(y.astype(BF16))
    pool = jnp.concatenate(ys, axis=-1)
    mixed = (jnp.dot(o_ref[0], woa_ref[...], preferred_element_type=F32)
             + jnp.dot(pool, wob_ref[...], preferred_element_type=F32))
    g1 = mod_ref[0, 2:3, :]
    sh2 = mod_ref[0, 3:4, :]
    sc2 = mod_ref[0, 4:5, :]
    x1 = x_ref[0] + g1 * mixed
    x1_ref[0] = x1
    h2 = _rms(x1) * (1.0 + sc2) + sh2
    hx_ref[0, :, 0:D_MODEL] = h2.astype(BF16)

    h_hi = h2.astype(BF16)
    h_lo = (h2 - h_hi.astype(F32)).astype(BF16)
    both = jnp.dot(h_hi, wr_ref[...], preferred_element_type=F32)
    logits = (both[:, :LANES] + both[:, LANES:]
              + jnp.dot(h_lo, wr_ref[:, :LANES], preferred_element_type=F32))
    lane = lax.broadcasted_iota(I32, (ts, LANES), 1)
    logits = jnp.where(lane < N_EXPERTS, logits, -jnp.inf)
    ex = jnp.exp(logits - jnp.max(logits, axis=-1, keepdims=True))
    aff = ex / jnp.sum(ex, axis=-1, keepdims=True)
    aff_ref[0] = aff
    afft_ref[0] = aff.T[0:N_EXPERTS, :]
    hi_part = aff.astype(BF16).astype(F32)
    lo_part = pltpu.roll(aff - hi_part, N_EXPERTS, 1)
    hx_ref[0, :, D_MODEL:] = jnp.where(lane < N_EXPERTS, hi_part, lo_part).astype(BF16)


def _pool_inv_counts(seq):
    t = np.arange(seq)[:, None]
    w = np.asarray(POOL_WINDOWS)[None, :]
    cnt = np.minimum(t + (w - w // 2), seq) - np.maximum(t - w // 2, 0)
    tab = np.zeros((seq, LANES), np.float64)
    tab[:, :len(POOL_WINDOWS)] = 1.0 / cnt
    return jnp.asarray(tab, dtype=F32)


def _mix(x, o, u, mod, wp, ps, woa, wob, wr):
    b, s, _ = x.shape
    ts = TS_PRE
    hpt = ts // POOL_HALO
    nh = s // POOL_HALO
    const = lambda shape: pl.BlockSpec(shape, lambda bi, i: (0,) * len(shape))
    tok = lambda w: pl.BlockSpec((1, ts, w), lambda bi, i: (bi, i, 0))
    return pl.pallas_call(
        functools.partial(_mix_kernel, ts=ts),
        out_shape=(jax.ShapeDtypeStruct((b, s, D_MODEL), F32),
                   jax.ShapeDtypeStruct((b, s, X_WIDTH), BF16),
                   jax.ShapeDtypeStruct((b, s, LANES), F32),
                   jax.ShapeDtypeStruct((b, N_EXPERTS, s), F32)),
        grid=(b, s // ts),
        in_specs=[tok(D_MODEL), tok(D_ATT), tok(D_POOL),
                  pl.BlockSpec((1, POOL_HALO, D_POOL),
                               lambda bi, i: (bi, jnp.maximum(i * hpt - 1, 0), 0)),
                  pl.BlockSpec((1, POOL_HALO, D_POOL),
                               lambda bi, i: (bi, jnp.minimum((i + 1) * hpt, nh - 1), 0)),
                  pl.BlockSpec((1, N_MOD, D_MODEL), lambda bi, i: (bi, 0, 0)),
                  const((len(POOL_WINDOWS), POOL_GROUP, POOL_GROUP)), const((1, D_POOL)),
                  const((D_ATT, D_MODEL)), const((D_POOL, D_MODEL)), const((D_MODEL, 2 * LANES)),
                  pl.BlockSpec((ts, LANES), lambda bi, i: (i, 0))],
        out_specs=(tok(D_MODEL), tok(X_WIDTH), tok(LANES),
                   pl.BlockSpec((1, N_EXPERTS, ts), lambda bi, i: (bi, 0, i))),
        scratch_shapes=[pltpu.VMEM((ts + 2 * POOL_HALO, D_POOL), F32)],
        compiler_params=_cparams(("parallel", "arbitrary")),
        name="mix",
    )(x, o, u, u, u, mod, wp, ps, woa, wob, wr, _pool_inv_counts(s))


def _plan_kernel(afft_ref, aff_ref, tri_ref, starts_ref, pos_ref, post_ref, split_ref, *, nb, cap,
                 tt):
    def count_ge(cand):
        def body(bi, acc):
            return acc + jnp.sum(jnp.where(afft_ref[bi] >= cand, 1.0, 0.0), axis=1, keepdims=True)
        return lax.fori_loop(0, nb, body, jnp.zeros((N_EXPERTS, 1), F32))

    def shrink(step, bounds):
        lo, hi = bounds
        mid = jnp.where(step < GEO_STEPS, jnp.sqrt(lo * hi), lo + 0.5 * (hi - lo))
        mid = jnp.clip(mid, lo, hi)
        enough = count_ge(mid) >= cap
        return jnp.where(enough, mid, lo), jnp.where(enough, hi, mid)

    lo_col, hi_col = lax.fori_loop(0, BISECT_STEPS, shrink,
                                   (jnp.full((N_EXPERTS, 1), MIN_NORMAL, F32),
                                    jnp.full((N_EXPERTS, 1), 2.0, F32)))
    tiny = count_ge(jnp.full((N_EXPERTS, 1), MIN_NORMAL, F32)) < cap
    lo_col = jnp.where(tiny, -1.0, lo_col)
    hi_col = jnp.where(tiny, MIN_NORMAL, hi_col)
    need_col = cap - count_ge(hi_col)

    sub = lax.broadcasted_iota(I32, (N_EXPERTS, LANES), 0)
    lane = lax.broadcasted_iota(I32, (N_EXPERTS, LANES), 1)
    real = lane[0:1, :] < N_EXPERTS

    def to_lanes(col, pad):
        row = jnp.sum(jnp.where(sub == lane, col, 0.0), axis=0, keepdims=True)
        return jnp.where(real, row, pad)

    lo = to_lanes(lo_col, 3.0)
    hi = to_lanes(hi_col, 3.0)
    need = to_lanes(need_col, 0.0)

    def tile_body(t, carry):
        sel_before, eq_before = carry
        off = pl.multiple_of(t * tt, tt)
        v = aff_ref[pl.ds(off, tt), :]
        eq = (v >= lo) & (v < hi)
        eq_f = jnp.where(eq, 1.0, 0.0)
        tri = tri_ref[...]
        rank = jnp.dot(tri, eq_f.astype(BF16), preferred_element_type=F32) + eq_before
        sel = (v >= hi) | (eq & (rank < need))
        sel_f = jnp.where(sel, 1.0, 0.0)
        pos = jnp.dot(tri, sel_f.astype(BF16), preferred_element_type=F32) + sel_before
        posm = jnp.where(sel, pos, -1.0)
        starts_ref[t] = jnp.broadcast_to(sel_before.astype(I32), (8, LANES))
        pos_ref[t] = posm
        post_ref[t] = posm.T[0:N_EXPERTS, :]
        slot1 = posm + 1.0
        high = jnp.floor(slot1 * (1.0 / SLOT_RADIX))
        split_ref[t] = jnp.concatenate([high, slot1 - SLOT_RADIX * high], axis=-1).astype(BF16)
        return (sel_before + jnp.sum(sel_f, axis=0, keepdims=True),
                eq_before + jnp.sum(eq_f, axis=0, keepdims=True))

    nt = pos_ref.shape[0]
    zero = jnp.zeros((1, LANES), F32)
    sel_total, _ = lax.fori_loop(0, nt, tile_body, (zero, zero), unroll=PLAN_UNROLL)
    starts_ref[nt] = jnp.broadcast_to(sel_total.astype(I32), (8, LANES))


def _plan(afft, aff2d, tri, cap):
    nb = afft.shape[0]
    n_tok = aff2d.shape[0]
    tt = TT_MOE
    nt = n_tok // tt
    vm = pl.BlockSpec(memory_space=pltpu.VMEM)
    return pl.pallas_call(
        functools.partial(_plan_kernel, nb=nb, cap=cap, tt=tt),
        out_shape=(jax.ShapeDtypeStruct((nt + 1, 8, LANES), I32),
                   jax.ShapeDtypeStruct((nt, tt, LANES), F32),
                   jax.ShapeDtypeStruct((nt, N_EXPERTS, tt), F32),
                   jax.ShapeDtypeStruct((nt, tt, 2 * LANES), BF16)),
        in_specs=[vm, vm, vm],
        out_specs=(vm, vm, vm, vm),
        compiler_params=pltpu.CompilerParams(vmem_limit_bytes=VMEM_LIMIT),
        name="plan",
    )(afft, aff2d, tri)


def _disp_copy(stage, xe_ref, sems, e, row, src_e=None):
    src_e = e if src_e is None else src_e
    return pltpu.make_async_copy(stage.at[pl.ds(src_e * CH_DISP, CH_DISP)],
                                 xe_ref.at[e, pl.ds(row, CH_DISP)], sems.at[e])


def _disp_kernel(starts_s, hx_ref, post_ref, xe_ref, stages, carry, sems, *, cap, nt, tt):
    t = pl.program_id(0)
    stage = stages.at[lax.rem(t, 2)]

    @pl.when(t == 0)
    def _():
        carry[...] = jnp.zeros(carry.shape, carry.dtype)
        idle = stages.at[1]
        idle[0:CH_DISP, :] = jnp.zeros((CH_DISP, X_WIDTH), BF16)
        for e in range(N_EXPERTS):
            _disp_copy(idle, xe_ref, sems, e, cap, src_e=0).start()

    pos_t = post_ref[0]
    slot_iota = lax.broadcasted_iota(I32, (CH_DISP, tt), 0).astype(F32)

    def window(e, base):
        onehot = jnp.where(pos_t[e:e + 1, :] - base.astype(F32) == slot_iota, 1.0, 0.0)
        return onehot.astype(BF16)

    def carry_rows(e, base):
        off = (starts_s[t + 1, e] & -BF16_ROWS) - base
        inside = jnp.minimum(off, CH_DISP - BF16_ROWS)
        blk = stage[pl.ds(pl.multiple_of(e * CH_DISP + inside, BF16_ROWS), BF16_ROWS), :]
        return jnp.where(off < CH_DISP, blk, jnp.zeros_like(blk))

    bases = [starts_s[t, e] & -BF16_ROWS for e in range(N_EXPERTS)]
    onehot = jnp.concatenate([window(e, bases[e]) for e in range(N_EXPERTS)], axis=0)
    stage[...] = jnp.dot(onehot, hx_ref[...], preferred_element_type=F32).astype(BF16)
    for e in range(N_EXPERTS):
        head = pl.ds(e * CH_DISP, BF16_ROWS)
        stage[head, :] = (stage[head, :].astype(F32) + carry[e].astype(F32)).astype(BF16)
    for e in range(N_EXPERTS):
        _disp_copy(stage, xe_ref, sems, e, 0).wait()
        _disp_copy(stage, xe_ref, sems, e, pl.multiple_of(bases[e], BF16_ROWS)).start()
        carry[e] = carry_rows(e, bases[e])

    n_win = [(starts_s[t + 1, e] - bases[e] + (CH_DISP - 1)) // CH_DISP
             for e in range(N_EXPERTS)]
    most = functools.reduce(jnp.maximum, n_win)

    @pl.when(most > 1)
    def _():
        for e in range(N_EXPERTS):
            def more(c, _, e=e):
                cb = pl.multiple_of(bases[e] + c * CH_DISP, BF16_ROWS)
                extra = jnp.dot(window(e, cb), hx_ref[...], preferred_element_type=F32)
                _disp_copy(stage, xe_ref, sems, e, 0).wait()
                stage[pl.ds(e * CH_DISP, CH_DISP), :] = extra.astype(BF16)
                _disp_copy(stage, xe_ref, sems, e, cb).start()
                return 0

            lax.fori_loop(1, n_win[e], more, 0)

            @pl.when(n_win[e] > 1)
            def _(e=e):
                carry[e] = carry_rows(e, bases[e] + (n_win[e] - 1) * CH_DISP)

    @pl.when(t == nt - 1)
    def _():
        for e in range(N_EXPERTS):
            _disp_copy(stage, xe_ref, sems, e, 0).wait()


def _disp(starts_s, hx2d, post, cap):
    n_tok = hx2d.shape[0]
    tt = TT_MOE
    nt = n_tok // tt
    grid_spec = pltpu.PrefetchScalarGridSpec(
        num_scalar_prefetch=1,
        grid=(nt,),
        in_specs=[pl.BlockSpec((tt, X_WIDTH), lambda t, s: (t, 0)),
                  pl.BlockSpec((1, N_EXPERTS, tt), lambda t, s: (t, 0, 0))],
        out_specs=pl.BlockSpec(memory_space=pl.ANY),
        scratch_shapes=[pltpu.VMEM((2, N_EXPERTS * CH_DISP, X_WIDTH), BF16),
                        pltpu.VMEM((N_EXPERTS, BF16_ROWS, X_WIDTH), BF16),
                        pltpu.SemaphoreType.DMA((N_EXPERTS,))])
    return pl.pallas_call(
        functools.partial(_disp_kernel, cap=cap, nt=nt, tt=tt),
        out_shape=jax.ShapeDtypeStruct((N_EXPERTS, cap + CH_DISP, X_WIDTH), BF16),
        grid_spec=grid_spec,
        compiler_params=_cparams(("arbitrary",)),
        name="disp",
    )(starts_s, hx2d, post)


def _ffn_kernel(*refs, blocks):
    n_grp = len(blocks)
    xe_refs = refs[:n_grp]
    wg_ref, wu_ref, wd_ref, ye_ref, wg_bf, wu_bf, wd_bf = refs[n_grp:]
    e = pl.program_id(0)
    j = pl.program_id(1)

    @pl.when(j == 0)
    def _():
        wg_bf[...] = wg_ref[0].astype(BF16)
        wu_bf[...] = wu_ref[0].astype(BF16)
        wd_bf[...] = wd_ref[0].astype(BF16)
        ye_ref[...] = jnp.zeros(ye_ref.shape, ye_ref.dtype)

    def experts(xe_ref):
        blk = xe_ref[0]
        x = blk[:, 0:D_MODEL]
        gcols = blk[:, D_MODEL:].astype(F32)
        lane = lax.broadcasted_iota(I32, gcols.shape, 1)
        gate = jnp.sum(jnp.where((lane == e) | (lane == e + N_EXPERTS), gcols, 0.0),
                       axis=-1, keepdims=True)
        a = jnp.dot(x, wg_bf[...], preferred_element_type=F32)
        u = jnp.dot(x, wu_bf[...], preferred_element_type=F32)
        hmid = (a * jax.nn.sigmoid(a) * u).astype(BF16)
        y = jnp.dot(hmid, wd_bf[...], preferred_element_type=F32) * gate
        ye_ref[0] = y.astype(BF16)

    for g, (first, count) in enumerate(blocks):
        pl.when((j > first) & (j <= first + count))(functools.partial(experts, xe_refs[g]))


def _ffn(xes, caps, wg, wu, wd):
    tc = TC_FFN
    blocks, first = [], 0
    for cap in caps:
        blocks.append((first, cap // tc))
        first += cap // tc
    wspec = pl.BlockSpec((1, D_MODEL, EXPERT_FF), lambda e, j: (e, 0, 0))

    def slot_spec(first, count):
        return pl.BlockSpec((1, tc, X_WIDTH),
                            lambda e, j: (e, jnp.clip(j - 1 - first, 0, count - 1), 0))

    return pl.pallas_call(
        functools.partial(_ffn_kernel, blocks=tuple(blocks)),
        out_shape=jax.ShapeDtypeStruct((N_EXPERTS, (first + 1) * tc, D_MODEL), BF16),
        grid=(N_EXPERTS, first + 1),
        in_specs=[slot_spec(f, c) for f, c in blocks] + [wspec, wspec,
                  pl.BlockSpec((1, EXPERT_FF, D_MODEL), lambda e, j: (e, 0, 0))],
        out_specs=pl.BlockSpec((1, tc, D_MODEL),
                               lambda e, j: (e, jnp.where(j == 0, first, j - 1), 0)),
        scratch_shapes=[pltpu.VMEM((D_MODEL, EXPERT_FF), BF16), pltpu.VMEM((D_MODEL, EXPERT_FF), BF16),
                        pltpu.VMEM((EXPERT_FF, D_MODEL), BF16)],
        compiler_params=_cparams(("parallel", "arbitrary")),
        name="ffn",
    )(*xes, wg, wu, wd)


def _comb_copy(ye_ref, buf, sems, slot, e, row):
    return pltpu.make_async_copy(ye_ref.at[e, pl.ds(row, CR_COMB)],
                                 buf.at[slot, pl.ds(e * CR_COMB, CR_COMB)], sems.at[slot])


def _comb_kernel(starts_s, x1_ref, pos_ref, split_ref, spread_ref, mod_ref, fg_ref, ye_ref, y_ref,
                 buf, xbuf, acc_ref, sems, xsem, *, row0, nt, tt):
    t = pl.program_id(0)
    slot = lax.rem(t, 2)

    def issue(tile, sl):
        for e in range(N_EXPERTS):
            b0 = pl.multiple_of(starts_s[tile, e] & -BF16_ROWS, BF16_ROWS)
            _comb_copy(ye_ref, buf, sems, sl, e, row0 + b0).start()

    def wait_all(sl):
        pltpu.make_async_copy(ye_ref.at[0, pl.ds(0, N_EXPERTS * CR_COMB)], buf.at[sl],
                              sems.at[sl]).wait()

    @pl.when(t == 0)
    def _():
        issue(0, 0)

    issue(jnp.minimum(t + 1, nt - 1), 1 - slot)

    posm = pos_ref[0]
    bases = [starts_s[t, e] & -BF16_ROWS for e in range(N_EXPERTS)]
    slot1 = jnp.dot(split_ref[0], spread_ref[...], preferred_element_type=F32)
    col = lax.broadcasted_iota(I32, (1, N_EXPERTS * CR_COMB), 1)
    want = (col & (CR_COMB - 1)) + 1
    for e in range(N_EXPERTS):
        want = want + jnp.where(col // CR_COMB == e, bases[e], 0)
    onehot = jnp.where(slot1 == want.astype(F32), 1.0, 0.0).astype(BF16)

    wait_all(slot)
    acc_ref[...] = jnp.dot(onehot, buf[slot], preferred_element_type=F32)

    n_win = [(starts_s[t + 1, e] - bases[e] + (CR_COMB - 1)) // CR_COMB
             for e in range(N_EXPERTS)]
    most = functools.reduce(jnp.maximum, n_win)

    @pl.when(most > 1)
    def _():
        win_iota = lax.broadcasted_iota(I32, (tt, CR_COMB), 1).astype(F32)
        for e in range(N_EXPERTS):
            def extra(c, _, e=e):
                cb = pl.multiple_of(bases[e] + c * CR_COMB, BF16_ROWS)
                cp = pltpu.make_async_copy(ye_ref.at[e, pl.ds(row0 + cb, CR_COMB)], xbuf,
                                           xsem.at[0])
                cp.start()
                cp.wait()
                rel = posm[:, e:e + 1] - cb.astype(F32)
                oh = jnp.where(rel == win_iota, 1.0, 0.0).astype(BF16)
                acc_ref[...] += jnp.dot(oh, xbuf[...], preferred_element_type=F32)
                return 0

            lax.fori_loop(1, n_win[e], extra, 0)

    g2 = mod_ref[0, 5:6, :]
    y_ref[...] = _rms(x1_ref[...] + g2 * acc_ref[...]) * fg_ref[...]

    @pl.when(t == nt - 1)
    def _():
        wait_all(1 - slot)


def _slot_spread():
    w = np.zeros((2 * LANES, N_EXPERTS * CR_COMB), np.float32)
    for e in range(N_EXPERTS):
        w[e, e * CR_COMB:(e + 1) * CR_COMB] = SLOT_RADIX
        w[LANES + e, e * CR_COMB:(e + 1) * CR_COMB] = 1.0
    return jnp.asarray(w, dtype=BF16)


def _comb(starts_s, x1_2d, pos, split, mod, fg, ye, row0, seq):
    n_tok = x1_2d.shape[0]
    tt = TT_MOE
    nt = n_tok // tt
    tiles_per_seq = seq // tt
    grid_spec = pltpu.PrefetchScalarGridSpec(
        num_scalar_prefetch=1,
        grid=(nt,),
        in_specs=[pl.BlockSpec((tt, D_MODEL), lambda t, s: (t, 0)),
                  pl.BlockSpec((1, tt, LANES), lambda t, s: (t, 0, 0)),
                  pl.BlockSpec((1, tt, 2 * LANES), lambda t, s: (t, 0, 0)),
                  pl.BlockSpec((2 * LANES, N_EXPERTS * CR_COMB), lambda t, s: (0, 0)),
                  pl.BlockSpec((1, N_MOD, D_MODEL), lambda t, s: (t // tiles_per_seq, 0, 0)),
                  pl.BlockSpec((1, D_MODEL), lambda t, s: (0, 0)),
                  pl.BlockSpec(memory_space=pl.ANY)],
        out_specs=pl.BlockSpec((tt, D_MODEL), lambda t, s: (t, 0)),
        scratch_shapes=[pltpu.VMEM((2, N_EXPERTS * CR_COMB, D_MODEL), BF16),
                        pltpu.VMEM((CR_COMB, D_MODEL), BF16),
                        pltpu.VMEM((tt, D_MODEL), F32),
                        pltpu.SemaphoreType.DMA((2,)),
                        pltpu.SemaphoreType.DMA((1,))])
    return pl.pallas_call(
        functools.partial(_comb_kernel, row0=row0, nt=nt, tt=tt),
        out_shape=jax.ShapeDtypeStruct((n_tok, D_MODEL), F32),
        grid_spec=grid_spec,
        compiler_params=_cparams(("arbitrary",)),
        name="comb",
    )(starts_s, x1_2d, pos, split, _slot_spread(), mod, fg, ye)


def _head_tile_cols(base, nope):
    r = base + nope
    return (list(range(base, base + nope)) + list(range(r, r + QK_ROPE))
            + list(range(r, r + HALF_ROPE)))


def _prep_weights(w_in, w_uq, w_ukv):
    zcol = lambda w, n: jnp.zeros((w.shape[0], n), w.dtype)
    kpe0 = Q_LORA + KV_LORA
    kpe_tile = jnp.concatenate(
        [zcol(w_in, QK_NOPE), w_in[:, kpe0:kpe0 + QK_ROPE], w_in[:, kpe0:kpe0 + HALF_ROPE],
         zcol(w_in, HEAD_PAD - QK_NOPE - QK_ROPE - HALF_ROPE)], axis=1)
    win = jnp.concatenate([w_in[:, :kpe0], kpe_tile, w_in[:, kpe0 + QK_ROPE:]], axis=1)

    q_tiles, k_tiles, v_cols = [], [], []
    for hd in range(N_HEADS):
        cols = np.asarray(_head_tile_cols(hd * (QK_NOPE + QK_ROPE), QK_NOPE))
        q_tiles += [w_uq[:, cols], zcol(w_uq, HEAD_PAD - len(cols))]
        kb = hd * (QK_NOPE + V_DIM)
        k_tiles += [w_ukv[:, kb:kb + QK_NOPE], zcol(w_ukv, HEAD_PAD - QK_NOPE)]
        w_v = w_ukv[:, kb + QK_NOPE:kb + QK_NOPE + V_DIM]
        pad = zcol(w_ukv, HEAD_PAD - V_DIM)
        v_cols += [w_v, pad] if hd % 2 == 0 else [pad, w_v]
    wq = jnp.concatenate(q_tiles, axis=1)
    wkv = jnp.concatenate(k_tiles + v_cols, axis=1)
    lane = np.arange(N_HEADS * HEAD_PAD)
    ones_here = ((lane % HEAD_PAD) >= V_DIM) == ((lane // HEAD_PAD) % 2 == 0)
    vone = jnp.asarray(ones_here.astype(np.float32))[None, :]
    return win.astype(BF16), wq.astype(BF16), wkv.astype(BF16), vone


def _rope_tabs(seq):
    inv = ROPE_THETA ** (-jnp.arange(0, QK_ROPE, 2, dtype=F32) / QK_ROPE)
    ang = jnp.arange(seq, dtype=F32)[:, None] * inv[None, :]
    cos, sin = jnp.cos(ang), jnp.sin(ang)
    one = jnp.ones((seq, QK_NOPE), F32)
    z_nope = jnp.zeros((seq, QK_NOPE), F32)
    z_pad = jnp.zeros((seq, HEAD_PAD - QK_NOPE - QK_ROPE), F32)
    scale = (QK_NOPE + QK_ROPE) ** -0.5 * float(np.log2(np.e))
    c_full = jnp.concatenate([one, cos, cos, z_pad], axis=1)
    s_full = jnp.concatenate([z_nope, -sin, sin, z_pad], axis=1)
    c_rope = jnp.concatenate([z_nope, cos, cos, z_pad], axis=1)
    return c_full * scale, s_full * scale, c_rope, s_full


def _tri(tt):
    r = np.arange(tt)
    return jnp.asarray((r[None, :] < r[:, None]).astype(np.float32), dtype=BF16)


def kernel(x_prompt, x_sample, c_prompt, c_sample, w_ada, b_ada, w_in, q_norm_g, kv_norm_g,
           w_uq, w_ukv, w_pool, pool_scale, w_out, w_router, w_gate, w_up, w_down, final_norm_g):
    groups = ((x_prompt, c_prompt), (x_sample, c_sample))
    n_c = sum(c.shape[0] for _, c in groups)
    c_all = jnp.concatenate([c for _, c in groups]
                            + [jnp.zeros((-n_c % 8, D_MODEL), F32)], axis=0)
    mod_all = _ada(c_all, w_ada[0], b_ada)

    win, wq, wkv, vone = _prep_weights(w_in[0], w_uq[0], w_ukv[0])
    qg = q_norm_g[0][None, :]
    kvg = kv_norm_g[0][None, :]
    wp = w_pool[0].astype(BF16)
    ps = pool_scale[0][None, :]
    woa = w_out[0][:D_ATT].astype(BF16)
    wob = w_out[0][D_ATT:].astype(BF16)
    wr_pad = jnp.concatenate([w_router[0], jnp.zeros((D_MODEL, LANES - N_EXPERTS), F32)], axis=1)
    wr_hi = wr_pad.astype(BF16)
    wr = jnp.concatenate([wr_hi, (wr_pad - wr_hi.astype(F32)).astype(BF16)], axis=1)
    wg, wu, wd = w_gate[0], w_up[0], w_down[0]
    fg = final_norm_g[None, :]
    tri = _tri(TT_MOE)

    caps = [CAPACITY_FACTOR * x.shape[0] * x.shape[1] // N_EXPERTS for x, _ in groups]
    assert all(c % TC_FFN == 0 for c in caps)
    xes = []

    staged = []
    c_off = 0
    row0 = 0
    for (x, c), cap in zip(groups, caps):
        b, s, _ = x.shape
        mod = mod_all[c_off:c_off + b].reshape(b, N_MOD, D_MODEL)
        c_off += b
        q, k, v, u = _pre(x, mod, win, qg, kvg, wq, wkv, vone, _rope_tabs(s))
        o = _attn(q, k, v)
        x1, hx, aff, afft = _mix(x, o, u, mod, wp, ps, woa, wob, wr)
        n_tok = b * s
        starts_v, pos, post, split = _plan(afft, aff.reshape(n_tok, LANES), tri, cap)
        starts_s = starts_v[:, 0, :N_EXPERTS]
        xes.append(_disp(starts_s, hx.reshape(n_tok, X_WIDTH), post, cap))
        staged.append((x1.reshape(n_tok, D_MODEL), pos, split, starts_s, mod, row0, (b, s)))
        row0 += cap

    ye = _ffn(xes, caps, wg, wu, wd)

    outs = []
    for x1, pos, split, starts_s, mod, r0, (b, s) in staged:
        y = _comb(starts_s, x1, pos, split, mod, fg, ye, r0, s)
        outs.append(y.reshape(b, s, D_MODEL))
    return tuple(outs)
```

```python
import functools

import jax
import jax.numpy as jnp
import numpy as np
from jax import lax
from jax.experimental import pallas as pl
from jax.experimental.pallas import tpu as pltpu

F32 = jnp.float32
BF16 = jnp.bfloat16
I32 = jnp.int32

D_MODEL = 1024
N_HEADS = 8
QK_NOPE = 64
QK_ROPE = 32
V_DIM = 64
Q_LORA = 384
KV_LORA = 256
ROPE_THETA = 10000.0
D_ATT = N_HEADS * V_DIM
D_POOL = 512
POOL_WINDOWS = (2, 4, 8, 16)
POOL_GROUP = 128
N_EXPERTS = 16
CAPACITY_FACTOR = 2
EXPERT_FF = 1024
N_MOD = 6
EPS = 1e-6

LANES = 128
HEAD_PAD = LANES
HALF_ROPE = QK_ROPE // 2
D_IN_PAD = Q_LORA + KV_LORA + LANES + D_POOL
X_WIDTH = D_MODEL + LANES
POOL_HALO = 8
BF16_ROWS = 16

TS_PRE = 512
ROWS_ATT = 512
ROW_CHAINS = 2
TK_ATT = 8192
SCORE_TILE = 512 * 4096
UNROLL_ATT = 2
TT_MOE = 256
CH_DISP = 64
CR_COMB = 64
SLOT_RADIX = 64.0
TC_FFN = 1024
PLAN_UNROLL = 4
MIN_NORMAL = float(np.finfo(np.float32).tiny)
GEO_STEPS = 32
BISECT_STEPS = GEO_STEPS + 6
VMEM_LIMIT = 56 * 1024 * 1024


def _attn_tiles(seq):
    tk = min(seq, TK_ATT)
    rows = min(ROWS_ATT, SCORE_TILE // tk)
    return ROW_CHAINS * rows, rows, tk


def _cparams(sem):
    return pltpu.CompilerParams(dimension_semantics=sem, vmem_limit_bytes=VMEM_LIMIT)


def _rms(x):
    return x * lax.rsqrt(jnp.mean(x * x, axis=-1, keepdims=True) + EPS)


def _ada_kernel(c_ref, w_ref, b_ref, o_ref):
    c = c_ref[...]
    s = c * jax.nn.sigmoid(c)
    o_ref[...] = jnp.dot(s, w_ref[...], preferred_element_type=F32,
                         precision=lax.Precision.HIGHEST) + b_ref[...]


def _ada(c_pad, w_ada, b_ada):
    rows = c_pad.shape[0]
    return pl.pallas_call(
        _ada_kernel,
        out_shape=jax.ShapeDtypeStruct((rows, N_MOD * D_MODEL), F32),
        grid=(N_MOD,),
        in_specs=[pl.BlockSpec((rows, D_MODEL), lambda j: (0, 0)),
                  pl.BlockSpec((D_MODEL, D_MODEL), lambda j: (0, j)),
                  pl.BlockSpec((1, D_MODEL), lambda j: (0, j))],
        out_specs=pl.BlockSpec((rows, D_MODEL), lambda j: (0, j)),
        compiler_params=_cparams(("arbitrary",)),
        name="ada",
    )(c_pad, w_ada, b_ada)


def _pre_kernel(x_ref, mod_ref, win_ref, qg_ref, kvg_ref, wq_ref, wkv_ref,
                cq_ref, sq_ref, ck_ref, sk_ref, vone_ref, q_ref, k_ref, v_ref, u_ref):
    x = x_ref[0]
    sh1 = mod_ref[0, 0:1, :]
    sc1 = mod_ref[0, 1:2, :]
    h = (_rms(x) * (1.0 + sc1) + sh1).astype(BF16)
    z = jnp.dot(h, win_ref[...], preferred_element_type=F32)
    c_q = z[:, :Q_LORA]
    c_kv = z[:, Q_LORA:Q_LORA + KV_LORA]
    kpe = z[:, Q_LORA + KV_LORA:Q_LORA + KV_LORA + LANES]
    u_ref[0] = z[:, Q_LORA + KV_LORA + LANES:]

    q = jnp.dot((_rms(c_q) * qg_ref[...]).astype(BF16), wq_ref[...],
                preferred_element_type=F32)
    cq_t = cq_ref[...]
    sq_t = sq_ref[...]
    for hd in range(N_HEADS):
        blk = q[:, hd * HEAD_PAD:(hd + 1) * HEAD_PAD]
        rot = pltpu.roll(blk, HEAD_PAD - HALF_ROPE, 1)
        q_ref[0, :, hd * HEAD_PAD:(hd + 1) * HEAD_PAD] = (blk * cq_t + rot * sq_t).astype(BF16)

    kv = jnp.dot((_rms(c_kv) * kvg_ref[...]).astype(BF16), wkv_ref[...],
                 preferred_element_type=F32)
    kpe_r = kpe * ck_ref[...] + pltpu.roll(kpe, HEAD_PAD - HALF_ROPE, 1) * sk_ref[...]
    k_full = jnp.concatenate([kv[:, hd * HEAD_PAD:(hd + 1) * HEAD_PAD] + kpe_r
                              for hd in range(N_HEADS)], axis=-1)
    k_ref[0, 0] = k_full.T.astype(BF16)
    v_ref[0] = (kv[:, N_HEADS * HEAD_PAD:] + vone_ref[...]).astype(BF16)


def _pre(x, mod, win, qg, kvg, wq, wkv, vone, tabs):
    b, s, _ = x.shape
    ts = TS_PRE
    _, _, tk = _attn_tiles(s)
    per_chunk = tk // ts
    assert tk % ts == 0 and s % tk == 0
    const = lambda shape: pl.BlockSpec(shape, lambda bi, i: (0,) * len(shape))
    tab = pl.BlockSpec((ts, LANES), lambda bi, i: (i, 0))
    tok = lambda w: pl.BlockSpec((1, ts, w), lambda bi, i: (bi, i, 0))
    return pl.pallas_call(
        _pre_kernel,
        out_shape=(jax.ShapeDtypeStruct((b, s, N_HEADS * HEAD_PAD), BF16),
                   jax.ShapeDtypeStruct((b, s // tk, N_HEADS * HEAD_PAD, tk), BF16),
                   jax.ShapeDtypeStruct((b, s, N_HEADS * HEAD_PAD), BF16),
                   jax.ShapeDtypeStruct((b, s, D_POOL), F32)),
        grid=(b, s // ts),
        in_specs=[tok(D_MODEL),
                  pl.BlockSpec((1, N_MOD, D_MODEL), lambda bi, i: (bi, 0, 0)),
                  const((D_MODEL, D_IN_PAD)), const((1, Q_LORA)), const((1, KV_LORA)),
                  const((Q_LORA, N_HEADS * HEAD_PAD)),
                  const((KV_LORA, 2 * N_HEADS * HEAD_PAD)),
                  tab, tab, tab, tab, const((1, N_HEADS * HEAD_PAD))],
        out_specs=(tok(N_HEADS * HEAD_PAD),
                   pl.BlockSpec((1, 1, N_HEADS * HEAD_PAD, ts),
                                lambda bi, i: (bi, i // per_chunk, 0, i % per_chunk)),
                   tok(N_HEADS * HEAD_PAD), tok(D_POOL)),
        compiler_params=_cparams(("parallel", "arbitrary")),
        name="pre",
    )(x, mod, win, qg, kvg, wq, wkv, *tabs, vone)


def _attn_kernel(q_ref, k_ref, v_ref, o_ref, *, seq, tq, tk, rows, unroll):
    chains = [(r, hh) for r in range(tq // rows) for hh in range(2)]

    def body(c, carry):
        off = pl.multiple_of(c * tk, tk)
        new = []
        for (r, hh), (m, acc) in zip(chains, carry):
            cols = slice(hh * HEAD_PAD, (hh + 1) * HEAD_PAD)
            qh = q_ref[0, r * rows:(r + 1) * rows, cols]
            s = jnp.dot(qh, k_ref[0, c, cols, :], preferred_element_type=F32)
            m_new = jnp.maximum(m, jnp.max(s, axis=-1, keepdims=True))
            p = jnp.exp2((s - m_new).astype(BF16))
            acc = jnp.exp2(m - m_new) * acc + jnp.dot(p, v_ref[0, pl.ds(off, tk), cols],
                                                      preferred_element_type=F32)
            new.append((m_new, acc))
        return tuple(new)

    init = tuple((jnp.full((rows, 1), -jnp.inf, F32), jnp.zeros((rows, HEAD_PAD), F32))
                 for _ in chains)
    final = lax.fori_loop(0, seq // tk, body, init, unroll=unroll)
    lane = lax.broadcasted_iota(I32, (rows, HEAD_PAD), 1)
    for r in range(tq // rows):
        a0, a1 = final[2 * r][1], final[2 * r + 1][1]
        o_ref[0, r * rows:(r + 1) * rows, :] = jnp.where(
            lane < V_DIM, a0 / a0[:, V_DIM:V_DIM + 1], a1 / a1[:, 0:1]).astype(BF16)


def _attn(q, k, v):
    b, s, _ = q.shape
    tq, rows, tk = _attn_tiles(s)
    assert s % tq == 0 and s % tk == 0
    return pl.pallas_call(
        functools.partial(_attn_kernel, seq=s, tq=tq, tk=tk, rows=rows, unroll=UNROLL_ATT),
        out_shape=jax.ShapeDtypeStruct((b, s, D_ATT), BF16),
        grid=(b, N_HEADS // 2, s // tq),
        in_specs=[pl.BlockSpec((1, tq, 2 * HEAD_PAD), lambda bi, p, i: (bi, i, p)),
                  pl.BlockSpec((1, s // tk, 2 * HEAD_PAD, tk), lambda bi, p, i: (bi, 0, p, 0)),
                  pl.BlockSpec((1, s, 2 * HEAD_PAD), lambda bi, p, i: (bi, 0, p))],
        out_specs=pl.BlockSpec((1, tq, 2 * V_DIM), lambda bi, p, i: (bi, i, p)),
        compiler_params=_cparams(("parallel", "parallel", "arbitrary")),
        name="attn",
    )(q, k, v)


def _mix_kernel(x_ref, o_ref, u_ref, up_ref, un_ref, mod_ref, wp_ref, ps_ref, woa_ref, wob_ref,
                wr_ref, icnt_ref, x1_ref, hx_ref, aff_ref, afft_ref, ue_ref, *, ts):
    i = pl.program_id(1)
    nt = pl.num_programs(1)
    hb = POOL_HALO
    ue_ref[0:hb, :] = jnp.where(i > 0, up_ref[0], 0.0)
    ue_ref[hb:hb + ts, :] = u_ref[0]
    ue_ref[hb + ts:2 * hb + ts, :] = jnp.where(i < nt - 1, un_ref[0], 0.0)

    ys = []
    for g, w in enumerate(POOL_WINDOWS):
        cols = slice(g * POOL_GROUP, (g + 1) * POOL_GROUP)
        acc = None
        for off in range(-(w // 2), w - w // 2):
            sl = ue_ref[hb + off:hb + off + ts, cols]
            acc = sl if acc is None else acc + sl
        d = (acc * icnt_ref[:, g:g + 1] - ue_ref[hb:hb + ts, cols]).astype(BF16)
        y = jnp.dot(d, wp_ref[g], preferred_element_type=F32) * ps_ref[:, cols]
        ys.append(y.astype(BF16))
    pool = jnp.concatenate(ys, axis=-1)
    mixed = (jnp.dot(o_ref[0], woa_ref[...], preferred_element_type=F32)
             + jnp.dot(pool, wob_ref[...], preferred_element_type=F32))
    g1 = mod_ref[0, 2:3, :]
    sh2 = mod_ref[0, 3:4, :]
    sc2 = mod_ref[0, 4:5, :]
    x1 = x_ref[0] + g1 * mixed
    x1_ref[0] = x1
    h2 = _rms(x1) * (1.0 + sc2) + sh2
    hx_ref[0, :, 0:D_MODEL] = h2.astype(BF16)

    h_hi = h2.astype(BF16)
    h_lo = (h2 - h_hi.astype(F32)).astype(BF16)
    both = jnp.dot(h_hi, wr_ref[...], preferred_element_type=F32)
    logits = (both[:, :LANES] + both[:, LANES:]
              + jnp.dot(h_lo, wr_ref[:, :LANES], preferred_element_type=F32))
    lane = lax.broadcasted_iota(I32, (ts, LANES), 1)
    logits = jnp.where(lane < N_EXPERTS, logits, -jnp.inf)
    ex = jnp.exp(logits - jnp.max(logits, axis=-1, keepdims=True))
    aff = ex / jnp.sum(ex, axis=-1, keepdims=True)
    aff_ref[0] = aff
    afft_ref[0] = aff.T[0:N_EXPERTS, :]
    hi_part = aff.astype(BF16).astype(F32)
    lo_part = pltpu.roll(aff - hi_part, N_EXPERTS, 1)
    hx_ref[0, :, D_MODEL:] = jnp.where(lane < N_EXPERTS, hi_part, lo_part).astype(BF16)


def _pool_inv_counts(seq):
    t = np.arange(seq)[:, None]
    w = np.asarray(POOL_WINDOWS)[None, :]
    cnt = np.minimum(t + (w - w // 2), seq) - np.maximum(t - w // 2, 0)
    tab = np.zeros((seq, LANES), np.float64)
    tab[:, :len(POOL_WINDOWS)] = 1.0 / cnt
    return jnp.asarray(tab, dtype=F32)


def _mix(x, o, u, mod, wp, ps, woa, wob, wr):
    b, s, _ = x.shape
    ts = TS_PRE
    hpt = ts // POOL_HALO
    nh = s // POOL_HALO
    const = lambda shape: pl.BlockSpec(shape, lambda bi, i: (0,) * len(shape))
    tok = lambda w: pl.BlockSpec((1, ts, w), lambda bi, i: (bi, i, 0))
    return pl.pallas_call(
        functools.partial(_mix_kernel, ts=ts),
        out_shape=(jax.ShapeDtypeStruct((b, s, D_MODEL), F32),
                   jax.ShapeDtypeStruct((b, s, X_WIDTH), BF16),
                   jax.ShapeDtypeStruct((b, s, LANES), F32),
                   jax.ShapeDtypeStruct((b, N_EXPERTS, s), F32)),
        grid=(b, s // ts),
        in_specs=[tok(D_MODEL), tok(D_ATT), tok(D_POOL),
                  pl.BlockSpec((1, POOL_HALO, D_POOL),
                               lambda bi, i: (bi, jnp.maximum(i * hpt - 1, 0), 0)),
                  pl.BlockSpec((1, POOL_HALO, D_POOL),
                               lambda bi, i: (bi, jnp.minimum((i + 1) * hpt, nh - 1), 0)),
                  pl.BlockSpec((1, N_MOD, D_MODEL), lambda bi, i: (bi, 0, 0)),
                  const((len(POOL_WINDOWS), POOL_GROUP, POOL_GROUP)), const((1, D_POOL)),
                  const((D_ATT, D_MODEL)), const((D_POOL, D_MODEL)), const((D_MODEL, 2 * LANES)),
                  pl.BlockSpec((ts, LANES), lambda bi, i: (i, 0))],
        out_specs=(tok(D_MODEL), tok(X_WIDTH), tok(LANES),
                   pl.BlockSpec((1, N_EXPERTS, ts), lambda bi, i: (bi, 0, i))),
        scratch_shapes=[pltpu.VMEM((ts + 2 * POOL_HALO, D_POOL), F32)],
        compiler_params=_cparams(("parallel", "arbitrary")),
        name="mix",
    )(x, o, u, u, u, mod, wp, ps, woa, wob, wr, _pool_inv_counts(s))


def _plan_kernel(afft_ref, aff_ref, tri_ref, starts_ref, pos_ref, post_ref, split_ref, *, nb, cap,
                 tt):
    def count_ge(cand):
        def body(bi, acc):
            return acc + jnp.sum(jnp.where(afft_ref[bi] >= cand, 1.0, 0.0), axis=1, keepdims=True)
        return lax.fori_loop(0, nb, body, jnp.zeros((N_EXPERTS, 1), F32))

    def shrink(step, bounds):
        lo, hi = bounds
        mid = jnp.where(step < GEO_STEPS, jnp.sqrt(lo * hi), lo + 0.5 * (hi - lo))
        mid = jnp.clip(mid, lo, hi)
        enough = count_ge(mid) >= cap
        return jnp.where(enough, mid, lo), jnp.where(enough, hi, mid)

    lo_col, hi_col = lax.fori_loop(0, BISECT_STEPS, shrink,
                                   (jnp.full((N_EXPERTS, 1), MIN_NORMAL, F32),
                                    jnp.full((N_EXPERTS, 1), 2.0, F32)))
    tiny = count_ge(jnp.full((N_EXPERTS, 1), MIN_NORMAL, F32)) < cap
    lo_col = jnp.where(tiny, -1.0, lo_col)
    hi_col = jnp.where(tiny, MIN_NORMAL, hi_col)
    need_col = cap - count_ge(hi_col)

    sub = lax.broadcasted_iota(I32, (N_EXPERTS, LANES), 0)
    lane = lax.broadcasted_iota(I32, (N_EXPERTS, LANES), 1)
    real = lane[0:1, :] < N_EXPERTS

    def to_lanes(col, pad):
        row = jnp.sum(jnp.where(sub == lane, col, 0.0), axis=0, keepdims=True)
        return jnp.where(real, row, pad)

    lo = to_lanes(lo_col, 3.0)
    hi = to_lanes(hi_col, 3.0)
    need = to_lanes(need_col, 0.0)

    def tile_body(t, carry):
        sel_before, eq_before = carry
        off = pl.multiple_of(t * tt, tt)
        v = aff_ref[pl.ds(off, tt), :]
        eq = (v >= lo) & (v < hi)
        eq_f = jnp.where(eq, 1.0, 0.0)
        tri = tri_ref[...]
        rank = jnp.dot(tri, eq_f.astype(BF16), preferred_element_type=F32) + eq_before
        sel = (v >= hi) | (eq & (rank < need))
        sel_f = jnp.where(sel, 1.0, 0.0)
        pos = jnp.dot(tri, sel_f.astype(BF16), preferred_element_type=F32) + sel_before
        posm = jnp.where(sel, pos, -1.0)
        starts_ref[t] = jnp.broadcast_to(sel_before.astype(I32), (8, LANES))
        pos_ref[t] = posm
        post_ref[t] = posm.T[0:N_EXPERTS, :]
        slot1 = posm + 1.0
        high = jnp.floor(slot1 * (1.0 / SLOT_RADIX))
        split_ref[t] = jnp.concatenate([high, slot1 - SLOT_RADIX * high], axis=-1).astype(BF16)
        return (sel_before + jnp.sum(sel_f, axis=0, keepdims=True),
                eq_before + jnp.sum(eq_f, axis=0, keepdims=True))

    nt = pos_ref.shape[0]
    zero = jnp.zeros((1, LANES), F32)
    sel_total, _ = lax.fori_loop(0, nt, tile_body, (zero, zero), unroll=PLAN_UNROLL)
    starts_ref[nt] = jnp.broadcast_to(sel_total.astype(I32), (8, LANES))


def _plan(afft, aff2d, tri, cap):
    nb = afft.shape[0]
    n_tok = aff2d.shape[0]
    tt = TT_MOE
    nt = n_tok // tt
    vm = pl.BlockSpec(memory_space=pltpu.VMEM)
    return pl.pallas_call(
        functools.partial(_plan_kernel, nb=nb, cap=cap, tt=tt),
        out_shape=(jax.ShapeDtypeStruct((nt + 1, 8, LANES), I32),
                   jax.ShapeDtypeStruct((nt, tt, LANES), F32),
                   jax.ShapeDtypeStruct((nt, N_EXPERTS, tt), F32),
                   jax.ShapeDtypeStruct((nt, tt, 2 * LANES), BF16)),
        in_specs=[vm, vm, vm],
        out_specs=(vm, vm, vm, vm),
        compiler_params=pltpu.CompilerParams(vmem_limit_bytes=VMEM_LIMIT),
        name="plan",
    )(afft, aff2d, tri)


def _disp_copy(stage, xe_ref, sems, e, row, src_e=None):
    src_e = e if src_e is None else src_e
    return pltpu.make_async_copy(stage.at[pl.ds(src_e * CH_DISP, CH_DISP)],
                                 xe_ref.at[e, pl.ds(row, CH_DISP)], sems.at[e])


def _disp_kernel(starts_s, hx_ref, post_ref, xe_ref, stages, carry, sems, *, cap, nt, tt):
    t = pl.program_id(0)
    stage = stages.at[lax.rem(t, 2)]

    @pl.when(t == 0)
    def _():
        carry[...] = jnp.zeros(carry.shape, carry.dtype)
        idle = stages.at[1]
        idle[0:CH_DISP, :] = jnp.zeros((CH_DISP, X_WIDTH), BF16)
        for e in range(N_EXPERTS):
            _disp_copy(idle, xe_ref, sems, e, cap, src_e=0).start()

    pos_t = post_ref[0]
    slot_iota = lax.broadcasted_iota(I32, (CH_DISP, tt), 0).astype(F32)

    def window(e, base):
        onehot = jnp.where(pos_t[e:e + 1, :] - base.astype(F32) == slot_iota, 1.0, 0.0)
        return onehot.astype(BF16)

    def carry_rows(e, base):
        off = (starts_s[t + 1, e] & -BF16_ROWS) - base
        inside = jnp.minimum(off, CH_DISP - BF16_ROWS)
        blk = stage[pl.ds(pl.multiple_of(e * CH_DISP + inside, BF16_ROWS), BF16_ROWS), :]
        return jnp.where(off < CH_DISP, blk, jnp.zeros_like(blk))

    bases = [starts_s[t, e] & -BF16_ROWS for e in range(N_EXPERTS)]
    onehot = jnp.concatenate([window(e, bases[e]) for e in range(N_EXPERTS)], axis=0)
    stage[...] = jnp.dot(onehot, hx_ref[...], preferred_element_type=F32).astype(BF16)
    for e in range(N_EXPERTS):
        head = pl.ds(e * CH_DISP, BF16_ROWS)
        stage[head, :] = (stage[head, :].astype(F32) + carry[e].astype(F32)).astype(BF16)
    for e in range(N_EXPERTS):
        _disp_copy(stage, xe_ref, sems, e, 0).wait()
        _disp_copy(stage, xe_ref, sems, e, pl.multiple_of(bases[e], BF16_ROWS)).start()
        carry[e] = carry_rows(e, bases[e])

    n_win = [(starts_s[t + 1, e] - bases[e] + (CH_DISP - 1)) // CH_DISP
             for e in range(N_EXPERTS)]
    most = functools.reduce(jnp.maximum, n_win)

    @pl.when(most > 1)
    def _():
        for e in range(N_EXPERTS):
            def more(c, _, e=e):
                cb = pl.multiple_of(bases[e] + c * CH_DISP, BF16_ROWS)
                extra = jnp.dot(window(e, cb), hx_ref[...], preferred_element_type=F32)
                _disp_copy(stage, xe_ref, sems, e, 0).wait()
                stage[pl.ds(e * CH_DISP, CH_DISP), :] = extra.astype(BF16)
                _disp_copy(stage, xe_ref, sems, e, cb).start()
                return 0

            lax.fori_loop(1, n_win[e], more, 0)

            @pl.when(n_win[e] > 1)
            def _(e=e):
                carry[e] = carry_rows(e, bases[e] + (n_win[e] - 1) * CH_DISP)

    @pl.when(t == nt - 1)
    def _():
        for e in range(N_EXPERTS):
            _disp_copy(stage, xe_ref, sems, e, 0).wait()


def _disp(starts_s, hx2d, post, cap):
    n_tok = hx2d.shape[0]
    tt = TT_MOE
    nt = n_tok // tt
    grid_spec = pltpu.PrefetchScalarGridSpec(
        num_scalar_prefetch=1,
        grid=(nt,),
        in_specs=[pl.BlockSpec((tt, X_WIDTH), lambda t, s: (t, 0)),
                  pl.BlockSpec((1, N_EXPERTS, tt), lambda t, s: (t, 0, 0))],
        out_specs=pl.BlockSpec(memory_space=pl.ANY),
        scratch_shapes=[pltpu.VMEM((2, N_EXPERTS * CH_DISP, X_WIDTH), BF16),
                        pltpu.VMEM((N_EXPERTS, BF16_ROWS, X_WIDTH), BF16),
                        pltpu.SemaphoreType.DMA((N_EXPERTS,))])
    return pl.pallas_call(
        functools.partial(_disp_kernel, cap=cap, nt=nt, tt=tt),
        out_shape=jax.ShapeDtypeStruct((N_EXPERTS, cap + CH_DISP, X_WIDTH), BF16),
        grid_spec=grid_spec,
        compiler_params=_cparams(("arbitrary",)),
        name="disp",
    )(starts_s, hx2d, post)


def _ffn_kernel(*refs, blocks):
    n_grp = len(blocks)
    xe_refs = refs[:n_grp]
    wg_ref, wu_ref, wd_ref, ye_ref, wg_bf, wu_bf, wd_bf = refs[n_grp:]
    e = pl.program_id(0)
    j = pl.program_id(1)

    @pl.when(j == 0)
    def _():
        wg_bf[...] = wg_ref[0].astype(BF16)
        wu_bf[...] = wu_ref[0].astype(BF16)
        wd_bf[...] = wd_ref[0].astype(BF16)
        ye_ref[...] = jnp.zeros(ye_ref.shape, ye_ref.dtype)

    def experts(xe_ref):
        blk = xe_ref[0]
        x = blk[:, 0:D_MODEL]
        gcols = blk[:, D_MODEL:].astype(F32)
        lane = lax.broadcasted_iota(I32, gcols.shape, 1)
        gate = jnp.sum(jnp.where((lane == e) | (lane == e + N_EXPERTS), gcols, 0.0),
                       axis=-1, keepdims=True)
        a = jnp.dot(x, wg_bf[...], preferred_element_type=F32)
        u = jnp.dot(x, wu_bf[...], preferred_element_type=F32)
        hmid = (a * jax.nn.sigmoid(a) * u).astype(BF16)
        y = jnp.dot(hmid, wd_bf[...], preferred_element_type=F32) * gate
        ye_ref[0] = y.astype(BF16)

    for g, (first, count) in enumerate(blocks):
        pl.when((j > first) & (j <= first + count))(functools.partial(experts, xe_refs[g]))


def _ffn(xes, caps, wg, wu, wd):
    tc = TC_FFN
    blocks, first = [], 0
    for cap in caps:
        blocks.append((first, cap // tc))
        first += cap // tc
    wspec = pl.BlockSpec((1, D_MODEL, EXPERT_FF), lambda e, j: (e, 0, 0))

    def slot_spec(first, count):
        return pl.BlockSpec((1, tc, X_WIDTH),
                            lambda e, j: (e, jnp.clip(j - 1 - first, 0, count - 1), 0))

    return pl.pallas_call(
        functools.partial(_ffn_kernel, blocks=tuple(blocks)),
        out_shape=jax.ShapeDtypeStruct((N_EXPERTS, (first + 1) * tc, D_MODEL), BF16),
        grid=(N_EXPERTS, first + 1),
        in_specs=[slot_spec(f, c) for f, c in blocks] + [wspec, wspec,
                  pl.BlockSpec((1, EXPERT_FF, D_MODEL), lambda e, j: (e, 0, 0))],
        out_specs=pl.BlockSpec((1, tc, D_MODEL),
                               lambda e, j: (e, jnp.where(j == 0, first, j - 1), 0)),
        scratch_shapes=[pltpu.VMEM((D_MODEL, EXPERT_FF), BF16), pltpu.VMEM((D_MODEL, EXPERT_FF), BF16),
                        pltpu.VMEM((EXPERT_FF, D_MODEL), BF16)],
        compiler_params=_cparams(("parallel", "arbitrary")),
        name="ffn",
    )(*xes, wg, wu, wd)


def _comb_copy(ye_ref, buf, sems, slot, e, row):
    return pltpu.make_async_copy(ye_ref.at[e, pl.ds(row, CR_COMB)],
                                 buf.at[slot, pl.ds(e * CR_COMB, CR_COMB)], sems.at[slot])


def _comb_kernel(starts_s, x1_ref, pos_ref, split_ref, spread_ref, mod_ref, fg_ref, ye_ref, y_ref,
                 buf, xbuf, acc_ref, sems, xsem, *, row0, nt, tt):
    t = pl.program_id(0)
    slot = lax.rem(t, 2)

    def issue(tile, sl):
        for e in range(N_EXPERTS):
            b0 = pl.multiple_of(starts_s[tile, e] & -BF16_ROWS, BF16_ROWS)
            _comb_copy(ye_ref, buf, sems, sl, e, row0 + b0).start()

    def wait_all(sl):
        pltpu.make_async_copy(ye_ref.at[0, pl.ds(0, N_EXPERTS * CR_COMB)], buf.at[sl],
                              sems.at[sl]).wait()

    @pl.when(t == 0)
    def _():
        issue(0, 0)

    issue(jnp.minimum(t + 1, nt - 1), 1 - slot)

    posm = pos_ref[0]
    bases = [starts_s[t, e] & -BF16_ROWS for e in range(N_EXPERTS)]
    slot1 = jnp.dot(split_ref[0], spread_ref[...], preferred_element_type=F32)
    col = lax.broadcasted_iota(I32, (1, N_EXPERTS * CR_COMB), 1)
    want = (col & (CR_COMB - 1)) + 1
    for e in range(N_EXPERTS):
        want = want + jnp.where(col // CR_COMB == e, bases[e], 0)
    onehot = jnp.where(slot1 == want.astype(F32), 1.0, 0.0).astype(BF16)

    wait_all(slot)
    acc_ref[...] = jnp.dot(onehot, buf[slot], preferred_element_type=F32)

    n_win = [(starts_s[t + 1, e] - bases[e] + (CR_COMB - 1)) // CR_COMB
             for e in range(N_EXPERTS)]
    most = functools.reduce(jnp.maximum, n_win)

    @pl.when(most > 1)
    def _():
        win_iota = lax.broadcasted_iota(I32, (tt, CR_COMB), 1).astype(F32)
        for e in range(N_EXPERTS):
            def extra(c, _, e=e):
                cb = pl.multiple_of(bases[e] + c * CR_COMB, BF16_ROWS)
                cp = pltpu.make_async_copy(ye_ref.at[e, pl.ds(row0 + cb, CR_COMB)], xbuf,
                                           xsem.at[0])
                cp.start()
                cp.wait()
                rel = posm[:, e:e + 1] - cb.astype(F32)
                oh = jnp.where(rel == win_iota, 1.0, 0.0).astype(BF16)
                acc_ref[...] += jnp.dot(oh, xbuf[...], preferred_element_type=F32)
                return 0

            lax.fori_loop(1, n_win[e], extra, 0)

    g2 = mod_ref[0, 5:6, :]
    y_ref[...] = _rms(x1_ref[...] + g2 * acc_ref[...]) * fg_ref[...]

    @pl.when(t == nt - 1)
    def _():
        wait_all(1 - slot)


def _slot_spread():
    w = np.zeros((2 * LANES, N_EXPERTS * CR_COMB), np.float32)
    for e in range(N_EXPERTS):
        w[e, e * CR_COMB:(e + 1) * CR_COMB] = SLOT_RADIX
        w[LANES + e, e * CR_COMB:(e + 1) * CR_COMB] = 1.0
    return jnp.asarray(w, dtype=BF16)


def _comb(starts_s, x1_2d, pos, split, mod, fg, ye, row0, seq):
    n_tok = x1_2d.shape[0]
    tt = TT_MOE
    nt = n_tok // tt
    tiles_per_seq = seq // tt
    grid_spec = pltpu.PrefetchScalarGridSpec(
        num_scalar_prefetch=1,
        grid=(nt,),
        in_specs=[pl.BlockSpec((tt, D_MODEL), lambda t, s: (t, 0)),
                  pl.BlockSpec((1, tt, LANES), lambda t, s: (t, 0, 0)),
                  pl.BlockSpec((1, tt, 2 * LANES), lambda t, s: (t, 0, 0)),
                  pl.BlockSpec((2 * LANES, N_EXPERTS * CR_COMB), lambda t, s: (0, 0)),
                  pl.BlockSpec((1, N_MOD, D_MODEL), lambda t, s: (t // tiles_per_seq, 0, 0)),
                  pl.BlockSpec((1, D_MODEL), lambda t, s: (0, 0)),
                  pl.BlockSpec(memory_space=pl.ANY)],
        out_specs=pl.BlockSpec((tt, D_MODEL), lambda t, s: (t, 0)),
        scratch_shapes=[pltpu.VMEM((2, N_EXPERTS * CR_COMB, D_MODEL), BF16),
                        pltpu.VMEM((CR_COMB, D_MODEL), BF16),
                        pltpu.VMEM((tt, D_MODEL), F32),
                        pltpu.SemaphoreType.DMA((2,)),
                        pltpu.SemaphoreType.DMA((1,))])
    return pl.pallas_call(
        functools.partial(_comb_kernel, row0=row0, nt=nt, tt=tt),
        out_shape=jax.ShapeDtypeStruct((n_tok, D_MODEL), F32),
        grid_spec=grid_spec,
        compiler_params=_cparams(("arbitrary",)),
        name="comb",
    )(starts_s, x1_2d, pos, split, _slot_spread(), mod, fg, ye)


def _head_tile_cols(base, nope):
    r = base + nope
    return (list(range(base, base + nope)) + list(range(r, r + QK_ROPE))
            + list(range(r, r + HALF_ROPE)))


def _prep_weights(w_in, w_uq, w_ukv):
    zcol = lambda w, n: jnp.zeros((w.shape[0], n), w.dtype)
    kpe0 = Q_LORA + KV_LORA
    kpe_tile = jnp.concatenate(
        [zcol(w_in, QK_NOPE), w_in[:, kpe0:kpe0 + QK_ROPE], w_in[:, kpe0:kpe0 + HALF_ROPE],
         zcol(w_in, HEAD_PAD - QK_NOPE - QK_ROPE - HALF_ROPE)], axis=1)
    win = jnp.concatenate([w_in[:, :kpe0], kpe_tile, w_in[:, kpe0 + QK_ROPE:]], axis=1)

    q_tiles, k_tiles, v_cols = [], [], []
    for hd in range(N_HEADS):
        cols = np.asarray(_head_tile_cols(hd * (QK_NOPE + QK_ROPE), QK_NOPE))
        q_tiles += [w_uq[:, cols], zcol(w_uq, HEAD_PAD - len(cols))]
        kb = hd * (QK_NOPE + V_DIM)
        k_tiles += [w_ukv[:, kb:kb + QK_NOPE], zcol(w_ukv, HEAD_PAD - QK_NOPE)]
        w_v = w_ukv[:, kb + QK_NOPE:kb + QK_NOPE + V_DIM]
        pad = zcol(w_ukv, HEAD_PAD - V_DIM)
        v_cols += [w_v, pad] if hd % 2 == 0 else [pad, w_v]
    wq = jnp.concatenate(q_tiles, axis=1)
    wkv = jnp.concatenate(k_tiles + v_cols, axis=1)
    lane = np.arange(N_HEADS * HEAD_PAD)
    ones_here = ((lane % HEAD_PAD) >= V_DIM) == ((lane // HEAD_PAD) % 2 == 0)
    vone = jnp.asarray(ones_here.astype(np.float32))[None, :]
    return win.astype(BF16), wq.astype(BF16), wkv.astype(BF16), vone


def _rope_tabs(seq):
    inv = ROPE_THETA ** (-jnp.arange(0, QK_ROPE, 2, dtype=F32) / QK_ROPE)
    ang = jnp.arange(seq, dtype=F32)[:, None] * inv[None, :]
    cos, sin = jnp.cos(ang), jnp.sin(ang)
    one = jnp.ones((seq, QK_NOPE), F32)
    z_nope = jnp.zeros((seq, QK_NOPE), F32)
    z_pad = jnp.zeros((seq, HEAD_PAD - QK_NOPE - QK_ROPE), F32)
    scale = (QK_NOPE + QK_ROPE) ** -0.5 * float(np.log2(np.e))
    c_full = jnp.concatenate([one, cos, cos, z_pad], axis=1)
    s_full = jnp.concatenate([z_nope, -sin, sin, z_pad], axis=1)
    c_rope = jnp.concatenate([z_nope, cos, cos, z_pad], axis=1)
    return c_full * scale, s_full * scale, c_rope, s_full


def _tri(tt):
    r = np.arange(tt)
    return jnp.asarray((r[None, :] < r[:, None]).astype(np.float32), dtype=BF16)


def kernel(x_prompt, x_sample, c_prompt, c_sample, w_ada, b_ada, w_in, q_norm_g, kv_norm_g,
           w_uq, w_ukv, w_pool, pool_scale, w_out, w_router, w_gate, w_up, w_down, final_norm_g):
    groups = ((x_prompt, c_prompt), (x_sample, c_sample))
    n_c = sum(c.shape[0] for _, c in groups)
    c_all = jnp.concatenate([c for _, c in groups]
                            + [jnp.zeros((-n_c % 8, D_MODEL), F32)], axis=0)
    mod_all = _ada(c_all, w_ada[0], b_ada)

    win, wq, wkv, vone = _prep_weights(w_in[0], w_uq[0], w_ukv[0])
    qg = q_norm_g[0][None, :]
    kvg = kv_norm_g[0][None, :]
    wp = w_pool[0].astype(BF16)
    ps = pool_scale[0][None, :]
    woa = w_out[0][:D_ATT].astype(BF16)
    wob = w_out[0][D_ATT:].astype(BF16)
    wr_pad = jnp.concatenate([w_router[0], jnp.zeros((D_MODEL, LANES - N_EXPERTS), F32)], axis=1)
    wr_hi = wr_pad.astype(BF16)
    wr = jnp.concatenate([wr_hi, (wr_pad - wr_hi.astype(F32)).astype(BF16)], axis=1)
    wg, wu, wd = w_gate[0], w_up[0], w_down[0]
    fg = final_norm_g[None, :]
    tri = _tri(TT_MOE)

    caps = [CAPACITY_FACTOR * x.shape[0] * x.shape[1] // N_EXPERTS for x, _ in groups]
    assert all(c % TC_FFN == 0 for c in caps)
    xes = []

    staged = []
    c_off = 0
    row0 = 0
    for (x, c), cap in zip(groups, caps):
        b, s, _ = x.shape
        mod = mod_all[c_off:c_off + b].reshape(b, N_MOD, D_MODEL)
        c_off += b
        q, k, v, u = _pre(x, mod, win, qg, kvg, wq, wkv, vone, _rope_tabs(s))
        o = _attn(q, k, v)
        x1, hx, aff, afft = _mix(x, o, u, mod, wp, ps, woa, wob, wr)
        n_tok = b * s
        starts_v, pos, post, split = _plan(afft, aff.reshape(n_tok, LANES), tri, cap)
        starts_s = starts_v[:, 0, :N_EXPERTS]
        xes.append(_disp(starts_s, hx.reshape(n_tok, X_WIDTH), post, cap))
        staged.append((x1.reshape(n_tok, D_MODEL), pos, split, starts_s, mod, row0, (b, s)))
        row0 += cap

    ye = _ffn(xes, caps, wg, wu, wd)

    outs = []
    for x1, pos, split, starts_s, mod, r0, (b, s) in staged:
        y = _comb(starts_s, x1, pos, split, mod, fg, ye, r0, s)
        outs.append(y.reshape(b, s, D_MODEL))
    return tuple(outs)
```

```python
import functools

import jax
import jax.numpy as jnp
import numpy as np
from jax import lax
from jax.experimental import pallas as pl
from jax.experimental.pallas import tpu as pltpu

F32 = jnp.float32
BF16 = jnp.bfloat16
I32 = jnp.int32

D_MODEL = 1024
N_HEADS = 8
QK_NOPE = 64
QK_ROPE = 32
V_DIM = 64
Q_LORA = 384
KV_LORA = 256
ROPE_THETA = 10000.0
D_ATT = N_HEADS * V_DIM
D_POOL = 512
POOL_WINDOWS = (2, 4, 8, 16)
POOL_GROUP = 128
N_EXPERTS = 16
CAPACITY_FACTOR = 2
EXPERT_FF = 1024
N_MOD = 6
EPS = 1e-6

LANES = 128
HEAD_PAD = LANES
HALF_ROPE = QK_ROPE // 2
D_IN_PAD = Q_LORA + KV_LORA + LANES + D_POOL
X_WIDTH = D_MODEL + LANES
POOL_HALO = 8
BF16_ROWS = 16

TS_PRE = 512
ROWS_ATT = 512
ROW_CHAINS = 2
TK_ATT = 8192
SCORE_TILE = 512 * 4096
UNROLL_ATT = 2
TT_MOE = 256
CH_DISP = 64
CR_COMB = 64
SLOT_RADIX = 64.0
TC_FFN = 1024
PLAN_UNROLL = 4
MIN_NORMAL = float(np.finfo(np.float32).tiny)
GEO_STEPS = 32
BISECT_STEPS = GEO_STEPS + 6
VMEM_LIMIT = 56 * 1024 * 1024


def _attn_tiles(seq):
    tk = min(seq, TK_ATT)
    rows = min(ROWS_ATT, SCORE_TILE // tk)
    return ROW_CHAINS * rows, rows, tk


def _cparams(sem):
    return pltpu.CompilerParams(dimension_semantics=sem, vmem_limit_bytes=VMEM_LIMIT)


def _rms(x):
    return x * lax.rsqrt(jnp.mean(x * x, axis=-1, keepdims=True) + EPS)


def _ada_kernel(c_ref, w_ref, b_ref, o_ref):
    c = c_ref[...]
    s = c * jax.nn.sigmoid(c)
    o_ref[...] = jnp.dot(s, w_ref[...], preferred_element_type=F32,
                         precision=lax.Precision.HIGHEST) + b_ref[...]


def _ada(c_pad, w_ada, b_ada):
    rows = c_pad.shape[0]
    return pl.pallas_call(
        _ada_kernel,
        out_shape=jax.ShapeDtypeStruct((rows, N_MOD * D_MODEL), F32),
        grid=(N_MOD,),
        in_specs=[pl.BlockSpec((rows, D_MODEL), lambda j: (0, 0)),
                  pl.BlockSpec((D_MODEL, D_MODEL), lambda j: (0, j)),
                  pl.BlockSpec((1, D_MODEL), lambda j: (0, j))],
        out_specs=pl.BlockSpec((rows, D_MODEL), lambda j: (0, j)),
        compiler_params=_cparams(("arbitrary",)),
        name="ada",
    )(c_pad, w_ada, b_ada)


def _pre_kernel(x_ref, mod_ref, win_ref, qg_ref, kvg_ref, wq_ref, wkv_ref,
                cq_ref, sq_ref, ck_ref, sk_ref, vone_ref, q_ref, k_ref, v_ref, u_ref):
    x = x_ref[0]
    sh1 = mod_ref[0, 0:1, :]
    sc1 = mod_ref[0, 1:2, :]
    h = (_rms(x) * (1.0 + sc1) + sh1).astype(BF16)
    z = jnp.dot(h, win_ref[...], preferred_element_type=F32)
    c_q = z[:, :Q_LORA]
    c_kv = z[:, Q_LORA:Q_LORA + KV_LORA]
    kpe = z[:, Q_LORA + KV_LORA:Q_LORA + KV_LORA + LANES]
    u_ref[0] = z[:, Q_LORA + KV_LORA + LANES:]

    q = jnp.dot((_rms(c_q) * qg_ref[...]).astype(BF16), wq_ref[...],
                preferred_element_type=F32)
    cq_t = cq_ref[...]
    sq_t = sq_ref[...]
    for hd in range(N_HEADS):
        blk = q[:, hd * HEAD_PAD:(hd + 1) * HEAD_PAD]
        rot = pltpu.roll(blk, HEAD_PAD - HALF_ROPE, 1)
        q_ref[0, :, hd * HEAD_PAD:(hd + 1) * HEAD_PAD] = (blk * cq_t + rot * sq_t).astype(BF16)

    kv = jnp.dot((_rms(c_kv) * kvg_ref[...]).astype(BF16), wkv_ref[...],
                 preferred_element_type=F32)
    kpe_r = kpe * ck_ref[...] + pltpu.roll(kpe, HEAD_PAD - HALF_ROPE, 1) * sk_ref[...]
    k_full = jnp.concatenate([kv[:, hd * HEAD_PAD:(hd + 1) * HEAD_PAD] + kpe_r
                              for hd in range(N_HEADS)], axis=-1)
    k_ref[0, 0] = k_full.T.astype(BF16)
    v_ref[0] = (kv[:, N_HEADS * HEAD_PAD:] + vone_ref[...]).astype(BF16)


def _pre(x, mod, win, qg, kvg, wq, wkv, vone, tabs):
    b, s, _ = x.shape
    ts = TS_PRE
    _, _, tk = _attn_tiles(s)
    per_chunk = tk // ts
    assert tk % ts == 0 and s % tk == 0
    const = lambda shape: pl.BlockSpec(shape, lambda bi, i: (0,) * len(shape))
    tab = pl.BlockSpec((ts, LANES), lambda bi, i: (i, 0))
    tok = lambda w: pl.BlockSpec((1, ts, w), lambda bi, i: (bi, i, 0))
    return pl.pallas_call(
        _pre_kernel,
        out_shape=(jax.ShapeDtypeStruct((b, s, N_HEADS * HEAD_PAD), BF16),
                   jax.ShapeDtypeStruct((b, s // tk, N_HEADS * HEAD_PAD, tk), BF16),
                   jax.ShapeDtypeStruct((b, s, N_HEADS * HEAD_PAD), BF16),
                   jax.ShapeDtypeStruct((b, s, D_POOL), F32)),
        grid=(b, s // ts),
        in_specs=[tok(D_MODEL),
                  pl.BlockSpec((1, N_MOD, D_MODEL), lambda bi, i: (bi, 0, 0)),
                  const((D_MODEL, D_IN_PAD)), const((1, Q_LORA)), const((1, KV_LORA)),
                  const((Q_LORA, N_HEADS * HEAD_PAD)),
                  const((KV_LORA, 2 * N_HEADS * HEAD_PAD)),
                  tab, tab, tab, tab, const((1, N_HEADS * HEAD_PAD))],
        out_specs=(tok(N_HEADS * HEAD_PAD),
                   pl.BlockSpec((1, 1, N_HEADS * HEAD_PAD, ts),
                                lambda bi, i: (bi, i // per_chunk, 0, i % per_chunk)),
                   tok(N_HEADS * HEAD_PAD), tok(D_POOL)),
        compiler_params=_cparams(("parallel", "arbitrary")),
        name="pre",
    )(x, mod, win, qg, kvg, wq, wkv, *tabs, vone)


def _attn_kernel(q_ref, k_ref, v_ref, o_ref, *, seq, tq, tk, rows, unroll):
    chains = [(r, hh) for r in range(tq // rows) for hh in range(2)]

    def body(c, carry):
        off = pl.multiple_of(c * tk, tk)
        new = []
        for (r, hh), (m, acc) in zip(chains, carry):
            cols = slice(hh * HEAD_PAD, (hh + 1) * HEAD_PAD)
            qh = q_ref[0, r * rows:(r + 1) * rows, cols]
            s = jnp.dot(qh, k_ref[0, c, cols, :], preferred_element_type=F32)
            m_new = jnp.maximum(m, jnp.max(s, axis=-1, keepdims=True))
            p = jnp.exp2((s - m_new).astype(BF16))
            acc = jnp.exp2(m - m_new) * acc + jnp.dot(p, v_ref[0, pl.ds(off, tk), cols],
                                                      preferred_element_type=F32)
            new.append((m_new, acc))
        return tuple(new)

    init = tuple((jnp.full((rows, 1), -jnp.inf, F32), jnp.zeros((rows, HEAD_PAD), F32))
                 for _ in chains)
    final = lax.fori_loop(0, seq // tk, body, init, unroll=unroll)
    lane = lax.broadcasted_iota(I32, (rows, HEAD_PAD), 1)
    for r in range(tq // rows):
        a0, a1 = final[2 * r][1], final[2 * r + 1][1]
        o_ref[0, r * rows:(r + 1) * rows, :] = jnp.where(
            lane < V_DIM, a0 / a0[:, V_DIM:V_DIM + 1], a1 / a1[:, 0:1]).astype(BF16)


def _attn(q, k, v):
    b, s, _ = q.shape
    tq, rows, tk = _attn_tiles(s)
    assert s % tq == 0 and s % tk == 0
    return pl.pallas_call(
        functools.partial(_attn_kernel, seq=s, tq=tq, tk=tk, rows=rows, unroll=UNROLL_ATT),
        out_shape=jax.ShapeDtypeStruct((b, s, D_ATT), BF16),
        grid=(b, N_HEADS // 2, s // tq),
        in_specs=[pl.BlockSpec((1, tq, 2 * HEAD_PAD), lambda bi, p, i: (bi, i, p)),
                  pl.BlockSpec((1, s // tk, 2 * HEAD_PAD, tk), lambda bi, p, i: (bi, 0, p, 0)),
                  pl.BlockSpec((1, s, 2 * HEAD_PAD), lambda bi, p, i: (bi, 0, p))],
        out_specs=pl.BlockSpec((1, tq, 2 * V_DIM), lambda bi, p, i: (bi, i, p)),
        compiler_params=_cparams(("parallel", "parallel", "arbitrary")),
        name="attn",
    )(q, k, v)


def _mix_kernel(x_ref, o_ref, u_ref, up_ref, un_ref, mod_ref, wp_ref, ps_ref, woa_ref, wob_ref,
                wr_ref, icnt_ref, x1_ref, hx_ref, aff_ref, afft_ref, ue_ref, *, ts):
    i = pl.program_id(1)
    nt = pl.num_programs(1)
    hb = POOL_HALO
    ue_ref[0:hb, :] = jnp.where(i > 0, up_ref[0], 0.0)
    ue_ref[hb:hb + ts, :] = u_ref[0]
    ue_ref[hb + ts:2 * hb + ts, :] = jnp.where(i < nt - 1, un_ref[0], 0.0)

    ys = []
    for g, w in enumerate(POOL_WINDOWS):
        cols = slice(g * POOL_GROUP, (g + 1) * POOL_GROUP)
        acc = None
        for off in range(-(w // 2), w - w // 2):
            sl = ue_ref[hb + off:hb + off + ts, cols]
            acc = sl if acc is None else acc + sl
        d = (acc * icnt_ref[:, g:g + 1] - ue_ref[hb:hb + ts, cols]).astype(BF16)
        y = jnp.dot(d, wp_ref[g], preferred_element_type=F32) * ps_ref[:, cols]
        ys.append(y.astype(BF16))
    pool = jnp.concatenate(ys, axis=-1)
    mixed = (jnp.dot(o_ref[0], woa_ref[...], preferred_element_type=F32)
             + jnp.dot(pool, wob_ref[...], preferred_element_type=F32))
    g1 = mod_ref[0, 2:3, :]
    sh2 = mod_ref[0, 3:4, :]
    sc2 = mod_ref[0, 4:5, :]
    x1 = x_ref[0] + g1 * mixed
    x1_ref[0] = x1
    h2 = _rms(x1) * (1.0 + sc2) + sh2
    hx_ref[0, :, 0:D_MODEL] = h2.astype(BF16)

    h_hi = h2.astype(BF16)
    h_lo = (h2 - h_hi.astype(F32)).astype(BF16)
    both = jnp.dot(h_hi, wr_ref[...], preferred_element_type=F32)
    logits = (both[:, :LANES] + both[:, LANES:]
              + jnp.dot(h_lo, wr_ref[:, :LANES], preferred_element_type=F32))
    lane = lax.broadcasted_iota(I32, (ts, LANES), 1)
    logits = jnp.where(lane < N_EXPERTS, logits, -jnp.inf)
    ex = jnp.exp(logits - jnp.max(logits, axis=-1, keepdims=True))
    aff = ex / jnp.sum(ex, axis=-1, keepdims=True)
    aff_ref[0] = aff
    afft_ref[0] = aff.T[0:N_EXPERTS, :]
    hi_part = aff.astype(BF16).astype(F32)
    lo_part = pltpu.roll(aff - hi_part, N_EXPERTS, 1)
    hx_ref[0, :, D_MODEL:] = jnp.where(lane < N_EXPERTS, hi_part, lo_part).astype(BF16)


def _pool_inv_counts(seq):
    t = np.arange(seq)[:, None]
    w = np.asarray(POOL_WINDOWS)[None, :]
    cnt = np.minimum(t + (w - w // 2), seq) - np.maximum(t - w // 2, 0)
    tab = np.zeros((seq, LANES), np.float64)
    tab[:, :len(POOL_WINDOWS)] = 1.0 / cnt
    return jnp.asarray(tab, dtype=F32)


def _mix(x, o, u, mod, wp, ps, woa, wob, wr):
    b, s, _ = x.shape
    ts = TS_PRE
    hpt = ts // POOL_HALO
    nh = s // POOL_HALO
    const = lambda shape: pl.BlockSpec(shape, lambda bi, i: (0,) * len(shape))
    tok = lambda w: pl.BlockSpec((1, ts, w), lambda bi, i: (bi, i, 0))
    return pl.pallas_call(
        functools.partial(_mix_kernel, ts=ts),
        out_shape=(jax.ShapeDtypeStruct((b, s, D_MODEL), F32),
                   jax.ShapeDtypeStruct((b, s, X_WIDTH), BF16),
                   jax.ShapeDtypeStruct((b, s, LANES), F32),
                   jax.ShapeDtypeStruct((b, N_EXPERTS, s), F32)),
        grid=(b, s // ts),
        in_specs=[tok(D_MODEL), tok(D_ATT), tok(D_POOL),
                  pl.BlockSpec((1, POOL_HALO, D_POOL),
                               lambda bi, i: (bi, jnp.maximum(i * hpt - 1, 0), 0)),
                  pl.BlockSpec((1, POOL_HALO, D_POOL),
                               lambda bi, i: (bi, jnp.minimum((i + 1) * hpt, nh - 1), 0)),
                  pl.BlockSpec((1, N_MOD, D_MODEL), lambda bi, i: (bi, 0, 0)),
                  const((len(POOL_WINDOWS), POOL_GROUP, POOL_GROUP)), const((1, D_POOL)),
                  const((D_ATT, D_MODEL)), const((D_POOL, D_MODEL)), const((D_MODEL, 2 * LANES)),
                  pl.BlockSpec((ts, LANES), lambda bi, i: (i, 0))],
        out_specs=(tok(D_MODEL), tok(X_WIDTH), tok(LANES),
                   pl.BlockSpec((1, N_EXPERTS, ts), lambda bi, i: (bi, 0, i))),
        scratch_shapes=[pltpu.VMEM((ts + 2 * POOL_HALO, D_POOL), F32)],
        compiler_params=_cparams(("parallel", "arbitrary")),
        name="mix",
    )(x, o, u, u, u, mod, wp, ps, woa, wob, wr, _pool_inv_counts(s))


def _plan_kernel(afft_ref, aff_ref, tri_ref, starts_ref, pos_ref, post_ref, split_ref, *, nb, cap,
                 tt):
    def count_ge(cand):
        seq = afft_ref.shape[2]

        def body(bi, acc):
            hit = jnp.where(afft_ref[bi] >= cand, 1.0, 0.0)
            for c0 in range(0, seq, LANES):
                acc = acc + hit[:, c0:c0 + LANES]
            return acc

        lanes = lax.fori_loop(0, nb, body, jnp.zeros((N_EXPERTS, LANES), F32))
        return jnp.sum(lanes, axis=1, keepdims=True)

    def shrink(step, bounds):
        lo, hi = bounds
        mid = jnp.where(step < GEO_STEPS, jnp.sqrt(lo * hi), lo + 0.5 * (hi - lo))
        mid = jnp.clip(mid, lo, hi)
        enough = count_ge(mid) >= cap
        return jnp.where(enough, mid, lo), jnp.where(enough, hi, mid)

    lo_col, hi_col = lax.fori_loop(0, BISECT_STEPS, shrink,
                                   (jnp.full((N_EXPERTS, 1), MIN_NORMAL, F32),
                                    jnp.full((N_EXPERTS, 1), 2.0, F32)))
    tiny = count_ge(jnp.full((N_EXPERTS, 1), MIN_NORMAL, F32)) < cap
    lo_col = jnp.where(tiny, -1.0, lo_col)
    hi_col = jnp.where(tiny, MIN_NORMAL, hi_col)
    need_col = cap - count_ge(hi_col)

    sub = lax.broadcasted_iota(I32, (N_EXPERTS, LANES), 0)
    lane = lax.broadcasted_iota(I32, (N_EXPERTS, LANES), 1)
    real = lane[0:1, :] < N_EXPERTS

    def to_lanes(col, pad):
        row = jnp.sum(jnp.where(sub == lane, col, 0.0), axis=0, keepdims=True)
        return jnp.where(real, row, pad)

    lo = to_lanes(lo_col, 3.0)
    hi = to_lanes(hi_col, 3.0)
    need = to_lanes(need_col, 0.0)

    def tile_body(t, carry):
        sel_before, eq_before = carry
        off = pl.multiple_of(t * tt, tt)
        v = aff_ref[pl.ds(off, tt), :]
        eq = (v >= lo) & (v < hi)
        eq_f = jnp.where(eq, 1.0, 0.0)
        tri = tri_ref[...]
        rank = jnp.dot(tri, eq_f.astype(BF16), preferred_element_type=F32) + eq_before
        sel = (v >= hi) | (eq & (rank < need))
        sel_f = jnp.where(sel, 1.0, 0.0)
        pos = jnp.dot(tri, sel_f.astype(BF16), preferred_element_type=F32) + sel_before
        posm = jnp.where(sel, pos, -1.0)
        starts_ref[t] = jnp.broadcast_to(sel_before.astype(I32), (8, LANES))
        pos_ref[t] = posm
        post_ref[t] = posm.T[0:N_EXPERTS, :]
        slot1 = posm + 1.0
        high = jnp.floor(slot1 * (1.0 / SLOT_RADIX))
        split_ref[t] = jnp.concatenate([high, slot1 - SLOT_RADIX * high], axis=-1).astype(BF16)
        return (sel_before + jnp.sum(sel_f, axis=0, keepdims=True),
                eq_before + jnp.sum(eq_f, axis=0, keepdims=True))

    nt = pos_ref.shape[0]
    zero = jnp.zeros((1, LANES), F32)
    sel_total, _ = lax.fori_loop(0, nt, tile_body, (zero, zero), unroll=PLAN_UNROLL)
    starts_ref[nt] = jnp.broadcast_to(sel_total.astype(I32), (8, LANES))


def _plan(afft, aff2d, tri, cap):
    nb = afft.shape[0]
    n_tok = aff2d.shape[0]
    tt = TT_MOE
    nt = n_tok // tt
    vm = pl.BlockSpec(memory_space=pltpu.VMEM)
    return pl.pallas_call(
        functools.partial(_plan_kernel, nb=nb, cap=cap, tt=tt),
        out_shape=(jax.ShapeDtypeStruct((nt + 1, 8, LANES), I32),
                   jax.ShapeDtypeStruct((nt, tt, LANES), F32),
                   jax.ShapeDtypeStruct((nt, N_EXPERTS, tt), F32),
                   jax.ShapeDtypeStruct((nt, tt, 2 * LANES), BF16)),
        in_specs=[vm, vm, vm],
        out_specs=(vm, vm, vm, vm),
        compiler_params=pltpu.CompilerParams(vmem_limit_bytes=VMEM_LIMIT),
        name="plan",
    )(afft, aff2d, tri)


def _disp_copy(stage, xe_ref, sems, e, row, src_e=None):
    src_e = e if src_e is None else src_e
    return pltpu.make_async_copy(stage.at[pl.ds(src_e * CH_DISP, CH_DISP)],
                                 xe_ref.at[e, pl.ds(row, CH_DISP)], sems.at[e])


def _disp_kernel(starts_s, hx_ref, post_ref, xe_ref, stages, carry, sems, *, cap, nt, tt):
    t = pl.program_id(0)
    stage = stages.at[lax.rem(t, 2)]

    @pl.when(t == 0)
    def _():
        carry[...] = jnp.zeros(carry.shape, carry.dtype)
        idle = stages.at[1]
        idle[0:CH_DISP, :] = jnp.zeros((CH_DISP, X_WIDTH), BF16)
        for e in range(N_EXPERTS):
            _disp_copy(idle, xe_ref, sems, e, cap, src_e=0).start()

    pos_t = post_ref[0]
    slot_iota = lax.broadcasted_iota(I32, (CH_DISP, tt), 0).astype(F32)

    def window(e, base):
        onehot = jnp.where(pos_t[e:e + 1, :] - base.astype(F32) == slot_iota, 1.0, 0.0)
        return onehot.astype(BF16)

    def carry_rows(e, base):
        off = (starts_s[t + 1, e] & -BF16_ROWS) - base
        inside = jnp.minimum(off, CH_DISP - BF16_ROWS)
        blk = stage[pl.ds(pl.multiple_of(e * CH_DISP + inside, BF16_ROWS), BF16_ROWS), :]
        return jnp.where(off < CH_DISP, blk, jnp.zeros_like(blk))

    bases = [starts_s[t, e] & -BF16_ROWS for e in range(N_EXPERTS)]
    onehot = jnp.concatenate([window(e, bases[e]) for e in range(N_EXPERTS)], axis=0)
    stage[...] = jnp.dot(onehot, hx_ref[...], preferred_element_type=F32).astype(BF16)
    for e in range(N_EXPERTS):
        head = pl.ds(e * CH_DISP, BF16_ROWS)
        stage[head, :] = (stage[head, :].astype(F32) + carry[e].astype(F32)).astype(BF16)
    for e in range(N_EXPERTS):
        _disp_copy(stage, xe_ref, sems, e, 0).wait()
        _disp_copy(stage, xe_ref, sems, e, pl.multiple_of(bases[e], BF16_ROWS)).start()
        carry[e] = carry_rows(e, bases[e])

    n_win = [(starts_s[t + 1, e] - bases[e] + (CH_DISP - 1)) // CH_DISP
             for e in range(N_EXPERTS)]
    most = functools.reduce(jnp.maximum, n_win)

    @pl.when(most > 1)
    def _():
        for e in range(N_EXPERTS):
            def more(c, _, e=e):
                cb = pl.multiple_of(bases[e] + c * CH_DISP, BF16_ROWS)
                extra = jnp.dot(window(e, cb), hx_ref[...], preferred_element_type=F32)
                _disp_copy(stage, xe_ref, sems, e, 0).wait()
                stage[pl.ds(e * CH_DISP, CH_DISP), :] = extra.astype(BF16)
                _disp_copy(stage, xe_ref, sems, e, cb).start()
                return 0

            lax.fori_loop(1, n_win[e], more, 0)

            @pl.when(n_win[e] > 1)
            def _(e=e):
                carry[e] = carry_rows(e, bases[e] + (n_win[e] - 1) * CH_DISP)

    @pl.when(t == nt - 1)
    def _():
        for e in range(N_EXPERTS):
            _disp_copy(stage, xe_ref, sems, e, 0).wait()


def _disp(starts_s, hx2d, post, cap):
    n_tok = hx2d.shape[0]
    tt = TT_MOE
    nt = n_tok // tt
    grid_spec = pltpu.PrefetchScalarGridSpec(
        num_scalar_prefetch=1,
        grid=(nt,),
        in_specs=[pl.BlockSpec((tt, X_WIDTH), lambda t, s: (t, 0)),
                  pl.BlockSpec((1, N_EXPERTS, tt), lambda t, s: (t, 0, 0))],
        out_specs=pl.BlockSpec(memory_space=pl.ANY),
        scratch_shapes=[pltpu.VMEM((2, N_EXPERTS * CH_DISP, X_WIDTH), BF16),
                        pltpu.VMEM((N_EXPERTS, BF16_ROWS, X_WIDTH), BF16),
                        pltpu.SemaphoreType.DMA((N_EXPERTS,))])
    return pl.pallas_call(
        functools.partial(_disp_kernel, cap=cap, nt=nt, tt=tt),
        out_shape=jax.ShapeDtypeStruct((N_EXPERTS, cap + CH_DISP, X_WIDTH), BF16),
        grid_spec=grid_spec,
        compiler_params=_cparams(("arbitrary",)),
        name="disp",
    )(starts_s, hx2d, post)


def _ffn_kernel(*refs, blocks):
    n_grp = len(blocks)
    xe_refs = refs[:n_grp]
    wg_ref, wu_ref, wd_ref, ye_ref, wg_bf, wu_bf, wd_bf = refs[n_grp:]
    e = pl.program_id(0)
    j = pl.program_id(1)

    @pl.when(j == 0)
    def _():
        wg_bf[...] = wg_ref[0].astype(BF16)
        wu_bf[...] = wu_ref[0].astype(BF16)
        wd_bf[...] = wd_ref[0].astype(BF16)
        ye_ref[...] = jnp.zeros(ye_ref.shape, ye_ref.dtype)

    def experts(xe_ref):
        blk = xe_ref[0]
        x = blk[:, 0:D_MODEL]
        gcols = blk[:, D_MODEL:].astype(F32)
        lane = lax.broadcasted_iota(I32, gcols.shape, 1)
        gate = jnp.sum(jnp.where((lane == e) | (lane == e + N_EXPERTS), gcols, 0.0),
                       axis=-1, keepdims=True)
        a = jnp.dot(x, wg_bf[...], preferred_element_type=F32)
        u = jnp.dot(x, wu_bf[...], preferred_element_type=F32)
        hmid = (a * jax.nn.sigmoid(a) * u).astype(BF16)
        y = jnp.dot(hmid, wd_bf[...], preferred_element_type=F32) * gate
        ye_ref[0] = y.astype(BF16)

    for g, (first, count) in enumerate(blocks):
        pl.when((j > first) & (j <= first + count))(functools.partial(experts, xe_refs[g]))


def _ffn(xes, caps, wg, wu, wd):
    tc = TC_FFN
    blocks, first = [], 0
    for cap in caps:
        blocks.append((first, cap // tc))
        first += cap // tc
    wspec = pl.BlockSpec((1, D_MODEL, EXPERT_FF), lambda e, j: (e, 0, 0))

    def slot_spec(first, count):
        return pl.BlockSpec((1, tc, X_WIDTH),
                            lambda e, j: (e, jnp.clip(j - 1 - first, 0, count - 1), 0))

    return pl.pallas_call(
        functools.partial(_ffn_kernel, blocks=tuple(blocks)),
        out_shape=jax.ShapeDtypeStruct((N_EXPERTS, (first + 1) * tc, D_MODEL), BF16),
        grid=(N_EXPERTS, first + 1),
        in_specs=[slot_spec(f, c) for f, c in blocks] + [wspec, wspec,
                  pl.BlockSpec((1, EXPERT_FF, D_MODEL), lambda e, j: (e, 0, 0))],
        out_specs=pl.BlockSpec((1, tc, D_MODEL),
                               lambda e, j: (e, jnp.where(j == 0, first, j - 1), 0)),
        scratch_shapes=[pltpu.VMEM((D_MODEL, EXPERT_FF), BF16), pltpu.VMEM((D_MODEL, EXPERT_FF), BF16),
                        pltpu.VMEM((EXPERT_FF, D_MODEL), BF16)],
        compiler_params=_cparams(("parallel", "arbitrary")),
        name="ffn",
    )(*xes, wg, wu, wd)


def _comb_copy(ye_ref, buf, sems, slot, e, row):
    return pltpu.make_async_copy(ye_ref.at[e, pl.ds(row, CR_COMB)],
                                 buf.at[slot, pl.ds(e * CR_COMB, CR_COMB)], sems.at[slot])


def _comb_kernel(starts_s, x1_ref, pos_ref, split_ref, spread_ref, mod_ref, fg_ref, ye_ref, y_ref,
                 buf, xbuf, acc_ref, sems, xsem, *, row0, nt, tt):
    t = pl.program_id(0)
    slot = lax.rem(t, 2)

    def issue(tile, sl):
        for e in range(N_EXPERTS):
            b0 = pl.multiple_of(starts_s[tile, e] & -BF16_ROWS, BF16_ROWS)
            _comb_copy(ye_ref, buf, sems, sl, e, row0 + b0).start()

    def wait_all(sl):
        pltpu.make_async_copy(ye_ref.at[0, pl.ds(0, N_EXPERTS * CR_COMB)], buf.at[sl],
                              sems.at[sl]).wait()

    @pl.when(t == 0)
    def _():
        issue(0, 0)

    issue(jnp.minimum(t + 1, nt - 1), 1 - slot)

    posm = pos_ref[0]
    bases = [starts_s[t, e] & -BF16_ROWS for e in range(N_EXPERTS)]
    slot1 = jnp.dot(split_ref[0], spread_ref[...], preferred_element_type=F32)
    col = lax.broadcasted_iota(I32, (1, N_EXPERTS * CR_COMB), 1)
    want = (col & (CR_COMB - 1)) + 1
    for e in range(N_EXPERTS):
        want = want + jnp.where(col // CR_COMB == e, bases[e], 0)
    onehot = jnp.where(slot1 == want.astype(F32), 1.0, 0.0).astype(BF16)

    wait_all(slot)
    acc_ref[...] = jnp.dot(onehot, buf[slot], preferred_element_type=F32)

    n_win = [(starts_s[t + 1, e] - bases[e] + (CR_COMB - 1)) // CR_COMB
             for e in range(N_EXPERTS)]
    most = functools.reduce(jnp.maximum, n_win)

    @pl.when(most > 1)
    def _():
        win_iota = lax.broadcasted_iota(I32, (tt, CR_COMB), 1).astype(F32)
        for e in range(N_EXPERTS):
            def extra(c, _, e=e):
                cb = pl.multiple_of(bases[e] + c * CR_COMB, BF16_ROWS)
                cp = pltpu.make_async_copy(ye_ref.at[e, pl.ds(row0 + cb, CR_COMB)], xbuf,
                                           xsem.at[0])
                cp.start()
                cp.wait()
                rel = posm[:, e:e + 1] - cb.astype(F32)
                oh = jnp.where(rel == win_iota, 1.0, 0.0).astype(BF16)
                acc_ref[...] += jnp.dot(oh, xbuf[...], preferred_element_type=F32)
                return 0

            lax.fori_loop(1, n_win[e], extra, 0)

    g2 = mod_ref[0, 5:6, :]
    y_ref[...] = _rms(x1_ref[...] + g2 * acc_ref[...]) * fg_ref[...]

    @pl.when(t == nt - 1)
    def _():
        wait_all(1 - slot)


def _slot_spread():
    w = np.zeros((2 * LANES, N_EXPERTS * CR_COMB), np.float32)
    for e in range(N_EXPERTS):
        w[e, e * CR_COMB:(e + 1) * CR_COMB] = SLOT_RADIX
        w[LANES + e, e * CR_COMB:(e + 1) * CR_COMB] = 1.0
    return jnp.asarray(w, dtype=BF16)


def _comb(starts_s, x1_2d, pos, split, mod, fg, ye, row0, seq):
    n_tok = x1_2d.shape[0]
    tt = TT_MOE
    nt = n_tok // tt
    tiles_per_seq = seq // tt
    grid_spec = pltpu.PrefetchScalarGridSpec(
        num_scalar_prefetch=1,
        grid=(nt,),
        in_specs=[pl.BlockSpec((tt, D_MODEL), lambda t, s: (t, 0)),
                  pl.BlockSpec((1, tt, LANES), lambda t, s: (t, 0, 0)),
                  pl.BlockSpec((1, tt, 2 * LANES), lambda t, s: (t, 0, 0)),
                  pl.BlockSpec((2 * LANES, N_EXPERTS * CR_COMB), lambda t, s: (0, 0)),
                  pl.BlockSpec((1, N_MOD, D_MODEL), lambda t, s: (t // tiles_per_seq, 0, 0)),
                  pl.BlockSpec((1, D_MODEL), lambda t, s: (0, 0)),
                  pl.BlockSpec(memory_space=pl.ANY)],
        out_specs=pl.BlockSpec((tt, D_MODEL), lambda t, s: (t, 0)),
        scratch_shapes=[pltpu.VMEM((2, N_EXPERTS * CR_COMB, D_MODEL), BF16),
                        pltpu.VMEM((CR_COMB, D_MODEL), BF16),
                        pltpu.VMEM((tt, D_MODEL), F32),
                        pltpu.SemaphoreType.DMA((2,)),
                        pltpu.SemaphoreType.DMA((1,))])
    return pl.pallas_call(
        functools.partial(_comb_kernel, row0=row0, nt=nt, tt=tt),
        out_shape=jax.ShapeDtypeStruct((n_tok, D_MODEL), F32),
        grid_spec=grid_spec,
        compiler_params=_cparams(("arbitrary",)),
        name="comb",
    )(starts_s, x1_2d, pos, split, _slot_spread(), mod, fg, ye)


def _head_tile_cols(base, nope):
    r = base + nope
    return (list(range(base, base + nope)) + list(range(r, r + QK_ROPE))
            + list(range(r, r + HALF_ROPE)))


def _prep_weights(w_in, w_uq, w_ukv):
    zcol = lambda w, n: jnp.zeros((w.shape[0], n), w.dtype)
    kpe0 = Q_LORA + KV_LORA
    kpe_tile = jnp.concatenate(
        [zcol(w_in, QK_NOPE), w_in[:, kpe0:kpe0 + QK_ROPE], w_in[:, kpe0:kpe0 + HALF_ROPE],
         zcol(w_in, HEAD_PAD - QK_NOPE - QK_ROPE - HALF_ROPE)], axis=1)
    win = jnp.concatenate([w_in[:, :kpe0], kpe_tile, w_in[:, kpe0 + QK_ROPE:]], axis=1)

    q_tiles, k_tiles, v_cols = [], [], []
    for hd in range(N_HEADS):
        cols = np.asarray(_head_tile_cols(hd * (QK_NOPE + QK_ROPE), QK_NOPE))
        q_tiles += [w_uq[:, cols], zcol(w_uq, HEAD_PAD - len(cols))]
        kb = hd * (QK_NOPE + V_DIM)
        k_tiles += [w_ukv[:, kb:kb + QK_NOPE], zcol(w_ukv, HEAD_PAD - QK_NOPE)]
        w_v = w_ukv[:, kb + QK_NOPE:kb + QK_NOPE + V_DIM]
        pad = zcol(w_ukv, HEAD_PAD - V_DIM)
        v_cols += [w_v, pad] if hd % 2 == 0 else [pad, w_v]
    wq = jnp.concatenate(q_tiles, axis=1)
    wkv = jnp.concatenate(k_tiles + v_cols, axis=1)
    lane = np.arange(N_HEADS * HEAD_PAD)
    ones_here = ((lane % HEAD_PAD) >= V_DIM) == ((lane // HEAD_PAD) % 2 == 0)
    vone = jnp.asarray(ones_here.astype(np.float32))[None, :]
    return win.astype(BF16), wq.astype(BF16), wkv.astype(BF16), vone


def _rope_tabs(seq):
    inv = ROPE_THETA ** (-jnp.arange(0, QK_ROPE, 2, dtype=F32) / QK_ROPE)
    ang = jnp.arange(seq, dtype=F32)[:, None] * inv[None, :]
    cos, sin = jnp.cos(ang), jnp.sin(ang)
    one = jnp.ones((seq, QK_NOPE), F32)
    z_nope = jnp.zeros((seq, QK_NOPE), F32)
    z_pad = jnp.zeros((seq, HEAD_PAD - QK_NOPE - QK_ROPE), F32)
    scale = (QK_NOPE + QK_ROPE) ** -0.5 * float(np.log2(np.e))
    c_full = jnp.concatenate([one, cos, cos, z_pad], axis=1)
    s_full = jnp.concatenate([z_nope, -sin, sin, z_pad], axis=1)
    c_rope = jnp.concatenate([z_nope, cos, cos, z_pad], axis=1)
    return c_full * scale, s_full * scale, c_rope, s_full


def _tri(tt):
    r = np.arange(tt)
    return jnp.asarray((r[None, :] < r[:, None]).astype(np.float32), dtype=BF16)


def kernel(x_prompt, x_sample, c_prompt, c_sample, w_ada, b_ada, w_in, q_norm_g, kv_norm_g,
           w_uq, w_ukv, w_pool, pool_scale, w_out, w_router, w_gate, w_up, w_down, final_norm_g):
    groups = ((x_prompt, c_prompt), (x_sample, c_sample))
    n_c = sum(c.shape[0] for _, c in groups)
    c_all = jnp.concatenate([c for _, c in groups]
                            + [jnp.zeros((-n_c % 8, D_MODEL), F32)], axis=0)
    mod_all = _ada(c_all, w_ada[0], b_ada)

    win, wq, wkv, vone = _prep_weights(w_in[0], w_uq[0], w_ukv[0])
    qg = q_norm_g[0][None, :]
    kvg = kv_norm_g[0][None, :]
    wp = w_pool[0].astype(BF16)
    ps = pool_scale[0][None, :]
    woa = w_out[0][:D_ATT].astype(BF16)
    wob = w_out[0][D_ATT:].astype(BF16)
    wr_pad = jnp.concatenate([w_router[0], jnp.zeros((D_MODEL, LANES - N_EXPERTS), F32)], axis=1)
    wr_hi = wr_pad.astype(BF16)
    wr = jnp.concatenate([wr_hi, (wr_pad - wr_hi.astype(F32)).astype(BF16)], axis=1)
    wg, wu, wd = w_gate[0], w_up[0], w_down[0]
    fg = final_norm_g[None, :]
    tri = _tri(TT_MOE)

    caps = [CAPACITY_FACTOR * x.shape[0] * x.shape[1] // N_EXPERTS for x, _ in groups]
    assert all(c % TC_FFN == 0 for c in caps)
    xes = []

    staged = []
    c_off = 0
    row0 = 0
    for (x, c), cap in zip(groups, caps):
        b, s, _ = x.shape
        mod = mod_all[c_off:c_off + b].reshape(b, N_MOD, D_MODEL)
        c_off += b
        q, k, v, u = _pre(x, mod, win, qg, kvg, wq, wkv, vone, _rope_tabs(s))
        o = _attn(q, k, v)
        x1, hx, aff, afft = _mix(x, o, u, mod, wp, ps, woa, wob, wr)
        n_tok = b * s
        starts_v, pos, post, split = _plan(afft, aff.reshape(n_tok, LANES), tri, cap)
        starts_s = starts_v[:, 0, :N_EXPERTS]
        xes.append(_disp(starts_s, hx.reshape(n_tok, X_WIDTH), post, cap))
        staged.append((x1.reshape(n_tok, D_MODEL), pos, split, starts_s, mod, row0, (b, s)))
        row0 += cap

    ye = _ffn(xes, caps, wg, wu, wd)

    outs = []
    for x1, pos, split, starts_s, mod, r0, (b, s) in staged:
        y = _comb(starts_s, x1, pos, split, mod, fg, ye, r0, s)
        outs.append(y.reshape(b, s, D_MODEL))
    return tuple(outs)
```

```python
import functools

import jax
import jax.numpy as jnp
import numpy as np
from jax import lax
from jax.experimental import pallas as pl
from jax.experimental.pallas import tpu as pltpu

F32 = jnp.float32
BF16 = jnp.bfloat16
I32 = jnp.int32

D_MODEL = 1024
N_HEADS = 8
QK_NOPE = 64
QK_ROPE = 32
V_DIM = 64
Q_LORA = 384
KV_LORA = 256
ROPE_THETA = 10000.0
D_ATT = N_HEADS * V_DIM
D_POOL = 512
POOL_WINDOWS = (2, 4, 8, 16)
POOL_GROUP = 128
N_EXPERTS = 16
CAPACITY_FACTOR = 2
EXPERT_FF = 1024
N_MOD = 6
EPS = 1e-6

LANES = 128
HEAD_PAD = LANES
HALF_ROPE = QK_ROPE // 2
D_IN_PAD = Q_LORA + KV_LORA + LANES + D_POOL
X_WIDTH = D_MODEL + LANES
POOL_HALO = 8
BF16_ROWS = 16

TS_PRE = 512
ROWS_ATT = 512
ROW_CHAINS = 2
TK_ATT = 8192
SCORE_TILE = 512 * 4096
UNROLL_ATT = 2
TT_MOE = 256
CH_DISP = 64
CR_COMB = 64
SLOT_RADIX = 64.0
TC_FFN = 1024
PLAN_UNROLL = 4
MIN_NORMAL = float(np.finfo(np.float32).tiny)
AFF_CEIL = 2.0
GEO_STEPS = 32
BISECT_STEPS = GEO_STEPS + 6
VMEM_LIMIT = 56 * 1024 * 1024


def _attn_tiles(seq):
    tk = min(seq, TK_ATT)
    rows = min(ROWS_ATT, SCORE_TILE // tk)
    return ROW_CHAINS * rows, rows, tk


def _cparams(sem):
    return pltpu.CompilerParams(dimension_semantics=sem, vmem_limit_bytes=VMEM_LIMIT)


def _rms(x):
    return x * lax.rsqrt(jnp.mean(x * x, axis=-1, keepdims=True) + EPS)


def _ada_kernel(c_ref, w_ref, b_ref, o_ref):
    c = c_ref[...]
    s = c * jax.nn.sigmoid(c)
    o_ref[...] = jnp.dot(s, w_ref[...], preferred_element_type=F32,
                         precision=lax.Precision.HIGHEST) + b_ref[...]


def _ada(c_pad, w_ada, b_ada):
    rows = c_pad.shape[0]
    return pl.pallas_call(
        _ada_kernel,
        out_shape=jax.ShapeDtypeStruct((rows, N_MOD * D_MODEL), F32),
        grid=(N_MOD,),
        in_specs=[pl.BlockSpec((rows, D_MODEL), lambda j: (0, 0)),
                  pl.BlockSpec((D_MODEL, D_MODEL), lambda j: (0, j)),
                  pl.BlockSpec((1, D_MODEL), lambda j: (0, j))],
        out_specs=pl.BlockSpec((rows, D_MODEL), lambda j: (0, j)),
        compiler_params=_cparams(("arbitrary",)),
        name="ada",
    )(c_pad, w_ada, b_ada)


def _pre_kernel(x_ref, mod_ref, win_ref, qg_ref, kvg_ref, wq_ref, wkv_ref,
                cq_ref, sq_ref, ck_ref, sk_ref, vone_ref, q_ref, k_ref, v_ref, u_ref):
    x = x_ref[0]
    sh1 = mod_ref[0, 0:1, :]
    sc1 = mod_ref[0, 1:2, :]
    h = (_rms(x) * (1.0 + sc1) + sh1).astype(BF16)
    z = jnp.dot(h, win_ref[...], preferred_element_type=F32)
    c_q = z[:, :Q_LORA]
    c_kv = z[:, Q_LORA:Q_LORA + KV_LORA]
    kpe = z[:, Q_LORA + KV_LORA:Q_LORA + KV_LORA + LANES]
    u_ref[0] = z[:, Q_LORA + KV_LORA + LANES:]

    q = jnp.dot((_rms(c_q) * qg_ref[...]).astype(BF16), wq_ref[...],
                preferred_element_type=F32)
    cq_t = cq_ref[...]
    sq_t = sq_ref[...]
    for hd in range(N_HEADS):
        blk = q[:, hd * HEAD_PAD:(hd + 1) * HEAD_PAD]
        rot = pltpu.roll(blk, HEAD_PAD - HALF_ROPE, 1)
        q_ref[0, :, hd * HEAD_PAD:(hd + 1) * HEAD_PAD] = (blk * cq_t + rot * sq_t).astype(BF16)

    kv = jnp.dot((_rms(c_kv) * kvg_ref[...]).astype(BF16), wkv_ref[...],
                 preferred_element_type=F32)
    kpe_r = kpe * ck_ref[...] + pltpu.roll(kpe, HEAD_PAD - HALF_ROPE, 1) * sk_ref[...]
    k_full = jnp.concatenate([kv[:, hd * HEAD_PAD:(hd + 1) * HEAD_PAD] + kpe_r
                              for hd in range(N_HEADS)], axis=-1)
    k_ref[0, 0] = k_full.T.astype(BF16)
    v_ref[0] = (kv[:, N_HEADS * HEAD_PAD:] + vone_ref[...]).astype(BF16)


def _pre(x, mod, win, qg, kvg, wq, wkv, vone, tabs):
    b, s, _ = x.shape
    ts = TS_PRE
    _, _, tk = _attn_tiles(s)
    per_chunk = tk // ts
    assert tk % ts == 0 and s % tk == 0
    const = lambda shape: pl.BlockSpec(shape, lambda bi, i: (0,) * len(shape))
    tab = pl.BlockSpec((ts, LANES), lambda bi, i: (i, 0))
    tok = lambda w: pl.BlockSpec((1, ts, w), lambda bi, i: (bi, i, 0))
    return pl.pallas_call(
        _pre_kernel,
        out_shape=(jax.ShapeDtypeStruct((b, s, N_HEADS * HEAD_PAD), BF16),
                   jax.ShapeDtypeStruct((b, s // tk, N_HEADS * HEAD_PAD, tk), BF16),
                   jax.ShapeDtypeStruct((b, s, N_HEADS * HEAD_PAD), BF16),
                   jax.ShapeDtypeStruct((b, s, D_POOL), F32)),
        grid=(b, s // ts),
        in_specs=[tok(D_MODEL),
                  pl.BlockSpec((1, N_MOD, D_MODEL), lambda bi, i: (bi, 0, 0)),
                  const((D_MODEL, D_IN_PAD)), const((1, Q_LORA)), const((1, KV_LORA)),
                  const((Q_LORA, N_HEADS * HEAD_PAD)),
                  const((KV_LORA, 2 * N_HEADS * HEAD_PAD)),
                  tab, tab, tab, tab, const((1, N_HEADS * HEAD_PAD))],
        out_specs=(tok(N_HEADS * HEAD_PAD),
                   pl.BlockSpec((1, 1, N_HEADS * HEAD_PAD, ts),
                                lambda bi, i: (bi, i // per_chunk, 0, i % per_chunk)),
                   tok(N_HEADS * HEAD_PAD), tok(D_POOL)),
        compiler_params=_cparams(("parallel", "arbitrary")),
        name="pre",
    )(x, mod, win, qg, kvg, wq, wkv, *tabs, vone)


def _attn_kernel(q_ref, k_ref, v_ref, o_ref, *, seq, tq, tk, rows, unroll):
    chains = [(r, hh) for r in range(tq // rows) for hh in range(2)]

    def body(c, carry):
        off = pl.multiple_of(c * tk, tk)
        new = []
        for (r, hh), (m, acc) in zip(chains, carry):
            cols = slice(hh * HEAD_PAD, (hh + 1) * HEAD_PAD)
            qh = q_ref[0, r * rows:(r + 1) * rows, cols]
            s = jnp.dot(qh, k_ref[0, c, cols, :], preferred_element_type=F32)
            m_new = jnp.maximum(m, jnp.max(s, axis=-1, keepdims=True))
            p = jnp.exp2((s - m_new).astype(BF16))
            acc = jnp.exp2(m - m_new) * acc + jnp.dot(p, v_ref[0, pl.ds(off, tk), cols],
                                                      preferred_element_type=F32)
            new.append((m_new, acc))
        return tuple(new)

    init = tuple((jnp.full((rows, 1), -jnp.inf, F32), jnp.zeros((rows, HEAD_PAD), F32))
                 for _ in chains)
    final = lax.fori_loop(0, seq // tk, body, init, unroll=unroll)
    lane = lax.broadcasted_iota(I32, (rows, HEAD_PAD), 1)
    for r in range(tq // rows):
        a0, a1 = final[2 * r][1], final[2 * r + 1][1]
        o_ref[0, r * rows:(r + 1) * rows, :] = jnp.where(
            lane < V_DIM, a0 / a0[:, V_DIM:V_DIM + 1], a1 / a1[:, 0:1]).astype(BF16)


def _attn(q, k, v):
    b, s, _ = q.shape
    tq, rows, tk = _attn_tiles(s)
    assert s % tq == 0 and s % tk == 0
    return pl.pallas_call(
        functools.partial(_attn_kernel, seq=s, tq=tq, tk=tk, rows=rows, unroll=UNROLL_ATT),
        out_shape=jax.ShapeDtypeStruct((b, s, D_ATT), BF16),
        grid=(b, N_HEADS // 2, s // tq),
        in_specs=[pl.BlockSpec((1, tq, 2 * HEAD_PAD), lambda bi, p, i: (bi, i, p)),
                  pl.BlockSpec((1, s // tk, 2 * HEAD_PAD, tk), lambda bi, p, i: (bi, 0, p, 0)),
                  pl.BlockSpec((1, s, 2 * HEAD_PAD), lambda bi, p, i: (bi, 0, p))],
        out_specs=pl.BlockSpec((1, tq, 2 * V_DIM), lambda bi, p, i: (bi, i, p)),
        compiler_params=_cparams(("parallel", "parallel", "arbitrary")),
        name="attn",
    )(q, k, v)


def _mix_kernel(x_ref, o_ref, u_ref, up_ref, un_ref, mod_ref, wp_ref, ps_ref, woa_ref, wob_ref,
                wr_ref, icnt_ref, x1_ref, hx_ref, aff_ref, afft_ref, ue_ref, *, ts):
    i = pl.program_id(1)
    nt = pl.num_programs(1)
    hb = POOL_HALO
    ue_ref[0:hb, :] = jnp.where(i > 0, up_ref[0], 0.0)
    ue_ref[hb:hb + ts, :] = u_ref[0]
    ue_ref[hb + ts:2 * hb + ts, :] = jnp.where(i < nt - 1, un_ref[0], 0.0)

    ys = []
    for g, w in enumerate(POOL_WINDOWS):
        cols = slice(g * POOL_GROUP, (g + 1) * POOL_GROUP)
        acc = None
        for off in range(-(w // 2), w - w // 2):
            sl = ue_ref[hb + off:hb + off + ts, cols]
            acc = sl if acc is None else acc + sl
        d = (acc * icnt_ref[:, g:g + 1] - ue_ref[hb:hb + ts, cols]).astype(BF16)
        y = jnp.dot(d, wp_ref[g], preferred_element_type=F32) * ps_ref[:, cols]
        ys.append(y.astype(BF16))
    pool = jnp.concatenate(ys, axis=-1)
    mixed = (jnp.dot(o_ref[0], woa_ref[...], preferred_element_type=F32)
             + jnp.dot(pool, wob_ref[...], preferred_element_type=F32))
    g1 = mod_ref[0, 2:3, :]
    sh2 = mod_ref[0, 3:4, :]
    sc2 = mod_ref[0, 4:5, :]
    x1 = x_ref[0] + g1 * mixed
    x1_ref[0] = x1
    h2 = _rms(x1) * (1.0 + sc2) + sh2
    hx_ref[0, :, 0:D_MODEL] = h2.astype(BF16)

    h_hi = h2.astype(BF16)
    h_lo = (h2 - h_hi.astype(F32)).astype(BF16)
    both = jnp.dot(h_hi, wr_ref[...], preferred_element_type=F32)
    logits = (both[:, :LANES] + both[:, LANES:]
              + jnp.dot(h_lo, wr_ref[:, :LANES], preferred_element_type=F32))
    lane = lax.broadcasted_iota(I32, (ts, LANES), 1)
    logits = jnp.where(lane < N_EXPERTS, logits, -jnp.inf)
    ex = jnp.exp(logits - jnp.max(logits, axis=-1, keepdims=True))
    aff = ex / jnp.sum(ex, axis=-1, keepdims=True)
    aff_ref[0] = aff
    afft_ref[0] = aff.T[0:N_EXPERTS, :]
    hi_part = aff.astype(BF16).astype(F32)
    lo_part = pltpu.roll(aff - hi_part, N_EXPERTS, 1)
    hx_ref[0, :, D_MODEL:] = jnp.where(lane < N_EXPERTS, hi_part, lo_part).astype(BF16)


def _pool_inv_counts(seq):
    t = np.arange(seq)[:, None]
    w = np.asarray(POOL_WINDOWS)[None, :]
    cnt = np.minimum(t + (w - w // 2), seq) - np.maximum(t - w // 2, 0)
    tab = np.zeros((seq, LANES), np.float64)
    tab[:, :len(POOL_WINDOWS)] = 1.0 / cnt
    return jnp.asarray(tab, dtype=F32)


def _mix(x, o, u, mod, wp, ps, woa, wob, wr):
    b, s, _ = x.shape
    ts = TS_PRE
    hpt = ts // POOL_HALO
    nh = s // POOL_HALO
    const = lambda shape: pl.BlockSpec(shape, lambda bi, i: (0,) * len(shape))
    tok = lambda w: pl.BlockSpec((1, ts, w), lambda bi, i: (bi, i, 0))
    return pl.pallas_call(
        functools.partial(_mix_kernel, ts=ts),
        out_shape=(jax.ShapeDtypeStruct((b, s, D_MODEL), F32),
                   jax.ShapeDtypeStruct((b, s, X_WIDTH), BF16),
                   jax.ShapeDtypeStruct((b, s, LANES), F32),
                   jax.ShapeDtypeStruct((b, N_EXPERTS, s), F32)),
        grid=(b, s // ts),
        in_specs=[tok(D_MODEL), tok(D_ATT), tok(D_POOL),
                  pl.BlockSpec((1, POOL_HALO, D_POOL),
                               lambda bi, i: (bi, jnp.maximum(i * hpt - 1, 0), 0)),
                  pl.BlockSpec((1, POOL_HALO, D_POOL),
                               lambda bi, i: (bi, jnp.minimum((i + 1) * hpt, nh - 1), 0)),
                  pl.BlockSpec((1, N_MOD, D_MODEL), lambda bi, i: (bi, 0, 0)),
                  const((len(POOL_WINDOWS), POOL_GROUP, POOL_GROUP)), const((1, D_POOL)),
                  const((D_ATT, D_MODEL)), const((D_POOL, D_MODEL)), const((D_MODEL, 2 * LANES)),
                  pl.BlockSpec((ts, LANES), lambda bi, i: (i, 0))],
        out_specs=(tok(D_MODEL), tok(X_WIDTH), tok(LANES),
                   pl.BlockSpec((1, N_EXPERTS, ts), lambda bi, i: (bi, 0, i))),
        scratch_shapes=[pltpu.VMEM((ts + 2 * POOL_HALO, D_POOL), F32)],
        compiler_params=_cparams(("parallel", "arbitrary")),
        name="mix",
    )(x, o, u, u, u, mod, wp, ps, woa, wob, wr, _pool_inv_counts(s))


def _plan_kernel(afft_ref, aff_ref, tri_ref, starts_ref, pos_ref, post_ref, split_ref, *, nb, cap,
                 tt):
    def count_ge(cand):
        seq = afft_ref.shape[2]

        def body(bi, acc):
            hit = jnp.where(afft_ref[bi] >= cand, 1.0, 0.0)
            for c0 in range(0, seq, LANES):
                acc = acc + hit[:, c0:c0 + LANES]
            return acc

        lanes = lax.fori_loop(0, nb, body, jnp.zeros((N_EXPERTS, LANES), F32))
        return jnp.sum(lanes, axis=1, keepdims=True)

    def shrink(step, bounds):
        lo, hi = bounds
        mid = jnp.where(step < GEO_STEPS, jnp.sqrt(lo * hi), lo + 0.5 * (hi - lo))
        mid = jnp.clip(mid, lo, hi)
        enough = count_ge(mid) >= cap
        return jnp.where(enough, mid, lo), jnp.where(enough, hi, mid)

    lo_col, hi_col = lax.fori_loop(0, BISECT_STEPS, shrink,
                                   (jnp.full((N_EXPERTS, 1), MIN_NORMAL, F32),
                                    jnp.full((N_EXPERTS, 1), AFF_CEIL, F32)))
    tiny = count_ge(jnp.full((N_EXPERTS, 1), MIN_NORMAL, F32)) < cap
    lo_col = jnp.where(tiny, -1.0, lo_col)
    hi_col = jnp.where(tiny, MIN_NORMAL, hi_col)
    need_col = cap - count_ge(hi_col)

    sub = lax.broadcasted_iota(I32, (N_EXPERTS, LANES), 0)
    lane = lax.broadcasted_iota(I32, (N_EXPERTS, LANES), 1)
    real = lane[0:1, :] < N_EXPERTS

    def to_lanes(col, pad):
        row = jnp.sum(jnp.where(sub == lane, col, 0.0), axis=0, keepdims=True)
        return jnp.where(real, row, pad)

    lo = to_lanes(lo_col, AFF_CEIL)
    hi = to_lanes(hi_col, AFF_CEIL)
    need = to_lanes(need_col, 0.0)

    def tile_body(t, carry):
        sel_before, eq_before = carry
        off = pl.multiple_of(t * tt, tt)
        v = aff_ref[pl.ds(off, tt), :]
        eq = (v >= lo) & (v < hi)
        eq_f = jnp.where(eq, 1.0, 0.0)
        tri = tri_ref[...]
        rank = jnp.dot(tri, eq_f.astype(BF16), preferred_element_type=F32) + eq_before
        sel = (v >= hi) | (eq & (rank < need))
        sel_f = jnp.where(sel, 1.0, 0.0)
        pos = jnp.dot(tri, sel_f.astype(BF16), preferred_element_type=F32) + sel_before
        posm = jnp.where(sel, pos, -1.0)
        starts_ref[t] = jnp.broadcast_to(sel_before.astype(I32), (8, LANES))
        pos_ref[t] = posm
        post_ref[t] = posm.T[0:N_EXPERTS, :]
        slot1 = posm + 1.0
        high = jnp.floor(slot1 * (1.0 / SLOT_RADIX))
        split_ref[t] = jnp.concatenate([high, slot1 - SLOT_RADIX * high], axis=-1).astype(BF16)
        return (sel_before + jnp.sum(sel_f, axis=0, keepdims=True),
                eq_before + jnp.sum(eq_f, axis=0, keepdims=True))

    nt = pos_ref.shape[0]
    zero = jnp.zeros((1, LANES), F32)
    sel_total, _ = lax.fori_loop(0, nt, tile_body, (zero, zero), unroll=PLAN_UNROLL)
    starts_ref[nt] = jnp.broadcast_to(sel_total.astype(I32), (8, LANES))


def _plan(afft, aff2d, tri, cap):
    nb = afft.shape[0]
    n_tok = aff2d.shape[0]
    tt = TT_MOE
    nt = n_tok // tt
    vm = pl.BlockSpec(memory_space=pltpu.VMEM)
    return pl.pallas_call(
        functools.partial(_plan_kernel, nb=nb, cap=cap, tt=tt),
        out_shape=(jax.ShapeDtypeStruct((nt + 1, 8, LANES), I32),
                   jax.ShapeDtypeStruct((nt, tt, LANES), F32),
                   jax.ShapeDtypeStruct((nt, N_EXPERTS, tt), F32),
                   jax.ShapeDtypeStruct((nt, tt, 2 * LANES), BF16)),
        in_specs=[vm, vm, vm],
        out_specs=(vm, vm, vm, vm),
        compiler_params=pltpu.CompilerParams(vmem_limit_bytes=VMEM_LIMIT),
        name="plan",
    )(afft, aff2d, tri)


def _disp_copy(stage, xe_ref, sems, e, row, src_e=None):
    src_e = e if src_e is None else src_e
    return pltpu.make_async_copy(stage.at[pl.ds(src_e * CH_DISP, CH_DISP)],
                                 xe_ref.at[e, pl.ds(row, CH_DISP)], sems.at[e])


def _disp_kernel(starts_s, hx_ref, post_ref, xe_ref, stages, carry, sems, *, cap, nt, tt):
    t = pl.program_id(0)
    stage = stages.at[lax.rem(t, 2)]

    @pl.when(t == 0)
    def _():
        carry[...] = jnp.zeros(carry.shape, carry.dtype)
        idle = stages.at[1]
        idle[0:CH_DISP, :] = jnp.zeros((CH_DISP, X_WIDTH), BF16)
        for e in range(N_EXPERTS):
            _disp_copy(idle, xe_ref, sems, e, cap, src_e=0).start()

    pos_t = post_ref[0]
    slot_iota = lax.broadcasted_iota(I32, (CH_DISP, tt), 0).astype(F32)

    def window(e, base):
        onehot = jnp.where(pos_t[e:e + 1, :] - base.astype(F32) == slot_iota, 1.0, 0.0)
        return onehot.astype(BF16)

    def carry_rows(e, base):
        off = (starts_s[t + 1, e] & -BF16_ROWS) - base
        inside = jnp.minimum(off, CH_DISP - BF16_ROWS)
        blk = stage[pl.ds(pl.multiple_of(e * CH_DISP + inside, BF16_ROWS), BF16_ROWS), :]
        return jnp.where(off < CH_DISP, blk, jnp.zeros_like(blk))

    bases = [starts_s[t, e] & -BF16_ROWS for e in range(N_EXPERTS)]
    onehot = jnp.concatenate([window(e, bases[e]) for e in range(N_EXPERTS)], axis=0)
    stage[...] = jnp.dot(onehot, hx_ref[...], preferred_element_type=F32).astype(BF16)
    for e in range(N_EXPERTS):
        head = pl.ds(e * CH_DISP, BF16_ROWS)
        stage[head, :] = (stage[head, :].astype(F32) + carry[e].astype(F32)).astype(BF16)
    for e in range(N_EXPERTS):
        _disp_copy(stage, xe_ref, sems, e, 0).wait()
        _disp_copy(stage, xe_ref, sems, e, pl.multiple_of(bases[e], BF16_ROWS)).start()
        carry[e] = carry_rows(e, bases[e])

    n_win = [(starts_s[t + 1, e] - bases[e] + (CH_DISP - 1)) // CH_DISP
             for e in range(N_EXPERTS)]
    most = functools.reduce(jnp.maximum, n_win)

    @pl.when(most > 1)
    def _():
        for e in range(N_EXPERTS):
            def more(c, _, e=e):
                cb = pl.multiple_of(bases[e] + c * CH_DISP, BF16_ROWS)
                extra = jnp.dot(window(e, cb), hx_ref[...], preferred_element_type=F32)
                _disp_copy(stage, xe_ref, sems, e, 0).wait()
                stage[pl.ds(e * CH_DISP, CH_DISP), :] = extra.astype(BF16)
                _disp_copy(stage, xe_ref, sems, e, cb).start()
                return 0

            lax.fori_loop(1, n_win[e], more, 0)

            @pl.when(n_win[e] > 1)
            def _(e=e):
                carry[e] = carry_rows(e, bases[e] + (n_win[e] - 1) * CH_DISP)

    @pl.when(t == nt - 1)
    def _():
        for e in range(N_EXPERTS):
            _disp_copy(stage, xe_ref, sems, e, 0).wait()


def _disp(starts_s, hx2d, post, cap):
    n_tok = hx2d.shape[0]
    tt = TT_MOE
    nt = n_tok // tt
    grid_spec = pltpu.PrefetchScalarGridSpec(
        num_scalar_prefetch=1,
        grid=(nt,),
        in_specs=[pl.BlockSpec((tt, X_WIDTH), lambda t, s: (t, 0)),
                  pl.BlockSpec((1, N_EXPERTS, tt), lambda t, s: (t, 0, 0))],
        out_specs=pl.BlockSpec(memory_space=pl.ANY),
        scratch_shapes=[pltpu.VMEM((2, N_EXPERTS * CH_DISP, X_WIDTH), BF16),
                        pltpu.VMEM((N_EXPERTS, BF16_ROWS, X_WIDTH), BF16),
                        pltpu.SemaphoreType.DMA((N_EXPERTS,))])
    return pl.pallas_call(
        functools.partial(_disp_kernel, cap=cap, nt=nt, tt=tt),
        out_shape=jax.ShapeDtypeStruct((N_EXPERTS, cap + CH_DISP, X_WIDTH), BF16),
        grid_spec=grid_spec,
        compiler_params=_cparams(("arbitrary",)),
        name="disp",
    )(starts_s, hx2d, post)


def _ffn_kernel(*refs, blocks):
    n_grp = len(blocks)
    xe_refs = refs[:n_grp]
    wg_ref, wu_ref, wd_ref, ye_ref, wg_bf, wu_bf, wd_bf = refs[n_grp:]
    e = pl.program_id(0)
    j = pl.program_id(1)

    @pl.when(j == 0)
    def _():
        wg_bf[...] = wg_ref[0].astype(BF16)
        wu_bf[...] = wu_ref[0].astype(BF16)
        wd_bf[...] = wd_ref[0].astype(BF16)
        ye_ref[...] = jnp.zeros(ye_ref.shape, ye_ref.dtype)

    def experts(xe_ref):
        blk = xe_ref[0]
        x = blk[:, 0:D_MODEL]
        gcols = blk[:, D_MODEL:].astype(F32)
        lane = lax.broadcasted_iota(I32, gcols.shape, 1)
        gate = jnp.sum(jnp.where((lane == e) | (lane == e + N_EXPERTS), gcols, 0.0),
                       axis=-1, keepdims=True)
        a = jnp.dot(x, wg_bf[...], preferred_element_type=F32)
        u = jnp.dot(x, wu_bf[...], preferred_element_type=F32)
        hmid = (a * jax.nn.sigmoid(a) * u).astype(BF16)
        y = jnp.dot(hmid, wd_bf[...], preferred_element_type=F32) * gate
        ye_ref[0] = y.astype(BF16)

    for g, (first, count) in enumerate(blocks):
        pl.when((j > first) & (j <= first + count))(functools.partial(experts, xe_refs[g]))


def _ffn(xes, caps, wg, wu, wd):
    tc = TC_FFN
    blocks, first = [], 0
    for cap in caps:
        blocks.append((first, cap // tc))
        first += cap // tc
    wspec = pl.BlockSpec((1, D_MODEL, EXPERT_FF), lambda e, j: (e, 0, 0))

    def slot_spec(first, count):
        return pl.BlockSpec((1, tc, X_WIDTH),
                            lambda e, j: (e, jnp.clip(j - 1 - first, 0, count - 1), 0))

    return pl.pallas_call(
        functools.partial(_ffn_kernel, blocks=tuple(blocks)),
        out_shape=jax.ShapeDtypeStruct((N_EXPERTS, (first + 1) * tc, D_MODEL), BF16),
        grid=(N_EXPERTS, first + 1),
        in_specs=[slot_spec(f, c) for f, c in blocks] + [wspec, wspec,
                  pl.BlockSpec((1, EXPERT_FF, D_MODEL), lambda e, j: (e, 0, 0))],
        out_specs=pl.BlockSpec((1, tc, D_MODEL),
                               lambda e, j: (e, jnp.where(j == 0, first, j - 1), 0)),
        scratch_shapes=[pltpu.VMEM((D_MODEL, EXPERT_FF), BF16), pltpu.VMEM((D_MODEL, EXPERT_FF), BF16),
                        pltpu.VMEM((EXPERT_FF, D_MODEL), BF16)],
        compiler_params=_cparams(("parallel", "arbitrary")),
        name="ffn",
    )(*xes, wg, wu, wd)


def _comb_copy(ye_ref, buf, sems, slot, e, row):
    return pltpu.make_async_copy(ye_ref.at[e, pl.ds(row, CR_COMB)],
                                 buf.at[slot, pl.ds(e * CR_COMB, CR_COMB)], sems.at[slot])


def _comb_kernel(starts_s, x1_ref, pos_ref, split_ref, spread_ref, mod_ref, fg_ref, ye_ref, y_ref,
                 buf, xbuf, acc_ref, sems, xsem, *, row0, nt, tt):
    t = pl.program_id(0)
    slot = lax.rem(t, 2)

    def issue(tile, sl):
        for e in range(N_EXPERTS):
            b0 = pl.multiple_of(starts_s[tile, e] & -BF16_ROWS, BF16_ROWS)
            _comb_copy(ye_ref, buf, sems, sl, e, row0 + b0).start()

    def wait_all(sl):
        pltpu.make_async_copy(ye_ref.at[0, pl.ds(0, N_EXPERTS * CR_COMB)], buf.at[sl],
                              sems.at[sl]).wait()

    @pl.when(t == 0)
    def _():
        issue(0, 0)

    issue(jnp.minimum(t + 1, nt - 1), 1 - slot)

    posm = pos_ref[0]
    bases = [starts_s[t, e] & -BF16_ROWS for e in range(N_EXPERTS)]
    slot1 = jnp.dot(split_ref[0], spread_ref[...], preferred_element_type=F32)
    col = lax.broadcasted_iota(I32, (1, N_EXPERTS * CR_COMB), 1)
    want = (col & (CR_COMB - 1)) + 1
    for e in range(N_EXPERTS):
        want = want + jnp.where(col // CR_COMB == e, bases[e], 0)
    onehot = jnp.where(slot1 == want.astype(F32), 1.0, 0.0).astype(BF16)

    wait_all(slot)
    acc_ref[...] = jnp.dot(onehot, buf[slot], preferred_element_type=F32)

    n_win = [(starts_s[t + 1, e] - bases[e] + (CR_COMB - 1)) // CR_COMB
             for e in range(N_EXPERTS)]
    most = functools.reduce(jnp.maximum, n_win)

    @pl.when(most > 1)
    def _():
        win_iota = lax.broadcasted_iota(I32, (tt, CR_COMB), 1).astype(F32)
        for e in range(N_EXPERTS):
            def extra(c, _, e=e):
                cb = pl.multiple_of(bases[e] + c * CR_COMB, BF16_ROWS)
                cp = pltpu.make_async_copy(ye_ref.at[e, pl.ds(row0 + cb, CR_COMB)], xbuf,
                                           xsem.at[0])
                cp.start()
                cp.wait()
                rel = posm[:, e:e + 1] - cb.astype(F32)
                oh = jnp.where(rel == win_iota, 1.0, 0.0).astype(BF16)
                acc_ref[...] += jnp.dot(oh, xbuf[...], preferred_element_type=F32)
                return 0

            lax.fori_loop(1, n_win[e], extra, 0)

    g2 = mod_ref[0, 5:6, :]
    y_ref[...] = _rms(x1_ref[...] + g2 * acc_ref[...]) * fg_ref[...]

    @pl.when(t == nt - 1)
    def _():
        wait_all(1 - slot)


def _slot_spread():
    w = np.zeros((2 * LANES, N_EXPERTS * CR_COMB), np.float32)
    for e in range(N_EXPERTS):
        w[e, e * CR_COMB:(e + 1) * CR_COMB] = SLOT_RADIX
        w[LANES + e, e * CR_COMB:(e + 1) * CR_COMB] = 1.0
    return jnp.asarray(w, dtype=BF16)


def _comb(starts_s, x1_2d, pos, split, mod, fg, ye, row0, seq):
    n_tok = x1_2d.shape[0]
    tt = TT_MOE
    nt = n_tok // tt
    tiles_per_seq = seq // tt
    grid_spec = pltpu.PrefetchScalarGridSpec(
        num_scalar_prefetch=1,
        grid=(nt,),
        in_specs=[pl.BlockSpec((tt, D_MODEL), lambda t, s: (t, 0)),
                  pl.BlockSpec((1, tt, LANES), lambda t, s: (t, 0, 0)),
                  pl.BlockSpec((1, tt, 2 * LANES), lambda t, s: (t, 0, 0)),
                  pl.BlockSpec((2 * LANES, N_EXPERTS * CR_COMB), lambda t, s: (0, 0)),
                  pl.BlockSpec((1, N_MOD, D_MODEL), lambda t, s: (t // tiles_per_seq, 0, 0)),
                  pl.BlockSpec((1, D_MODEL), lambda t, s: (0, 0)),
                  pl.BlockSpec(memory_space=pl.ANY)],
        out_specs=pl.BlockSpec((tt, D_MODEL), lambda t, s: (t, 0)),
        scratch_shapes=[pltpu.VMEM((2, N_EXPERTS * CR_COMB, D_MODEL), BF16),
                        pltpu.VMEM((CR_COMB, D_MODEL), BF16),
                        pltpu.VMEM((tt, D_MODEL), F32),
                        pltpu.SemaphoreType.DMA((2,)),
                        pltpu.SemaphoreType.DMA((1,))])
    return pl.pallas_call(
        functools.partial(_comb_kernel, row0=row0, nt=nt, tt=tt),
        out_shape=jax.ShapeDtypeStruct((n_tok, D_MODEL), F32),
        grid_spec=grid_spec,
        compiler_params=_cparams(("arbitrary",)),
        name="comb",
    )(starts_s, x1_2d, pos, split, _slot_spread(), mod, fg, ye)


def _head_tile_cols(base, nope):
    r = base + nope
    return (list(range(base, base + nope)) + list(range(r, r + QK_ROPE))
            + list(range(r, r + HALF_ROPE)))


def _prep_weights(w_in, w_uq, w_ukv):
    zcol = lambda w, n: jnp.zeros((w.shape[0], n), w.dtype)
    kpe0 = Q_LORA + KV_LORA
    kpe_tile = jnp.concatenate(
        [zcol(w_in, QK_NOPE), w_in[:, kpe0:kpe0 + QK_ROPE], w_in[:, kpe0:kpe0 + HALF_ROPE],
         zcol(w_in, HEAD_PAD - QK_NOPE - QK_ROPE - HALF_ROPE)], axis=1)
    win = jnp.concatenate([w_in[:, :kpe0], kpe_tile, w_in[:, kpe0 + QK_ROPE:]], axis=1)

    q_tiles, k_tiles, v_cols = [], [], []
    for hd in range(N_HEADS):
        cols = np.asarray(_head_tile_cols(hd * (QK_NOPE + QK_ROPE), QK_NOPE))
        q_tiles += [w_uq[:, cols], zcol(w_uq, HEAD_PAD - len(cols))]
        kb = hd * (QK_NOPE + V_DIM)
        k_tiles += [w_ukv[:, kb:kb + QK_NOPE], zcol(w_ukv, HEAD_PAD - QK_NOPE)]
        w_v = w_ukv[:, kb + QK_NOPE:kb + QK_NOPE + V_DIM]
        pad = zcol(w_ukv, HEAD_PAD - V_DIM)
        v_cols += [w_v, pad] if hd % 2 == 0 else [pad, w_v]
    wq = jnp.concatenate(q_tiles, axis=1)
    wkv = jnp.concatenate(k_tiles + v_cols, axis=1)
    lane = np.arange(N_HEADS * HEAD_PAD)
    ones_here = ((lane % HEAD_PAD) >= V_DIM) == ((lane // HEAD_PAD) % 2 == 0)
    vone = jnp.asarray(ones_here.astype(np.float32))[None, :]
    return win.astype(BF16), wq.astype(BF16), wkv.astype(BF16), vone


def _rope_tabs(seq):
    inv = ROPE_THETA ** (-jnp.arange(0, QK_ROPE, 2, dtype=F32) / QK_ROPE)
    ang = jnp.arange(seq, dtype=F32)[:, None] * inv[None, :]
    cos, sin = jnp.cos(ang), jnp.sin(ang)
    one = jnp.ones((seq, QK_NOPE), F32)
    z_nope = jnp.zeros((seq, QK_NOPE), F32)
    z_pad = jnp.zeros((seq, HEAD_PAD - QK_NOPE - QK_ROPE), F32)
    scale = (QK_NOPE + QK_ROPE) ** -0.5 * float(np.log2(np.e))
    c_full = jnp.concatenate([one, cos, cos, z_pad], axis=1)
    s_full = jnp.concatenate([z_nope, -sin, sin, z_pad], axis=1)
    c_rope = jnp.concatenate([z_nope, cos, cos, z_pad], axis=1)
    return c_full * scale, s_full * scale, c_rope, s_full


def _tri(tt):
    r = np.arange(tt)
    return jnp.asarray((r[None, :] < r[:, None]).astype(np.float32), dtype=BF16)


def kernel(x_prompt, x_sample, c_prompt, c_sample, w_ada, b_ada, w_in, q_norm_g, kv_norm_g,
           w_uq, w_ukv, w_pool, pool_scale, w_out, w_router, w_gate, w_up, w_down, final_norm_g):
    groups = ((x_prompt, c_prompt), (x_sample, c_sample))
    n_c = sum(c.shape[0] for _, c in groups)
    c_all = jnp.concatenate([c for _, c in groups]
                            + [jnp.zeros((-n_c % 8, D_MODEL), F32)], axis=0)
    mod_all = _ada(c_all, w_ada[0], b_ada)

    win, wq, wkv, vone = _prep_weights(w_in[0], w_uq[0], w_ukv[0])
    qg = q_norm_g[0][None, :]
    kvg = kv_norm_g[0][None, :]
    wp = w_pool[0].astype(BF16)
    ps = pool_scale[0][None, :]
    woa = w_out[0][:D_ATT].astype(BF16)
    wob = w_out[0][D_ATT:].astype(BF16)
    wr_pad = jnp.concatenate([w_router[0], jnp.zeros((D_MODEL, LANES - N_EXPERTS), F32)], axis=1)
    wr_hi = wr_pad.astype(BF16)
    wr = jnp.concatenate([wr_hi, (wr_pad - wr_hi.astype(F32)).astype(BF16)], axis=1)
    wg, wu, wd = w_gate[0], w_up[0], w_down[0]
    fg = final_norm_g[None, :]
    tri = _tri(TT_MOE)

    caps = [CAPACITY_FACTOR * x.shape[0] * x.shape[1] // N_EXPERTS for x, _ in groups]
    assert all(c % TC_FFN == 0 for c in caps)
    xes = []

    staged = []
    c_off = 0
    row0 = 0
    for (x, c), cap in zip(groups, caps):
        b, s, _ = x.shape
        mod = mod_all[c_off:c_off + b].reshape(b, N_MOD, D_MODEL)
        c_off += b
        q, k, v, u = _pre(x, mod, win, qg, kvg, wq, wkv, vone, _rope_tabs(s))
        o = _attn(q, k, v)
        x1, hx, aff, afft = _mix(x, o, u, mod, wp, ps, woa, wob, wr)
        n_tok = b * s
        starts_v, pos, post, split = _plan(afft, aff.reshape(n_tok, LANES), tri, cap)
        starts_s = starts_v[:, 0, :N_EXPERTS]
        xes.append(_disp(starts_s, hx.reshape(n_tok, X_WIDTH), post, cap))
        staged.append((x1.reshape(n_tok, D_MODEL), pos, split, starts_s, mod, row0, (b, s)))
        row0 += cap

    ye = _ffn(xes, caps, wg, wu, wd)

    outs = []
    for x1, pos, split, starts_s, mod, r0, (b, s) in staged:
        y = _comb(starts_s, x1, pos, split, mod, fg, ye, r0, s)
        outs.append(y.reshape(b, s, D_MODEL))
    return tuple(outs)
```
